```python
import math
import jax, jax.numpy as jnp
from jax import lax
import numpy as np

D_MODEL = 2048
BATCH = 1
SEQ = 8192
DEPTH = 4

Q_BLOCK = 128
MLA_HEADS = 8
MLA_Q_LORA = 512
MLA_KV_LORA = 256
MLA_NOPE = 128
MLA_ROPE = 64
MLA_V = 128
ROPE_THETA = 10000.0
DIFF_HEADS = 8
DIFF_QK = 64
DIFF_V = 2 * DIFF_QK
NSA_HEADS = 8
NSA_KV_HEADS = 2
NSA_DH = 128
CMP_BLOCK = 32
CMP_STRIDE = 16
SLC_BLOCK = 64
N_SELECT = 16
WINDOW = 512
SLC_PER_CMP = SLC_BLOCK // CMP_STRIDE
CMP_LPAD = CMP_BLOCK // CMP_STRIDE - 1
N_OVERLAP = SLC_PER_CMP + CMP_BLOCK // CMP_STRIDE - 1
N_ALIBI_HEADS = DIFF_HEADS + NSA_HEADS
N_BRANCH = 3
BRANCH_WIDTH = 1024
D_FF = 4 * D_MODEL
N_MOD = 6
ALPHA = (2.0 * DEPTH) ** 0.25
BETA = (8.0 * DEPTH) ** -0.25
SPLITS = (MLA_Q_LORA, MLA_KV_LORA, MLA_ROPE,
          DIFF_HEADS * 2 * DIFF_QK, DIFF_HEADS * 2 * DIFF_QK, DIFF_HEADS * DIFF_V,
          NSA_HEADS * NSA_DH, 3 * 2 * NSA_KV_HEADS * NSA_DH, NSA_HEADS * 3,
          N_BRANCH * D_MODEL)
D_IN = sum(SPLITS)
SPLIT_POINTS = tuple(int(v) for v in np.cumsum(SPLITS)[:-1])

kernel_name = "hybrid_mla_diff_nsa_deepnorm_adaln"


def layer_norm_plain(x, eps=1e-5):
    xf = x.astype(jnp.float32)
    mu = jnp.mean(xf, -1, keepdims=True)
    var = jnp.mean(jnp.square(xf - mu), -1, keepdims=True)
    return ((xf - mu) * lax.rsqrt(var + eps)).astype(x.dtype)


def layer_norm(x, g, b, eps=1e-5):
    return layer_norm_plain(x, eps) * g + b


def rms_norm(x, g, eps=1e-6):
    xf = x.astype(jnp.float32)
    y = xf * lax.rsqrt(jnp.mean(jnp.square(xf), -1, keepdims=True) + eps)
    return y.astype(x.dtype) * g


def modulate(x, shift, scale):
    return layer_norm_plain(x) * (1 + scale) + shift


def masked_softmax(s, mask):
    s = jnp.where(mask, s, -jnp.inf)
    m = jnp.max(s, -1, keepdims=True)
    m = jnp.where(jnp.isfinite(m), m, 0.0)
    e = jnp.exp(s - m)
    d = jnp.sum(e, -1, keepdims=True)
    return e / jnp.where(d > 0, d, 1.0)


def alibi_slopes():
    return 2.0 ** (-8.0 * jnp.arange(1, N_ALIBI_HEADS + 1, dtype=jnp.float32) / N_ALIBI_HEADS)


def dist(a, b):
    return jnp.abs(a - b).astype(jnp.float32)


def rope(x, pos):
    d = x.shape[-1]
    inv = ROPE_THETA ** (-jnp.arange(0, d, 2, dtype=jnp.float32) / d)
    ang = pos.astype(jnp.float32)[:, None, :, None] * inv
    cos, sin = jnp.cos(ang).astype(x.dtype), jnp.sin(ang).astype(x.dtype)
    x1, x2 = x[..., : d // 2], x[..., d // 2:]
    return jnp.concatenate([x1 * cos - x2 * sin, x1 * sin + x2 * cos], -1)


def seq_blocks(x, axis):
    s = x.shape[axis]
    x = x.reshape(x.shape[:axis] + (s // Q_BLOCK, Q_BLOCK) + x.shape[axis + 1:])
    return jnp.moveaxis(x, axis, 0)


def merge_blocks(o, axis):
    o = jnp.moveaxis(o, 0, axis)
    return o.reshape(o.shape[:axis] + (o.shape[axis] * o.shape[axis + 1],) + o.shape[axis + 2:])


def mla_attention(c_q, c_kv, k_r, pos, w_uq, w_ukv, g_q, g_kv):
    b, s, _ = c_q.shape
    q = (rms_norm(c_q, g_q) @ w_uq).reshape(b, s, MLA_HEADS, MLA_NOPE + MLA_ROPE).transpose(0, 2, 1, 3)
    kv = (rms_norm(c_kv, g_kv) @ w_ukv).reshape(b, s, MLA_HEADS, MLA_NOPE + MLA_V).transpose(0, 2, 1, 3)
    q_nope, q_rope = q[..., :MLA_NOPE], rope(q[..., MLA_NOPE:], pos)
    k_nope, v = kv[..., :MLA_NOPE], kv[..., MLA_NOPE:]
    k_rope = rope(k_r[:, None], pos)[:, 0]
    scale = (MLA_NOPE + MLA_ROPE) ** -0.5
    kidx = jnp.arange(s)

    def block(args):
        qn, qr, q0 = args
        sc = (jnp.einsum('bhqd,bhkd->bhqk', qn, k_nope)
              + jnp.einsum('bhqd,bkd->bhqk', qr, k_rope)).astype(jnp.float32) * scale
        qidx = q0 + jnp.arange(Q_BLOCK)
        p = masked_softmax(sc, kidx[None, :] <= qidx[:, None])
        return jnp.einsum('bhqk,bhkd->bhqd', p.astype(v.dtype), v)

    q0s = jnp.arange(s // Q_BLOCK, dtype=jnp.int32) * Q_BLOCK
    o = merge_blocks(lax.map(block, (seq_blocks(q_nope, 2), seq_blocks(q_rope, 2), q0s)), 2)
    return o.transpose(0, 2, 1, 3).reshape(b, s, MLA_HEADS * MLA_V)


def diff_attention(q, k, v, pos, slopes, lam_params, g_sub, layer_idx):
    b, s, _ = q.shape
    q = q.reshape(b, s, DIFF_HEADS, 2, DIFF_QK).transpose(0, 2, 3, 1, 4)
    k = k.reshape(b, s, DIFF_HEADS, 2, DIFF_QK).transpose(0, 2, 3, 1, 4)
    v = v.reshape(b, s, DIFF_HEADS, DIFF_V).transpose(0, 2, 1, 3)
    lam_init = 0.8 - 0.6 * math.exp(-0.3 * layer_idx)
    lp = lam_params.astype(jnp.float32)
    lam = jnp.exp(jnp.sum(lp[0] * lp[1])) - jnp.exp(jnp.sum(lp[2] * lp[3])) + lam_init
    scale = DIFF_QK ** -0.5
    kidx = jnp.arange(s)
    sl = slopes[None, :, None, None, None]

    def block(args):
        qb, pq, q0 = args
        sc = jnp.einsum('bhiqd,bhikd->bhiqk', qb, k).astype(jnp.float32) * scale
        sc = sc - sl * dist(pq[:, None, None, :, None], pos[:, None, None, None, :])
        qidx = q0 + jnp.arange(Q_BLOCK)
        p = masked_softmax(sc, kidx[None, :] <= qidx[:, None])
        a = p[:, :, 0] - lam * p[:, :, 1]
        return jnp.einsum('bhqk,bhkd->bhqd', a.astype(v.dtype), v)

    q0s = jnp.arange(s // Q_BLOCK, dtype=jnp.int32) * Q_BLOCK
    o = merge_blocks(lax.map(block, (seq_blocks(q, 3), seq_blocks(pos, 1), q0s)), 2)
    o = rms_norm(o, g_sub) * (1.0 - lam_init)
    return o.transpose(0, 2, 1, 3).reshape(b, s, DIFF_HEADS * DIFF_V)


def compress(x, w1, b1, w2):
    b, g, s, d = x.shape
    half = x.reshape(b, g, s // CMP_STRIDE, CMP_STRIDE, d)
    blocks = jnp.concatenate([half[:, :, :-1], half[:, :, 1:]], axis=3)
    flat = blocks.reshape(b, g, s // CMP_STRIDE - 1, CMP_BLOCK * d)
    return jax.nn.gelu(flat @ w1 + b1) @ w2


def nsa_attention(q, kv, gate_logits, pos, slopes, w1, b1, w2):
    b, s, _ = q.shape
    G, R, dh = NSA_KV_HEADS, NSA_HEADS // NSA_KV_HEADS, NSA_DH
    q = q.reshape(b, s, G, R, dh).transpose(0, 2, 3, 1, 4)
    kv = kv.reshape(b, s, 3, 2, G, dh).transpose(2, 3, 0, 4, 1, 5)
    gates = jax.nn.sigmoid(gate_logits).reshape(b, s, G, R, 3).transpose(0, 2, 3, 1, 4)
    scale = dh ** -0.5
    sl = slopes.reshape(1, G, R, 1, 1)
    k_cmp = compress(kv[0, 0], w1[0], b1[0], w2[0])
    v_cmp = compress(kv[0, 1], w1[1], b1[1], w2[1])
    n_cmp = k_cmp.shape[2]
    cmp_end = jnp.arange(n_cmp) * CMP_STRIDE + (CMP_BLOCK - 1)
    pos_cmp = pos[:, CMP_BLOCK - 1::CMP_STRIDE]
    n_slc = s // SLC_BLOCK
    top = min(N_SELECT, n_slc)
    k_slc = kv[1, 0].reshape(b, G, n_slc, SLC_BLOCK, dh)
    v_slc = kv[1, 1].reshape(b, G, n_slc, SLC_BLOCK, dh)
    pos_slc = pos.reshape(b, n_slc, SLC_BLOCK)
    bi = jnp.arange(b)[:, None, None, None]
    gi = jnp.arange(G)[None, :, None, None]
    in_blk = jnp.arange(SLC_BLOCK)
    blk = jnp.arange(n_slc)
    pad = ((0, 0), (0, 0), (WINDOW, 0), (0, 0))
    k_win, v_win = jnp.pad(kv[2, 0], pad), jnp.pad(kv[2, 1], pad)
    pos_win = jnp.pad(pos, ((0, 0), (WINDOW, 0)))

    def block(args):
        qb, gb, pq, q0 = args
        qidx = q0 + jnp.arange(Q_BLOCK)
        pq5 = pq[:, None, None, :, None]
        sc = jnp.einsum('bgrqd,bgnd->bgrqn', qb, k_cmp).astype(jnp.float32) * scale
        sc = sc - sl * dist(pq5, pos_cmp[:, None, None, None, :])
        p_cmp = masked_softmax(sc, cmp_end[None, :] <= qidx[:, None])
        o_cmp = jnp.einsum('bgrqn,bgnd->bgrqd', p_cmp.astype(v_cmp.dtype), v_cmp)
        imp = jnp.pad(jnp.sum(p_cmp, 2), ((0, 0), (0, 0), (0, 0), (CMP_LPAD, N_OVERLAP)))
        span = SLC_PER_CMP * (n_slc - 1) + 1
        imp = sum(imp[..., o:o + span:SLC_PER_CMP] for o in range(N_OVERLAP))
        cur = qidx // SLC_BLOCK
        forced = (blk[None, :] == 0) | (blk[None, :] == cur[:, None]) | (blk[None, :] == cur[:, None] - 1)
        causal_blk = blk[None, :] * SLC_BLOCK <= qidx[:, None]
        score = jnp.where(forced, jnp.inf, jnp.where(causal_blk, imp, -jnp.inf))
        topv, idx = lax.top_k(score, top)
        blk_ok = topv > -jnp.inf
        kg = k_slc[bi, gi, idx].reshape(b, G, Q_BLOCK, top * SLC_BLOCK, dh)
        vg = v_slc[bi, gi, idx].reshape(b, G, Q_BLOCK, top * SLC_BLOCK, dh)
        pk = pos_slc[bi, idx].reshape(b, G, Q_BLOCK, top * SLC_BLOCK)
        tok = (idx[..., None] * SLC_BLOCK + in_blk).reshape(b, G, Q_BLOCK, top * SLC_BLOCK)
        ok = jnp.repeat(blk_ok, SLC_BLOCK, -1) & (tok <= qidx[None, None, :, None])
        sc = jnp.einsum('bgrqd,bgqkd->bgrqk', qb, kg).astype(jnp.float32) * scale
        sc = sc - sl * dist(pq5, pk[:, :, None])
        p = masked_softmax(sc, ok[:, :, None])
        o_slc = jnp.einsum('bgrqk,bgqkd->bgrqd', p.astype(vg.dtype), vg)
        kw = lax.dynamic_slice_in_dim(k_win, q0, WINDOW + Q_BLOCK, axis=2)
        vw = lax.dynamic_slice_in_dim(v_win, q0, WINDOW + Q_BLOCK, axis=2)
        pw = lax.dynamic_slice_in_dim(pos_win, q0, WINDOW + Q_BLOCK, axis=1)
        kidx = q0 - WINDOW + jnp.arange(WINDOW + Q_BLOCK)
        okw = (kidx[None, :] <= qidx[:, None]) & (kidx[None, :] > qidx[:, None] - WINDOW) & (kidx[None, :] >= 0)
        sc = jnp.einsum('bgrqd,bgkd->bgrqk', qb, kw).astype(jnp.float32) * scale
        sc = sc - sl * dist(pq5, pw[:, None, None, None, :])
        p = masked_softmax(sc, okw)
        o_win = jnp.einsum('bgrqk,bgkd->bgrqd', p.astype(vw.dtype), vw)
        return gb[..., 0:1] * o_cmp + gb[..., 1:2] * o_slc + gb[..., 2:3] * o_win

    q0s = jnp.arange(s // Q_BLOCK, dtype=jnp.int32) * Q_BLOCK
    o = lax.map(block, (seq_blocks(q, 3), seq_blocks(gates, 3), seq_blocks(pos, 1), q0s))
    o = merge_blocks(o, 3).reshape(b, NSA_HEADS, s, dh)
    return o.transpose(0, 2, 1, 3).reshape(b, s, NSA_HEADS * dh)


def setup_inputs(seed: int = 0) -> dict:
    key = jax.random.key(seed)
    ks = jax.random.split(key, 24)
    f32 = jnp.float32
    L, D = DEPTH, D_MODEL

    def nrm(k, shape, scale):
        return jax.random.normal(k, shape, f32) * scale

    def gain(k, shape):
        return 1.0 + 0.02 * jax.random.normal(k, shape, f32)

    return {
        "x": nrm(ks[0], (BATCH, SEQ, D), 1.0),
        "c": nrm(ks[1], (BATCH, D), 1.0),
        "positions": jnp.broadcast_to(jnp.arange(SEQ, dtype=jnp.int32), (BATCH, SEQ)),
        "w_ada": nrm(ks[2], (L, D, N_MOD * D), 0.5 * D ** -0.5),
        "b_ada": nrm(ks[3], (L, N_MOD * D), 0.02),
        "w_in": nrm(ks[4], (L, D, D_IN), D ** -0.5),
        "mla_q_norm": gain(ks[5], (L, MLA_Q_LORA)),
        "mla_kv_norm": gain(ks[6], (L, MLA_KV_LORA)),
        "mla_w_uq": nrm(ks[7], (L, MLA_Q_LORA, MLA_HEADS * (MLA_NOPE + MLA_ROPE)), MLA_Q_LORA ** -0.5),
        "mla_w_ukv": nrm(ks[8], (L, MLA_KV_LORA, MLA_HEADS * (MLA_NOPE + MLA_V)), MLA_KV_LORA ** -0.5),
        "diff_lambda": nrm(ks[9], (L, 4, DIFF_QK), 0.1),
        "diff_subln": gain(ks[10], (L, DIFF_V)),
        "nsa_cmp_w1": nrm(ks[11], (L, 2, CMP_BLOCK * NSA_DH, NSA_DH), (CMP_BLOCK * NSA_DH) ** -0.5),
        "nsa_cmp_b1": nrm(ks[12], (L, 2, NSA_DH), 0.02),
        "nsa_cmp_w2": nrm(ks[13], (L, 2, NSA_DH, NSA_DH), NSA_DH ** -0.5),
        "w_branch": nrm(ks[14], (L, N_BRANCH, BRANCH_WIDTH, D), BETA * BRANCH_WIDTH ** -0.5),
        "w_out": nrm(ks[15], (L, D, D), BETA * D ** -0.5),
        "ln1_g": gain(ks[16], (L, D)),
        "ln1_b": nrm(ks[17], (L, D), 0.02),
        "w_ff1": nrm(ks[18], (L, D, D_FF), D ** -0.5),
        "w_ff2": nrm(ks[19], (L, D_FF, D), BETA * D_FF ** -0.5),
        "ln2_g": gain(ks[20], (L, D)),
        "ln2_b": nrm(ks[21], (L, D), 0.02),
    }


def reference(x, c, positions, w_ada, b_ada, w_in, mla_q_norm, mla_kv_norm, mla_w_uq, mla_w_ukv,
              diff_lambda, diff_subln, nsa_cmp_w1, nsa_cmp_b1, nsa_cmp_w2, w_branch, w_out,
              ln1_g, ln1_b, w_ff1, w_ff2, ln2_g, ln2_b):
    b, s, _ = x.shape
    slopes = alibi_slopes()
    s_diff, s_nsa = slopes[0::2], slopes[1::2]
    c_act = jax.nn.silu(c)
    for l in range(DEPTH):
        mod = (c_act @ w_ada[l] + b_ada[l])[:, None, :]
        sh_a, sc_a, g_a, sh_m, sc_m, g_m = jnp.split(mod, N_MOD, -1)
        h = modulate(x, sh_a, sc_a)
        z = h @ w_in[l]
        cq, ckv, kr, dq, dk, dv, nq, nkv, ngate, mgate = jnp.split(z, SPLIT_POINTS, -1)
        o_mla = mla_attention(cq, ckv, kr, positions, mla_w_uq[l], mla_w_ukv[l], mla_q_norm[l], mla_kv_norm[l])
        o_diff = diff_attention(dq, dk, dv, positions, s_diff, diff_lambda[l], diff_subln[l], l)
        o_nsa = nsa_attention(nq, nkv, ngate, positions, s_nsa, nsa_cmp_w1[l], nsa_cmp_b1[l], nsa_cmp_w2[l])
        branches = jnp.stack([o_mla, o_diff, o_nsa], 2)
        proj = jnp.einsum('bsnf,nfd->bsnd', branches, w_branch[l])
        gate = jax.nn.sigmoid(mgate).reshape(b, s, N_BRANCH, D_MODEL)
        attn = jnp.sum(gate * proj, 2) @ w_out[l]
        x = layer_norm(ALPHA * x + g_a * attn, ln1_g[l], ln1_b[l])
        h = modulate(x, sh_m, sc_m)
        ff = jnp.square(jax.nn.relu(h @ w_ff1[l])) @ w_ff2[l]
        x = layer_norm(ALPHA * x + g_m * ff, ln2_g[l], ln2_b[l])
    return x
```

```python
import functools
import math

import numpy as np
import jax
import jax.numpy as jnp
from jax import lax
from jax.experimental import pallas as pl
from jax.experimental.pallas import tpu as pltpu

F32 = jnp.float32
BF16 = jnp.bfloat16

D_MODEL = 2048
MLA_HEADS = 8
MLA_Q_LORA = 512
MLA_KV_LORA = 256
MLA_NOPE = 128
MLA_ROPE = 64
MLA_V = 128
ROPE_THETA = 10000.0
DIFF_HEADS = 8
DIFF_QK = 64
DIFF_V = 128
NSA_HEADS = 8
NSA_KV_HEADS = 2
NSA_REP = NSA_HEADS // NSA_KV_HEADS
NSA_DH = 128
CMP_BLOCK = 32
CMP_STRIDE = 16
SLC_BLOCK = 64
N_SELECT = 16
WINDOW = 512
N_ALIBI_HEADS = DIFF_HEADS + NSA_HEADS
N_BRANCH = 3
BRANCH_WIDTH = 1024
D_FF = 4 * D_MODEL
N_MOD = 6
DEPTH = 4
ALPHA = (2.0 * DEPTH) ** 0.25

LANE = 128
NEG = -1e30
VMEM_LIMIT = 48 * 1024 * 1024

Z_CQ = 0
Z_DQ = 512
Z_DK = 1536
Z_DV = 2560
Z_NQ = 3584
Z_NKV = 4608
Z_MG = 6144
Z_CKV = 12288
Z_KR = 12544
Z_NG = 12672
Z_W = 12800


def _cparams(sem):
    return pltpu.CompilerParams(dimension_semantics=sem, vmem_limit_bytes=VMEM_LIMIT)


def _rot_half_cols(base):
    half = MLA_ROPE // 2
    cols = np.concatenate([base + half + np.arange(half), base + np.arange(half)])
    sign = np.concatenate([-np.ones(half), np.ones(half)])
    return cols, sign


def _w_in_layout():
    o_cq, o_ckv, o_kr, o_dq, o_dk, o_dv, o_nq, o_nkv, o_ng, o_mg = np.cumsum(
        [0, 512, 256, 64, 1024, 1024, 1024, 1024, 1536, 24])
    cols = np.zeros(Z_W, np.int32)
    sign = np.zeros(Z_W, np.float32)

    def put(dst, src, n):
        cols[dst:dst + n] = src + np.arange(n)
        sign[dst:dst + n] = 1.0

    put(Z_CQ, o_cq, 512)
    put(Z_DQ, o_dq, 1024)
    put(Z_DK, o_dk, 1024)
    put(Z_DV, o_dv, 1024)
    put(Z_NQ, o_nq, 1024)
    put(Z_NKV, o_nkv, 1536)
    put(Z_MG, o_mg, N_BRANCH * D_MODEL)
    put(Z_CKV, o_ckv, 256)
    put(Z_KR, o_kr, 64)
    rc, rs = _rot_half_cols(o_kr)
    cols[Z_KR + 64:Z_KR + 128] = rc
    sign[Z_KR + 64:Z_KR + 128] = rs
    put(Z_NG, o_ng, 24)
    return cols, sign


def _w_uq_layout():
    per = MLA_NOPE + MLA_ROPE
    cols = np.zeros(MLA_HEADS * 256, np.int32)
    sign = np.ones(MLA_HEADS * 256, np.float32)
    for h in range(MLA_HEADS):
        cols[h * 256:h * 256 + per] = h * per + np.arange(per)
        rc, rs = _rot_half_cols(h * per + MLA_NOPE)
        cols[h * 256 + per:h * 256 + 256] = rc
        sign[h * 256 + per:h * 256 + 256] = rs
    return cols, sign


def _ada_kernel(c_ref, w_ref, b_ref, o_ref):
    c = c_ref[...]
    ca = c * jax.nn.sigmoid(c)
    ca8 = jnp.broadcast_to(ca, (8, c.shape[1])).astype(BF16)
    r = jnp.dot(ca8, w_ref[...].astype(BF16), preferred_element_type=F32)
    o_ref[...] = r[0:1, :] + b_ref[...]


def _ada(c, w_ada, b_ada):
    L, D, N = w_ada.shape
    tn = 1536
    return pl.pallas_call(
        _ada_kernel,
        out_shape=jax.ShapeDtypeStruct((L, 1, N), F32),
        grid=(L, N // tn),
        in_specs=[pl.BlockSpec((1, D), lambda l, j: (0, 0)),
                  pl.BlockSpec((None, D, tn), lambda l, j: (l, 0, j)),
                  pl.BlockSpec((None, 1, tn), lambda l, j: (l, 0, j))],
        out_specs=pl.BlockSpec((None, 1, tn), lambda l, j: (l, 0, j)),
        compiler_params=_cparams(("arbitrary", "arbitrary")),
        name="ada_mod",
    )(c, w_ada, b_ada.reshape(L, 1, N))


def _lnmod_kernel(x_ref, sh_ref, sc_ref, o_ref):
    x = x_ref[...]
    mu = jnp.mean(x, -1, keepdims=True)
    xc = x - mu
    var = jnp.mean(xc * xc, -1, keepdims=True)
    y = xc * lax.rsqrt(var + 1e-5)
    o_ref[...] = (y * (1.0 + sc_ref[...]) + sh_ref[...]).astype(o_ref.dtype)


def _lnmod(x, mod, l, k_shift, k_scale):
    S, D = x.shape
    tm = 512
    return pl.pallas_call(
        _lnmod_kernel,
        out_shape=jax.ShapeDtypeStruct((S, D), BF16),
        grid=(S // tm,),
        in_specs=[pl.BlockSpec((tm, D), lambda i: (i, 0)),
                  pl.BlockSpec((None, 1, D), lambda i: (l, 0, k_shift)),
                  pl.BlockSpec((None, 1, D), lambda i: (l, 0, k_scale))],
        out_specs=pl.BlockSpec((tm, D), lambda i: (i, 0)),
        compiler_params=_cparams(("arbitrary",)),
        name="ln_modulate",
    )(x, mod, mod)


def _mm_kernel(x_ref, w_ref, o_ref, *, relu2):
    r = jnp.dot(x_ref[...], w_ref[...], preferred_element_type=F32)
    if relu2:
        r = jnp.square(jnp.maximum(r, 0.0))
    o_ref[...] = r.astype(o_ref.dtype)


def _matmul(x, w, l, *, tm, tn, relu2, name):
    S, K = x.shape
    N = w.shape[-1]
    return pl.pallas_call(
        functools.partial(_mm_kernel, relu2=relu2),
        out_shape=jax.ShapeDtypeStruct((S, N), BF16),
        grid=(S // tm, N // tn),
        in_specs=[pl.BlockSpec((tm, K), lambda i, j: (i, 0)),
                  pl.BlockSpec((None, K, tn), lambda i, j: (l, 0, j))],
        out_specs=pl.BlockSpec((tm, tn), lambda i, j: (i, j)),
        compiler_params=_cparams(("arbitrary", "arbitrary")),
        name=name,
    )(x, w)


def _mm_res_ln_kernel(a_ref, w_ref, x_ref, gate_ref, lng_ref, lnb_ref, o_ref, *, nk, alpha):
    k = pl.program_id(1)
    part = jnp.dot(a_ref[...], w_ref[...], preferred_element_type=F32)

    @pl.when(k == 0)
    def _():
        o_ref[...] = part

    @pl.when(k > 0)
    def _():
        o_ref[...] += part

    @pl.when(k == nk - 1)
    def _():
        y = alpha * x_ref[...] + gate_ref[...] * o_ref[...]
        mu = jnp.mean(y, -1, keepdims=True)
        yc = y - mu
        var = jnp.mean(yc * yc, -1, keepdims=True)
        o_ref[...] = yc * lax.rsqrt(var + 1e-5) * lng_ref[...] + lnb_ref[...]


def _matmul_res_ln(a, w, x, mod, lng, lnb, l, k_gate, alpha, name):
    S, K = a.shape
    D = x.shape[1]
    L = lng.shape[0]
    tm, tk = 512, 512
    nk = K // tk
    return pl.pallas_call(
        functools.partial(_mm_res_ln_kernel, nk=nk, alpha=alpha),
        out_shape=jax.ShapeDtypeStruct((S, D), F32),
        grid=(S // tm, nk),
        in_specs=[pl.BlockSpec((tm, tk), lambda i, k: (i, k)),
                  pl.BlockSpec((None, tk, D), lambda i, k: (l, k, 0)),
                  pl.BlockSpec((tm, D), lambda i, k: (i, 0)),
                  pl.BlockSpec((None, 1, D), lambda i, k: (l, 0, k_gate)),
                  pl.BlockSpec((None, 1, D), lambda i, k: (l, 0, 0)),
                  pl.BlockSpec((None, 1, D), lambda i, k: (l, 0, 0))],
        out_specs=pl.BlockSpec((tm, D), lambda i, k: (i, 0)),
        compiler_params=_cparams(("arbitrary", "arbitrary")),
        name=name,
    )(a, w, x, mod, lng.reshape(L, 1, D), lnb.reshape(L, 1, D))


def _merge_kernel(o0_ref, o1_ref, o2_ref, w_ref, gate_ref, out_ref, acc_ref):
    n = pl.program_id(2)
    gate = jax.nn.sigmoid(gate_ref[...].astype(F32))

    def contrib(o_ref):
        return gate * jnp.dot(o_ref[...], w_ref[...], preferred_element_type=F32)

    @pl.when(n == 0)
    def _():
        acc_ref[...] = contrib(o0_ref)

    @pl.when(n == 1)
    def _():
        acc_ref[...] += contrib(o1_ref)

    @pl.when(n == 2)
    def _():
        out_ref[...] = (acc_ref[...] + contrib(o2_ref)).astype(out_ref.dtype)


def _merge(o_mla, o_diff, o_nsa, w_branch, z, l):
    S = o_mla.shape[0]
    tm, tn = 1024, 512
    bspec = pl.BlockSpec((tm, BRANCH_WIDTH), lambda i, j, n: (i, 0))
    return pl.pallas_call(
        _merge_kernel,
        out_shape=jax.ShapeDtypeStruct((S, D_MODEL), BF16),
        grid=(S // tm, D_MODEL // tn, N_BRANCH),
        in_specs=[bspec, bspec, bspec,
                  pl.BlockSpec((None, None, BRANCH_WIDTH, tn), lambda i, j, n: (l, n, 0, j)),
                  pl.BlockSpec((tm, tn), lambda i, j, n: (i, (Z_MG + D_MODEL * n) // tn + j))],
        out_specs=pl.BlockSpec((tm, tn), lambda i, j, n: (i, j)),
        scratch_shapes=[pltpu.VMEM((tm, tn), F32)],
        compiler_params=_cparams(("arbitrary", "arbitrary", "arbitrary")),
        name="branch_merge",
    )(o_mla, o_diff, o_nsa, w_branch, z)


def _rope_kernel(pos_ref, inv_ref, o_ref):
    ang = pos_ref[...] * inv_ref[...]
    lane = lax.broadcasted_iota(jnp.int32, ang.shape, 1)
    o_ref[...] = jnp.where(lane < MLA_ROPE, jnp.cos(ang), jnp.sin(ang))


def _rope_table(pos_col):
    S = pos_col.shape[0]
    inv = ROPE_THETA ** (-np.arange(0, MLA_ROPE, 2, dtype=np.float64) / MLA_ROPE)
    inv4 = jnp.asarray(np.tile(inv, 4)[None, :], F32)
    tm = 512
    return pl.pallas_call(
        _rope_kernel,
        out_shape=jax.ShapeDtypeStruct((S, LANE), F32),
        grid=(S // tm,),
        in_specs=[pl.BlockSpec((tm, 1), lambda i: (i, 0)),
                  pl.BlockSpec((1, LANE), lambda i: (0, 0))],
        out_specs=pl.BlockSpec((tm, LANE), lambda i: (i, 0)),
        compiler_params=_cparams(("arbitrary",)),
        name="rope_table",
    )(pos_col, inv4)


def _rms(x, g):
    return x * lax.rsqrt(jnp.mean(x * x, -1, keepdims=True) + 1e-6) * g


def _mla_proj_kernel(cq_ref, ckv_ref, kr_ref, t_ref, gq_ref, gkv_ref, wuq_ref, wukv_ref,
                     q_out, k_out, v_out):
    scale = (MLA_NOPE + MLA_ROPE) ** -0.5
    t = t_ref[...]
    nq = _rms(cq_ref[...].astype(F32), gq_ref[...]).astype(BF16)
    q = jnp.dot(nq, wuq_ref[...], preferred_element_type=F32)
    for h in range(MLA_HEADS):
        c = 256 * h
        q_out[:, c:c + 128] = (q[:, c:c + 128] * scale).astype(BF16)
        q_out[:, c + 128:c + 256] = (q[:, c + 128:c + 256] * t * scale).astype(BF16)
    nkv = _rms(ckv_ref[...].astype(F32), gkv_ref[...]).astype(BF16)
    kv = jnp.dot(nkv, wukv_ref[...], preferred_element_type=F32)
    kr = kr_ref[...].astype(F32) * t
    kr2 = (kr + pltpu.roll(kr, MLA_ROPE, 1)).astype(BF16)
    for h in range(MLA_HEADS):
        c = 256 * h
        k_out[:, c:c + 128] = kv[:, c:c + 128].astype(BF16)
        k_out[:, c + 128:c + 256] = kr2
        v_out[:, 128 * h:128 * h + 128] = kv[:, c + 128:c + 256].astype(BF16)


def _mla_proj(z, rope_t, gq, gkv, wuq, wukv, l):
    S = z.shape[0]
    L = gq.shape[0]
    tm = 512
    HW = MLA_HEADS * 256
    return pl.pallas_call(
        _mla_proj_kernel,
        out_shape=(jax.ShapeDtypeStruct((S, HW), BF16),
                   jax.ShapeDtypeStruct((S, HW), BF16),
                   jax.ShapeDtypeStruct((S, MLA_HEADS * MLA_V), BF16)),
        grid=(S // tm,),
        in_specs=[pl.BlockSpec((tm, MLA_Q_LORA), lambda i: (i, Z_CQ // MLA_Q_LORA)),
                  pl.BlockSpec((tm, MLA_KV_LORA), lambda i: (i, Z_CKV // MLA_KV_LORA)),
                  pl.BlockSpec((tm, LANE), lambda i: (i, Z_KR // LANE)),
                  pl.BlockSpec((tm, LANE), lambda i: (i, 0)),
                  pl.BlockSpec((None, 1, MLA_Q_LORA), lambda i: (l, 0, 0)),
                  pl.BlockSpec((None, 1, MLA_KV_LORA), lambda i: (l, 0, 0)),
                  pl.BlockSpec((None, MLA_Q_LORA, HW), lambda i: (l, 0, 0)),
                  pl.BlockSpec((None, MLA_KV_LORA, HW), lambda i: (l, 0, 0))],
        out_specs=(pl.BlockSpec((tm, HW), lambda i: (i, 0)),
                   pl.BlockSpec((tm, HW), lambda i: (i, 0)),
                   pl.BlockSpec((tm, MLA_HEADS * MLA_V), lambda i: (i, 0))),
        compiler_params=_cparams(("arbitrary",)),
        name="mla_proj",
    )(z, z, z, rope_t, gq.reshape(L, 1, -1), gkv.reshape(L, 1, -1), wuq, wukv)


def _qk(q, k):
    return lax.dot_general(q, k, (((1,), (1,)), ((), ())), preferred_element_type=F32)


def _online(s, v, m, l, acc, valid=None):
    m_new = jnp.maximum(m, jnp.max(s, -1, keepdims=True))
    a = jnp.exp(m - m_new)
    p = jnp.exp(s - m_new)
    if valid is not None:
        p = jnp.where(valid, p, 0.0)
    l = a * l + jnp.sum(p, -1, keepdims=True)
    acc = a * acc + jnp.dot(p.astype(BF16), v, preferred_element_type=F32)
    return m_new, l, acc


def _init(rows, dv):
    return (jnp.full((rows, 1), NEG, F32), jnp.zeros((rows, 1), F32), jnp.zeros((rows, dv), F32))


def _mla_attn_kernel(q_ref, k_ref, v_ref, o_ref, *, t):
    i = pl.program_id(1)
    q = q_ref[...]

    def tile(j, carry, diag):
        off = pl.multiple_of(j * t, t)
        s = _qk(q, k_ref[pl.ds(off, t), :])
        if diag:
            row = lax.broadcasted_iota(jnp.int32, (t, t), 0)
            col = lax.broadcasted_iota(jnp.int32, (t, t), 1)
            s = jnp.where(col <= row, s, NEG)
        return _online(s, v_ref[pl.ds(off, t), :], *carry)

    carry = lax.fori_loop(0, i, lambda j, c: tile(j, c, False), _init(t, MLA_V))
    _, l, acc = tile(i, carry, True)
    o_ref[...] = (acc / l).astype(o_ref.dtype)


def _mla_attn(qm, km, vm):
    S = qm.shape[0]
    t = 512
    return pl.pallas_call(
        functools.partial(_mla_attn_kernel, t=t),
        out_shape=jax.ShapeDtypeStruct((S, MLA_HEADS * MLA_V), BF16),
        grid=(MLA_HEADS, S // t),
        in_specs=[pl.BlockSpec((t, 256), lambda h, i: (i, h)),
                  pl.BlockSpec((S, 256), lambda h, i: (0, h)),
                  pl.BlockSpec((S, MLA_V), lambda h, i: (0, h))],
        out_specs=pl.BlockSpec((t, MLA_V), lambda h, i: (i, h)),
        compiler_params=_cparams(("arbitrary", "arbitrary")),
        name="mla_attn",
    )(qm, km, vm)


def _diff_attn_kernel(sc_ref, q_ref, k_ref, v_ref, pq_ref, pk_ref, lam_ref, g_ref, o_ref, *, t):
    h = pl.program_id(0)
    i = pl.program_id(1)
    slope = sc_ref[h]
    lam_init = sc_ref[DIFF_HEADS]
    q = q_ref[...]
    lane = lax.broadcasted_iota(jnp.int32, q.shape, 1)
    zero = jnp.zeros_like(q)
    qs = q * jnp.asarray(DIFF_QK ** -0.5, BF16)
    q0 = jnp.where(lane < DIFF_QK, qs, zero)
    q1 = jnp.where(lane >= DIFF_QK, qs, zero)
    pq = pq_ref[...]

    def tile(j, carry, diag):
        c0, c1 = carry
        off = pl.multiple_of(j * t, t)
        k = k_ref[pl.ds(off, t), :]
        v = v_ref[pl.ds(off, t), :]
        bias = slope * jnp.abs(pq - pk_ref[:, pl.ds(off, t)])
        s0 = _qk(q0, k) - bias
        s1 = _qk(q1, k) - bias
        if diag:
            row = lax.broadcasted_iota(jnp.int32, (t, t), 0)
            col = lax.broadcasted_iota(jnp.int32, (t, t), 1)
            ok = col <= row
            s0 = jnp.where(ok, s0, NEG)
            s1 = jnp.where(ok, s1, NEG)
        return _online(s0, v, *c0), _online(s1, v, *c1)

    carry = lax.fori_loop(0, i, lambda j, c: tile(j, c, False), (_init(t, DIFF_V), _init(t, DIFF_V)))
    (_, l0, a0), (_, l1, a1) = tile(i, carry, True)
    lp = lam_ref[...]
    lam = (jnp.exp(jnp.sum(lp[0:1] * lp[1:2], -1, keepdims=True))
           - jnp.exp(jnp.sum(lp[2:3] * lp[3:4], -1, keepdims=True)) + lam_init)
    o = a0 / l0 - lam * (a1 / l1)
    o_ref[...] = (_rms(o, g_ref[...]) * (1.0 - lam_init)).astype(o_ref.dtype)


def _diff_attn(z, pos_col, pos_row, scal, lam_p, g_sub, l):
    S = z.shape[0]
    L = g_sub.shape[0]
    t = 512
    return pl.pallas_call(
        functools.partial(_diff_attn_kernel, t=t),
        out_shape=jax.ShapeDtypeStruct((S, DIFF_HEADS * DIFF_V), BF16),
        grid=(DIFF_HEADS, S // t),
        in_specs=[pl.BlockSpec(memory_space=pltpu.SMEM),
                  pl.BlockSpec((t, LANE), lambda h, i: (i, Z_DQ // LANE + h)),
                  pl.BlockSpec((S, LANE), lambda h, i: (0, Z_DK // LANE + h)),
                  pl.BlockSpec((S, LANE), lambda h, i: (0, Z_DV // LANE + h)),
                  pl.BlockSpec((t, 1), lambda h, i: (i, 0)),
                  pl.BlockSpec((1, S), lambda h, i: (0, 0)),
                  pl.BlockSpec((None, 4, DIFF_QK), lambda h, i: (l, 0, 0)),
                  pl.BlockSpec((None, 1, DIFF_V), lambda h, i: (l, 0, 0))],
        out_specs=pl.BlockSpec((t, DIFF_V), lambda h, i: (i, h)),
        compiler_params=_cparams(("arbitrary", "arbitrary")),
        name="diff_attn",
    )(scal, z, z, z, pos_col, pos_row, lam_p, g_sub.reshape(L, 1, -1))


def _compress_kernel(x_ref, w1_ref, b1_ref, w2_ref, o_ref):
    x = x_ref[...]
    half = CMP_STRIDE * NSA_DH
    a = jnp.dot(x, w1_ref[0:half, :].astype(BF16), preferred_element_type=F32)
    b = jnp.dot(x, w1_ref[half:2 * half, :].astype(BF16), preferred_element_type=F32)
    n = x.shape[0]
    pre = a + pltpu.roll(b, n - 1, 0) + b1_ref[...]
    hdn = 0.5 * pre * (1.0 + jnp.tanh(0.7978845608028654 * (pre + 0.044715 * pre * pre * pre)))
    o_ref[...] = jnp.dot(hdn.astype(BF16), w2_ref[...].astype(BF16),
                         preferred_element_type=F32).astype(o_ref.dtype)


def _compress(xc, w1, b1, w2, l):
    _, nb, kk = xc.shape
    L = w1.shape[0]
    return pl.pallas_call(
        _compress_kernel,
        out_shape=jax.ShapeDtypeStruct((4, nb, NSA_DH), BF16),
        grid=(4,),
        in_specs=[pl.BlockSpec((None, nb, kk), lambda c: (c, 0, 0)),
                  pl.BlockSpec((None, None, 2 * kk, NSA_DH), lambda c: (l, c // 2, 0, 0)),
                  pl.BlockSpec((None, None, 1, NSA_DH), lambda c: (l, c // 2, 0, 0)),
                  pl.BlockSpec((None, None, NSA_DH, NSA_DH), lambda c: (l, c // 2, 0, 0))],
        out_specs=pl.BlockSpec((None, nb, NSA_DH), lambda c: (c, 0, 0)),
        compiler_params=_cparams(("arbitrary",)),
        name="nsa_compress",
    )(xc, w1, b1.reshape(L, 2, 1, NSA_DH), w2)


def _split3(x):
    hi = x.astype(BF16)
    r = x - hi.astype(F32)
    mid = r.astype(BF16)
    lo = (r - mid.astype(F32)).astype(BF16)
    return hi, mid, lo


def _cmp_sel_kernel(sc_ref, q_ref, kc_ref, vc_ref, pq_ref, pc_ref, o_ref, sel_ref, *, tq, n_slc):
    g = pl.program_id(0)
    i = pl.program_id(1)
    nc = kc_ref.shape[0]
    q = q_ref[...] * jnp.asarray(NSA_DH ** -0.5, BF16)
    kc = kc_ref[...]
    vc = vc_ref[...]
    pq = pq_ref[...]
    qidx = i * tq + lax.broadcasted_iota(jnp.int32, (tq, nc), 0)
    cmp_end = lax.broadcasted_iota(jnp.int32, (tq, nc), 1) * CMP_STRIDE + (CMP_BLOCK - 1)
    vis = cmp_end <= qidx
    dist = jnp.abs(pq - pc_ref[...])
    p_sum = jnp.zeros((tq, nc), F32)
    for r in range(NSA_REP):
        slope = sc_ref[g * NSA_REP + r]
        s = _qk(q[:, r * NSA_DH:(r + 1) * NSA_DH], kc) - slope * dist
        s = jnp.where(vis, s, NEG)
        m = jnp.max(s, -1, keepdims=True)
        e = jnp.where(vis, jnp.exp(s - m), 0.0)
        d = jnp.sum(e, -1, keepdims=True)
        p = e / jnp.where(d > 0.0, d, 1.0)
        p_sum = p_sum + p
        o_ref[:, r * NSA_DH:(r + 1) * NSA_DH] = jnp.dot(
            p.astype(BF16), vc, preferred_element_type=F32).astype(o_ref.dtype)
    nn = lax.broadcasted_iota(jnp.int32, (nc, n_slc), 0)
    jj = lax.broadcasted_iota(jnp.int32, (nc, n_slc), 1)
    lo_n = jj * (SLC_BLOCK // CMP_STRIDE) - 1
    ovl = jnp.where((nn >= lo_n) & (nn <= lo_n + 4), 1.0, 0.0).astype(BF16)
    imp = sum(jnp.dot(part, ovl, preferred_element_type=F32) for part in _split3(p_sum))
    blk = lax.broadcasted_iota(jnp.int32, (tq, n_slc), 1).astype(F32)
    cur = ((i * tq + lax.broadcasted_iota(jnp.int32, (tq, n_slc), 0)) // SLC_BLOCK).astype(F32)
    forced = (blk == 0) | (blk == cur) | (blk == cur - 1)
    causal = blk <= cur
    score = jnp.where(forced, 3e38, jnp.where(causal, imp, -1.0))
    sel = jnp.zeros((tq, n_slc), F32)
    for _ in range(min(N_SELECT, n_slc)):
        mx = jnp.max(score, -1, keepdims=True)
        first = jnp.min(jnp.where(score == mx, blk, float(n_slc)), -1, keepdims=True)
        pick = blk == first
        sel = jnp.where(pick, 1.0, sel)
        score = jnp.where(pick, -2.0, score)
    sel_ref[...] = jnp.where(causal, sel, 0.0).astype(sel_ref.dtype)


def _cmp_sel(z, cmp_kv, pos_col, pos_cmp, scal):
    S = z.shape[0]
    nc = cmp_kv.shape[1]
    n_slc = S // SLC_BLOCK
    tq = 256
    gw = NSA_REP * NSA_DH
    return pl.pallas_call(
        functools.partial(_cmp_sel_kernel, tq=tq, n_slc=n_slc),
        out_shape=(jax.ShapeDtypeStruct((S, NSA_HEADS * NSA_DH), BF16),
                   jax.ShapeDtypeStruct((NSA_KV_HEADS, S, n_slc), BF16)),
        grid=(NSA_KV_HEADS, S // tq),
        in_specs=[pl.BlockSpec(memory_space=pltpu.SMEM),
                  pl.BlockSpec((tq, gw), lambda g, i: (i, Z_NQ // gw + g)),
                  pl.BlockSpec((None, nc, NSA_DH), lambda g, i: (g, 0, 0)),
                  pl.BlockSpec((None, nc, NSA_DH), lambda g, i: (2 + g, 0, 0)),
                  pl.BlockSpec((tq, 1), lambda g, i: (i, 0)),
                  pl.BlockSpec((1, nc), lambda g, i: (0, 0))],
        out_specs=(pl.BlockSpec((tq, gw), lambda g, i: (i, g)),
                   pl.BlockSpec((None, tq, n_slc), lambda g, i: (g, i, 0))),
        compiler_params=_cparams(("arbitrary", "arbitrary")),
        name="nsa_cmp_select",
    )(scal, z, cmp_kv, cmp_kv, pos_col, pos_cmp)


def _slc_attn_kernel(sc_ref, q_ref, k_ref, v_ref, sel_ref, pq_ref, pk_ref, o_ref, *, tq, tk):
    g = pl.program_id(0)
    i = pl.program_id(1)
    n_slc = sel_ref.shape[1]
    bpt = tk // SLC_BLOCK
    q = q_ref[...] * jnp.asarray(NSA_DH ** -0.5, BF16)
    sel = sel_ref[...]
    pq = pq_ref[...]
    slopes = [sc_ref[g * NSA_REP + r] for r in range(NSA_REP)]
    row = i * tq + lax.broadcasted_iota(jnp.int32, (tq, tk), 0)
    col0 = lax.broadcasted_iota(jnp.int32, (tq, tk), 1)
    eb = lax.broadcasted_iota(jnp.int32, (n_slc, tk), 0)
    ec = lax.broadcasted_iota(jnp.int32, (n_slc, tk), 1) // SLC_BLOCK

    def tile(j, carry):
        off = pl.multiple_of(j * tk, tk)
        k = k_ref[pl.ds(off, tk), :]
        v = v_ref[pl.ds(off, tk), :]
        expand = jnp.where(eb == ec + j * bpt, 1.0, 0.0).astype(BF16)
        ok = (jnp.dot(sel, expand, preferred_element_type=F32) > 0.5) & (col0 + off <= row)
        dist = jnp.abs(pq - pk_ref[:, pl.ds(off, tk)])
        out = []
        for r in range(NSA_REP):
            s = _qk(q[:, r * NSA_DH:(r + 1) * NSA_DH], k) - slopes[r] * dist
            out.append(_online(jnp.where(ok, s, NEG), v, *carry[r]))
        return tuple(out)

    n_tiles = (i * tq + tq + tk - 1) // tk
    carry = lax.fori_loop(0, n_tiles, tile, tuple(_init(tq, NSA_DH) for _ in range(NSA_REP)))
    for r in range(NSA_REP):
        _, l, acc = carry[r]
        o_ref[:, r * NSA_DH:(r + 1) * NSA_DH] = (acc / l).astype(o_ref.dtype)


def _slc_attn(z, sel, pos_col, pos_row, scal):
    S = z.shape[0]
    n_slc = S // SLC_BLOCK
    tq, tk = 256, 512
    gw = NSA_REP * NSA_DH
    kcol = Z_NKV // LANE + 4
    return pl.pallas_call(
        functools.partial(_slc_attn_kernel, tq=tq, tk=tk),
        out_shape=jax.ShapeDtypeStruct((S, NSA_HEADS * NSA_DH), BF16),
        grid=(NSA_KV_HEADS, S // tq),
        in_specs=[pl.BlockSpec(memory_space=pltpu.SMEM),
                  pl.BlockSpec((tq, gw), lambda g, i: (i, Z_NQ // gw + g)),
                  pl.BlockSpec((S, NSA_DH), lambda g, i: (0, kcol + g)),
                  pl.BlockSpec((S, NSA_DH), lambda g, i: (0, kcol + 2 + g)),
                  pl.BlockSpec((None, tq, n_slc), lambda g, i: (g, i, 0)),
                  pl.BlockSpec((tq, 1), lambda g, i: (i, 0)),
                  pl.BlockSpec((1, S), lambda g, i: (0, 0))],
        out_specs=pl.BlockSpec((tq, gw), lambda g, i: (i, g)),
        compiler_params=_cparams(("arbitrary", "arbitrary")),
        name="nsa_slc_attn",
    )(scal, z, z, z, sel, pos_col, pos_row)


def _win_attn_kernel(sc_ref, q_ref, k_ref, v_ref, pq_ref, pk_ref, ocmp_ref, oslc_ref, gate_ref,
                     o_ref, *, t):
    g = pl.program_id(0)
    i = pl.program_id(1)
    q = q_ref[...] * jnp.asarray(NSA_DH ** -0.5, BF16)
    pq = pq_ref[...]
    slopes = [sc_ref[g * NSA_REP + r] for r in range(NSA_REP)]
    dif0 = (lax.broadcasted_iota(jnp.int32, (t, t), 0) - lax.broadcasted_iota(jnp.int32, (t, t), 1))

    def tile(jj, carry):
        j = i - jj
        off = pl.multiple_of(j * t, t)
        k = k_ref[pl.ds(off, t), :]
        v = v_ref[pl.ds(off, t), :]
        dif = dif0 + jj * t
        ok = (dif >= 0) & (dif < WINDOW)
        dist = jnp.abs(pq - pk_ref[:, pl.ds(off, t)])
        out = []
        for r in range(NSA_REP):
            s = _qk(q[:, r * NSA_DH:(r + 1) * NSA_DH], k) - slopes[r] * dist
            out.append(_online(jnp.where(ok, s, NEG), v, *carry[r], valid=ok))
        return tuple(out)

    n_back = (WINDOW + t - 1) // t
    carry = lax.fori_loop(0, jnp.minimum(i, n_back) + 1, tile,
                          tuple(_init(t, NSA_DH) for _ in range(NSA_REP)))
    gates = jax.nn.sigmoid(gate_ref[...].astype(F32))
    for r in range(NSA_REP):
        _, l, acc = carry[r]
        sl = slice(r * NSA_DH, (r + 1) * NSA_DH)

        def gcol(b):
            c0, c1 = 3 * r + b, 3 * (NSA_REP + r) + b
            return jnp.where(g == 0, gates[:, c0:c0 + 1], gates[:, c1:c1 + 1])

        o = (gcol(0) * ocmp_ref[:, sl].astype(F32) + gcol(1) * oslc_ref[:, sl].astype(F32)
             + gcol(2) * (acc / l))
        o_ref[:, sl] = o.astype(o_ref.dtype)


def _win_attn(z, o_cmp, o_slc, pos_col, pos_row, scal):
    S = z.shape[0]
    t = 256
    gw = NSA_REP * NSA_DH
    kcol = Z_NKV // LANE + 8
    return pl.pallas_call(
        functools.partial(_win_attn_kernel, t=t),
        out_shape=jax.ShapeDtypeStruct((S, NSA_HEADS * NSA_DH), BF16),
        grid=(NSA_KV_HEADS, S // t),
        in_specs=[pl.BlockSpec(memory_space=pltpu.SMEM),
                  pl.BlockSpec((t, gw), lambda g, i: (i, Z_NQ // gw + g)),
                  pl.BlockSpec((S, NSA_DH), lambda g, i: (0, kcol + g)),
                  pl.BlockSpec((S, NSA_DH), lambda g, i: (0, kcol + 2 + g)),
                  pl.BlockSpec((t, 1), lambda g, i: (i, 0)),
                  pl.BlockSpec((1, S), lambda g, i: (0, 0)),
                  pl.BlockSpec((t, gw), lambda g, i: (i, g)),
                  pl.BlockSpec((t, gw), lambda g, i: (i, g)),
                  pl.BlockSpec((t, LANE), lambda g, i: (i, Z_NG // LANE))],
        out_specs=pl.BlockSpec((t, gw), lambda g, i: (i, g)),
        compiler_params=_cparams(("arbitrary", "arbitrary")),
        name="nsa_win_attn",
    )(scal, z, z, z, pos_col, pos_row, o_cmp, o_slc, z)


def kernel(x, c, positions, w_ada, b_ada, w_in, mla_q_norm, mla_kv_norm, mla_w_uq, mla_w_ukv, diff_lambda, diff_subln, nsa_cmp_w1, nsa_cmp_b1, nsa_cmp_w2, w_branch, w_out, ln1_g, ln1_b, w_ff1, w_ff2, ln2_g, ln2_b):
    B, S, D = x.shape
    L = w_ada.shape[0]
    assert B == 1 and D == D_MODEL and S % 512 == 0 and S // SLC_BLOCK >= 3

    cols, sign = _w_in_layout()
    w_in_x = (jnp.take(w_in, jnp.asarray(cols), axis=2) * jnp.asarray(sign)).astype(BF16)
    cols, sign = _w_uq_layout()
    w_uq_x = (jnp.take(mla_w_uq, jnp.asarray(cols), axis=2) * jnp.asarray(sign)).astype(BF16)
    w_ukv_b = mla_w_ukv.astype(BF16)
    w_branch_b = w_branch.astype(BF16)
    w_out_b = w_out.astype(BF16)
    w_ff1_b = w_ff1.astype(BF16)
    w_ff2_b = w_ff2.astype(BF16)

    slopes = 2.0 ** (-8.0 * np.arange(1, N_ALIBI_HEADS + 1, dtype=np.float64) / N_ALIBI_HEADS)
    s_diff, s_nsa = slopes[0::2], slopes[1::2]
    nsa_scal = jnp.asarray(s_nsa, F32)

    pos_f = positions.astype(F32)
    pos_col = pos_f.reshape(S, 1)
    pos_row = pos_f.reshape(1, S)
    pos_cmp = jnp.pad(pos_f[0, CMP_BLOCK - 1::CMP_STRIDE], (0, 1)).reshape(1, S // CMP_STRIDE)

    mod = _ada(c, w_ada, b_ada)
    rope_t = _rope_table(pos_col)
    xs = x.reshape(S, D)

    for l in range(L):
        lam_init = 0.8 - 0.6 * math.exp(-0.3 * l)
        diff_scal = jnp.asarray(np.concatenate([s_diff, [lam_init]]), F32)

        h = _lnmod(xs, mod, l, 0, 1)
        z = _matmul(h, w_in_x, l, tm=1024, tn=1280, relu2=False, name="in_proj")

        qm, km, vm = _mla_proj(z, rope_t, mla_q_norm, mla_kv_norm, w_uq_x, w_ukv_b, l)
        o_mla = _mla_attn(qm, km, vm)

        o_diff = _diff_attn(z, pos_col, pos_row, diff_scal, diff_lambda, diff_subln, l)

        xc = z[:, Z_NKV:Z_NKV + 4 * NSA_DH].reshape(S // CMP_STRIDE, CMP_STRIDE, 4, NSA_DH)
        xc = xc.transpose(2, 0, 1, 3).reshape(4, S // CMP_STRIDE, CMP_STRIDE * NSA_DH)
        cmp_kv = _compress(xc, nsa_cmp_w1, nsa_cmp_b1, nsa_cmp_w2, l)
        o_cmp, sel = _cmp_sel(z, cmp_kv, pos_col, pos_cmp, nsa_scal)
        o_slc = _slc_attn(z, sel, pos_col, pos_row, nsa_scal)
        o_nsa = _win_attn(z, o_cmp, o_slc, pos_col, pos_row, nsa_scal)

        merged = _merge(o_mla, o_diff, o_nsa, w_branch_b, z, l)
        xs = _matmul_res_ln(merged, w_out_b, xs, mod, ln1_g, ln1_b, l, 2, ALPHA, "out_proj_ln")

        h = _lnmod(xs, mod, l, 3, 4)
        u = _matmul(h, w_ff1_b, l, tm=1024, tn=1024, relu2=True, name="ff1")
        xs = _matmul_res_ln(u, w_ff2_b, xs, mod, ln2_g, ln2_b, l, 5, ALPHA, "ff2_ln")

    return xs.reshape(B, S, D)
```

```python
import functools
import math

import numpy as np
import jax
import jax.numpy as jnp
from jax import lax
from jax.experimental import pallas as pl
from jax.experimental.pallas import tpu as pltpu

F32 = jnp.float32
BF16 = jnp.bfloat16

D_MODEL = 2048
MLA_HEADS = 8
MLA_Q_LORA = 512
MLA_KV_LORA = 256
MLA_NOPE = 128
MLA_ROPE = 64
MLA_V = 128
ROPE_THETA = 10000.0
DIFF_HEADS = 8
DIFF_QK = 64
DIFF_V = 128
NSA_HEADS = 8
NSA_KV_HEADS = 2
NSA_REP = NSA_HEADS // NSA_KV_HEADS
NSA_DH = 128
CMP_BLOCK = 32
CMP_STRIDE = 16
SLC_BLOCK = 64
N_SELECT = 16
WINDOW = 512
N_ALIBI_HEADS = DIFF_HEADS + NSA_HEADS
N_BRANCH = 3
BRANCH_WIDTH = 1024
D_FF = 4 * D_MODEL
N_MOD = 6
DEPTH = 4
ALPHA = (2.0 * DEPTH) ** 0.25

LANE = 128
NEG = -1e30
LOG2E = math.log2(math.e)
T_CHUNK = 512
VMEM_LIMIT = 48 * 1024 * 1024

Z_CQ = 0
Z_DQ = 512
Z_DK = 1536
Z_DV = 2560
Z_NQ = 3584
Z_NKV = 4608
Z_MG = 6144
Z_CKV = 12288
Z_KR = 12544
Z_NG = 12672
Z_W = 12800


def _cparams(sem):
    return pltpu.CompilerParams(dimension_semantics=sem, vmem_limit_bytes=VMEM_LIMIT)


def _rot_half(w):
    half = MLA_ROPE // 2
    return jnp.concatenate([-w[..., half:], w[..., :half]], axis=-1)


def _relayout_w_in(w_in):
    o = np.cumsum([0, 512, 256, 64, 1024, 1024, 1024, 1024, 1536, 24, N_BRANCH * D_MODEL])
    cq, ckv, kr, dq, dk, dv, nq, nkv, ng, mg = (w_in[..., o[k]:o[k + 1]] for k in range(10))
    pad = jnp.zeros(w_in.shape[:-1] + (Z_W - Z_NG - ng.shape[-1],), w_in.dtype)
    return jnp.concatenate([cq, dq, dk, dv, nq, nkv, mg, ckv, kr, _rot_half(kr), ng, pad],
                           axis=-1).astype(BF16)


def _relayout_w_uq(w_uq):
    L, K, _ = w_uq.shape
    w = w_uq.reshape(L, K, MLA_HEADS, MLA_NOPE + MLA_ROPE)
    rope = w[..., MLA_NOPE:]
    return jnp.concatenate([w, _rot_half(rope)], axis=-1).reshape(L, K, MLA_HEADS * 256).astype(BF16)


def _ada_kernel(c_ref, w_ref, b_ref, o_ref):
    c = c_ref[...]
    ca = c * jax.nn.sigmoid(c)
    ca8 = jnp.broadcast_to(ca, (8, c.shape[1])).astype(BF16)
    r = jnp.dot(ca8, w_ref[...].astype(BF16), preferred_element_type=F32)
    o_ref[...] = r[0:1, :] + b_ref[...]


def _ada(c, w_ada, b_ada):
    L, D, N = w_ada.shape
    tn = 1536
    return pl.pallas_call(
        _ada_kernel,
        out_shape=jax.ShapeDtypeStruct((L, 1, N), F32),
        grid=(L, N // tn),
        in_specs=[pl.BlockSpec((1, D), lambda l, j: (0, 0)),
                  pl.BlockSpec((None, D, tn), lambda l, j: (l, 0, j)),
                  pl.BlockSpec((None, 1, tn), lambda l, j: (l, 0, j))],
        out_specs=pl.BlockSpec((None, 1, tn), lambda l, j: (l, 0, j)),
        compiler_params=_cparams(("arbitrary", "arbitrary")),
        name="ada_mod",
    )(c, w_ada, b_ada.reshape(L, 1, N))


def _lnmod_kernel(x_ref, sh_ref, sc_ref, o_ref):
    x = x_ref[...]
    mu = jnp.mean(x, -1, keepdims=True)
    xc = x - mu
    var = jnp.mean(xc * xc, -1, keepdims=True)
    y = xc * lax.rsqrt(var + 1e-5)
    o_ref[...] = (y * (1.0 + sc_ref[...]) + sh_ref[...]).astype(o_ref.dtype)


def _lnmod(x, mod, l, k_shift, k_scale):
    S, D = x.shape
    tm = 512
    return pl.pallas_call(
        _lnmod_kernel,
        out_shape=jax.ShapeDtypeStruct((S, D), BF16),
        grid=(S // tm,),
        in_specs=[pl.BlockSpec((tm, D), lambda i: (i, 0)),
                  pl.BlockSpec((None, 1, D), lambda i: (l, 0, k_shift)),
                  pl.BlockSpec((None, 1, D), lambda i: (l, 0, k_scale))],
        out_specs=pl.BlockSpec((tm, D), lambda i: (i, 0)),
        compiler_params=_cparams(("arbitrary",)),
        name="ln_modulate",
    )(x, mod, mod)


def _mm_kernel(x_ref, w_ref, o_ref, *, relu2):
    r = jnp.dot(x_ref[...], w_ref[...], preferred_element_type=F32)
    if relu2:
        r = jnp.square(jnp.maximum(r, 0.0))
    o_ref[...] = r.astype(o_ref.dtype)


def _matmul(x, w, l, *, tm, tn, relu2, name):
    S, K = x.shape
    N = w.shape[-1]
    return pl.pallas_call(
        functools.partial(_mm_kernel, relu2=relu2),
        out_shape=jax.ShapeDtypeStruct((S, N), BF16),
        grid=(S // tm, N // tn),
        in_specs=[pl.BlockSpec((tm, K), lambda i, j: (i, 0)),
                  pl.BlockSpec((None, K, tn), lambda i, j: (l, 0, j))],
        out_specs=pl.BlockSpec((tm, tn), lambda i, j: (i, j)),
        compiler_params=_cparams(("arbitrary", "arbitrary")),
        name=name,
    )(x, w)


def _mm_res_ln_kernel(a_ref, w_ref, x_ref, gate_ref, lng_ref, lnb_ref, o_ref, *, nk, alpha):
    k = pl.program_id(1)
    part = jnp.dot(a_ref[...], w_ref[...], preferred_element_type=F32)

    @pl.when(k == 0)
    def _():
        o_ref[...] = part

    @pl.when(k > 0)
    def _():
        o_ref[...] += part

    @pl.when(k == nk - 1)
    def _():
        y = alpha * x_ref[...] + gate_ref[...] * o_ref[...]
        mu = jnp.mean(y, -1, keepdims=True)
        yc = y - mu
        var = jnp.mean(yc * yc, -1, keepdims=True)
        o_ref[...] = yc * lax.rsqrt(var + 1e-5) * lng_ref[...] + lnb_ref[...]


def _matmul_res_ln(a, w, x, mod, lng, lnb, l, k_gate, alpha, name):
    S, K = a.shape
    D = x.shape[1]
    L = lng.shape[0]
    tm, tk = 512, 2048
    nk = K // tk
    return pl.pallas_call(
        functools.partial(_mm_res_ln_kernel, nk=nk, alpha=alpha),
        out_shape=jax.ShapeDtypeStruct((S, D), F32),
        grid=(S // tm, nk),
        in_specs=[pl.BlockSpec((tm, tk), lambda i, k: (i, k)),
                  pl.BlockSpec((None, tk, D), lambda i, k: (l, k, 0)),
                  pl.BlockSpec((tm, D), lambda i, k: (i, 0)),
                  pl.BlockSpec((None, 1, D), lambda i, k: (l, 0, k_gate)),
                  pl.BlockSpec((None, 1, D), lambda i, k: (l, 0, 0)),
                  pl.BlockSpec((None, 1, D), lambda i, k: (l, 0, 0))],
        out_specs=pl.BlockSpec((tm, D), lambda i, k: (i, 0)),
        compiler_params=_cparams(("arbitrary", "arbitrary")),
        name=name,
    )(a, w, x, mod, lng.reshape(L, 1, D), lnb.reshape(L, 1, D))


def _merge_kernel(o0_ref, o1_ref, o2_ref, w_ref, g0_ref, g1_ref, g2_ref, out_ref):
    acc = None
    for n, (o_ref, g_ref) in enumerate(((o0_ref, g0_ref), (o1_ref, g1_ref), (o2_ref, g2_ref))):
        t = jax.nn.sigmoid(g_ref[...].astype(F32)) * jnp.dot(o_ref[...], w_ref[n],
                                                               preferred_element_type=F32)
        acc = t if acc is None else acc + t
    out_ref[...] = acc.astype(out_ref.dtype)


def _merge(o_mla, o_diff, o_nsa, w_branch, z, l):
    S = o_mla.shape[0]
    tm, tn = 1024, 512
    bspec = pl.BlockSpec((tm, BRANCH_WIDTH), lambda i, j: (i, 0))

    def gspec(n):
        return pl.BlockSpec((tm, tn), lambda i, j: (i, (Z_MG + D_MODEL * n) // tn + j))

    return pl.pallas_call(
        _merge_kernel,
        out_shape=jax.ShapeDtypeStruct((S, D_MODEL), BF16),
        grid=(S // tm, D_MODEL // tn),
        in_specs=[bspec, bspec, bspec,
                  pl.BlockSpec((None, N_BRANCH, BRANCH_WIDTH, tn), lambda i, j: (l, 0, 0, j)),
                  gspec(0), gspec(1), gspec(2)],
        out_specs=pl.BlockSpec((tm, tn), lambda i, j: (i, j)),
        compiler_params=_cparams(("arbitrary", "arbitrary")),
        name="branch_merge",
    )(o_mla, o_diff, o_nsa, w_branch, z, z, z)


def _rope_kernel(pos_ref, inv_ref, o_ref):
    ang = pos_ref[...] * inv_ref[...]
    lane = lax.broadcasted_iota(jnp.int32, ang.shape, 1)
    o_ref[...] = jnp.where(lane < MLA_ROPE, jnp.cos(ang), jnp.sin(ang))


def _rope_table(pos_col):
    S = pos_col.shape[0]
    inv = ROPE_THETA ** (-np.arange(0, MLA_ROPE, 2, dtype=np.float64) / MLA_ROPE)
    inv4 = jnp.asarray(np.tile(inv, 4)[None, :], F32)
    tm = 512
    return pl.pallas_call(
        _rope_kernel,
        out_shape=jax.ShapeDtypeStruct((S, LANE), F32),
        grid=(S // tm,),
        in_specs=[pl.BlockSpec((tm, 1), lambda i: (i, 0)),
                  pl.BlockSpec((1, LANE), lambda i: (0, 0))],
        out_specs=pl.BlockSpec((tm, LANE), lambda i: (i, 0)),
        compiler_params=_cparams(("arbitrary",)),
        name="rope_table",
    )(pos_col, inv4)


def _rms(x, g):
    return x * lax.rsqrt(jnp.mean(x * x, -1, keepdims=True) + 1e-6) * g


def _mla_proj_kernel(cq_ref, ckv_ref, kr_ref, t_ref, gq_ref, gkv_ref, wuq_ref, wukv_ref,
                     q_out, k_out, v_out):
    scale = LOG2E * (MLA_NOPE + MLA_ROPE) ** -0.5
    t = t_ref[...]
    nq = _rms(cq_ref[...].astype(F32), gq_ref[...]).astype(BF16)
    q = jnp.dot(nq, wuq_ref[...], preferred_element_type=F32)
    for h in range(MLA_HEADS):
        c = 256 * h
        q_out[:, c:c + 128] = (q[:, c:c + 128] * scale).astype(BF16)
        q_out[:, c + 128:c + 256] = (q[:, c + 128:c + 256] * t * scale).astype(BF16)
    nkv = _rms(ckv_ref[...].astype(F32), gkv_ref[...]).astype(BF16)
    kv = jnp.dot(nkv, wukv_ref[...], preferred_element_type=F32)
    kr = kr_ref[...].astype(F32) * t
    kr2 = (kr + pltpu.roll(kr, MLA_ROPE, 1)).astype(BF16)
    for h in range(MLA_HEADS):
        c = 256 * h
        k_out[:, c:c + 128] = kv[:, c:c + 128].astype(BF16)
        k_out[:, c + 128:c + 256] = kr2
        v_out[:, 128 * h:128 * h + 128] = kv[:, c + 128:c + 256].astype(BF16)


def _mla_proj(z, rope_t, gq, gkv, wuq, wukv, l):
    S = z.shape[0]
    L = gq.shape[0]
    tm = 512
    HW = MLA_HEADS * 256
    return pl.pallas_call(
        _mla_proj_kernel,
        out_shape=(jax.ShapeDtypeStruct((S, HW), BF16),
                   jax.ShapeDtypeStruct((S, HW), BF16),
                   jax.ShapeDtypeStruct((S, MLA_HEADS * MLA_V), BF16)),
        grid=(S // tm,),
        in_specs=[pl.BlockSpec((tm, MLA_Q_LORA), lambda i: (i, Z_CQ // MLA_Q_LORA)),
                  pl.BlockSpec((tm, MLA_KV_LORA), lambda i: (i, Z_CKV // MLA_KV_LORA)),
                  pl.BlockSpec((tm, LANE), lambda i: (i, Z_KR // LANE)),
                  pl.BlockSpec((tm, LANE), lambda i: (i, 0)),
                  pl.BlockSpec((None, 1, MLA_Q_LORA), lambda i: (l, 0, 0)),
                  pl.BlockSpec((None, 1, MLA_KV_LORA), lambda i: (l, 0, 0)),
                  pl.BlockSpec((None, MLA_Q_LORA, HW), lambda i: (l, 0, 0)),
                  pl.BlockSpec((None, MLA_KV_LORA, HW), lambda i: (l, 0, 0))],
        out_specs=(pl.BlockSpec((tm, HW), lambda i: (i, 0)),
                   pl.BlockSpec((tm, HW), lambda i: (i, 0)),
                   pl.BlockSpec((tm, MLA_HEADS * MLA_V), lambda i: (i, 0))),
        compiler_params=_cparams(("arbitrary",)),
        name="mla_proj",
    )(z, z, z, rope_t, gq.reshape(L, 1, -1), gkv.reshape(L, 1, -1), wuq, wukv)


def _qk(q, k):
    return lax.dot_general(q, k, (((1,), (1,)), ((), ())), preferred_element_type=F32)


def _online(s, v, m, l, acc, valid=None):
    m_new = jnp.maximum(m, jnp.max(s, -1, keepdims=True))
    a = jnp.exp(m - m_new)
    p = jnp.exp(s - m_new)
    if valid is not None:
        p = jnp.where(valid, p, 0.0)
    l = a * l + jnp.sum(p, -1, keepdims=True)
    acc = a * acc + jnp.dot(p.astype(BF16), v, preferred_element_type=F32)
    return m_new, l, acc


def _init(rows, dv):
    return (jnp.full((rows, 1), NEG, F32), jnp.zeros((rows, 1), F32), jnp.zeros((rows, dv), F32))


def _kq(k, q):
    return lax.dot_general(k, q, (((1,), (1,)), ((), ())), preferred_element_type=F32)


def _online_t(s, vt, m, l, acc):
    m_new = jnp.maximum(m, jnp.max(s, 0, keepdims=True))
    a = jnp.exp2(m - m_new)
    p = jnp.exp2(s - m_new)
    l = a * l + jnp.sum(p, 0, keepdims=True)
    acc = a * acc + jnp.dot(vt, p.astype(BF16), preferred_element_type=F32)
    return m_new, l, acc


def _init_t(nq, dv):
    return (jnp.full((1, nq), NEG, F32), jnp.zeros((1, nq), F32), jnp.zeros((dv, nq), F32))


def _transpose_into(vt_ref, v_ref):
    for c in range(v_ref.shape[0] // T_CHUNK):
        rows = slice(c * T_CHUNK, (c + 1) * T_CHUNK)
        vt_ref[:, rows] = v_ref[rows, :].astype(F32).T.astype(vt_ref.dtype)


def _mla_attn_kernel(q_ref, k_ref, v_ref, o_ref, vt_ref, *, tq, tk):
    i = pl.program_id(1)

    @pl.when(i == 0)
    def _():
        _transpose_into(vt_ref, v_ref)

    q = q_ref[...]

    def tile(j, carry, masked):
        off = pl.multiple_of(j * tk, tk)
        s = _kq(k_ref[pl.ds(off, tk), :], q)
        if masked:
            kidx = off + lax.broadcasted_iota(jnp.int32, (tk, tq), 0)
            qidx = i * tq + lax.broadcasted_iota(jnp.int32, (tk, tq), 1)
            s = jnp.where(kidx <= qidx, s, NEG)
        return _online_t(s, vt_ref[:, pl.ds(off, tk)], *carry)

    n_full = i * (tq // tk)
    carry = lax.fori_loop(0, n_full, lambda j, c: tile(j, c, False), _init_t(tq, MLA_V))
    for d in range(tq // tk):
        carry = tile(n_full + d, carry, True)
    _, l, acc = carry
    o_ref[...] = (acc / l).T.astype(o_ref.dtype)


def _mla_attn(qm, km, vm):
    S = qm.shape[0]
    tq, tk = min(1024, S), 512
    return pl.pallas_call(
        functools.partial(_mla_attn_kernel, tq=tq, tk=tk),
        out_shape=jax.ShapeDtypeStruct((S, MLA_HEADS * MLA_V), BF16),
        grid=(MLA_HEADS, S // tq),
        in_specs=[pl.BlockSpec((tq, 256), lambda h, i: (i, h)),
                  pl.BlockSpec((S, 256), lambda h, i: (0, h)),
                  pl.BlockSpec((S, MLA_V), lambda h, i: (0, h))],
        out_specs=pl.BlockSpec((tq, MLA_V), lambda h, i: (i, h)),
        scratch_shapes=[pltpu.VMEM((MLA_V, S), BF16)],
        compiler_params=_cparams(("arbitrary", "arbitrary")),
        name="mla_attn",
    )(qm, km, vm)


def _diff_attn_kernel(sc_ref, q_ref, k_ref, v_ref, pq_ref, pkb_ref, lam_ref, g_ref, o_ref, vt_ref, *, t):
    h = pl.program_id(0)
    i = pl.program_id(1)

    @pl.when(i == 0)
    def _():
        _transpose_into(vt_ref, v_ref)

    slope = sc_ref[h]
    lam_init = sc_ref[DIFF_HEADS]
    qs = (q_ref[...].astype(F32) * (LOG2E * DIFF_QK ** -0.5)).astype(BF16)
    lane = lax.broadcasted_iota(jnp.int32, qs.shape, 1)
    zero = jnp.zeros_like(qs)
    q2 = jnp.concatenate([jnp.where(lane < DIFF_QK, qs, zero), jnp.where(lane >= DIFF_QK, qs, zero)], 0)
    pq2 = jnp.tile(pq_ref[...], (1, 2))

    def tile(j, carry, masked):
        off = pl.multiple_of(j * t, t)
        s = _kq(k_ref[pl.ds(off, t), :], q2)
        pkb = jnp.tile(pkb_ref[pl.ds(off, t), :], (1, 2 * t // LANE))
        s = s - slope * jnp.abs(pkb - pq2)
        if masked:
            kidx = lax.broadcasted_iota(jnp.int32, (t, 2 * t), 0)
            lq = lax.broadcasted_iota(jnp.int32, (t, 2 * t), 1)
            qidx = jnp.where(lq >= t, lq - t, lq)
            s = jnp.where(kidx <= qidx, s, NEG)
        return _online_t(s, vt_ref[:, pl.ds(off, t)], *carry)

    carry = lax.fori_loop(0, i, lambda j, c: tile(j, c, False), _init_t(2 * t, DIFF_V))
    _, l, acc = tile(i, carry, True)
    on = acc / l
    lp = lam_ref[...]
    lam = (jnp.exp(jnp.sum(lp[0:1] * lp[1:2], -1, keepdims=True))
           - jnp.exp(jnp.sum(lp[2:3] * lp[3:4], -1, keepdims=True)) + lam_init)
    o = on[:, :t] - lam * on[:, t:]
    o = o * lax.rsqrt(jnp.mean(o * o, 0, keepdims=True) + 1e-6)
    o_ref[...] = (o.T * g_ref[...] * (1.0 - lam_init)).astype(o_ref.dtype)


def _diff_attn(z, pos_row, pos_b, scal, lam_p, g_sub, l):
    S = z.shape[0]
    L = g_sub.shape[0]
    t = 512
    return pl.pallas_call(
        functools.partial(_diff_attn_kernel, t=t),
        out_shape=jax.ShapeDtypeStruct((S, DIFF_HEADS * DIFF_V), BF16),
        grid=(DIFF_HEADS, S // t),
        in_specs=[pl.BlockSpec(memory_space=pltpu.SMEM),
                  pl.BlockSpec((t, LANE), lambda h, i: (i, Z_DQ // LANE + h)),
                  pl.BlockSpec((S, LANE), lambda h, i: (0, Z_DK // LANE + h)),
                  pl.BlockSpec((S, LANE), lambda h, i: (0, Z_DV // LANE + h)),
                  pl.BlockSpec((1, t), lambda h, i: (0, i)),
                  pl.BlockSpec((S, LANE), lambda h, i: (0, 0)),
                  pl.BlockSpec((None, 4, DIFF_QK), lambda h, i: (l, 0, 0)),
                  pl.BlockSpec((None, 1, DIFF_V), lambda h, i: (l, 0, 0))],
        out_specs=pl.BlockSpec((t, DIFF_V), lambda h, i: (i, h)),
        scratch_shapes=[pltpu.VMEM((DIFF_V, S), BF16)],
        compiler_params=_cparams(("arbitrary", "arbitrary")),
        name="diff_attn",
    )(scal, z, z, z, pos_row, pos_b, lam_p, g_sub.reshape(L, 1, -1))


def _compress_kernel(x_ref, w1_ref, b1_ref, w2_ref, o_ref):
    x = x_ref[...]
    half = CMP_STRIDE * NSA_DH
    a = jnp.dot(x, w1_ref[0:half, :].astype(BF16), preferred_element_type=F32)
    b = jnp.dot(x, w1_ref[half:2 * half, :].astype(BF16), preferred_element_type=F32)
    n = x.shape[0]
    pre = a + pltpu.roll(b, n - 1, 0) + b1_ref[...]
    hdn = 0.5 * pre * (1.0 + jnp.tanh(0.7978845608028654 * (pre + 0.044715 * pre * pre * pre)))
    o_ref[...] = jnp.dot(hdn.astype(BF16), w2_ref[...].astype(BF16),
                         preferred_element_type=F32).astype(o_ref.dtype)


def _compress(xc, w1, b1, w2, l):
    _, nb, kk = xc.shape
    L = w1.shape[0]
    return pl.pallas_call(
        _compress_kernel,
        out_shape=jax.ShapeDtypeStruct((4, nb, NSA_DH), BF16),
        grid=(4,),
        in_specs=[pl.BlockSpec((None, nb, kk), lambda c: (c, 0, 0)),
                  pl.BlockSpec((None, None, 2 * kk, NSA_DH), lambda c: (l, c // 2, 0, 0)),
                  pl.BlockSpec((None, None, 1, NSA_DH), lambda c: (l, c // 2, 0, 0)),
                  pl.BlockSpec((None, None, NSA_DH, NSA_DH), lambda c: (l, c // 2, 0, 0))],
        out_specs=pl.BlockSpec((None, nb, NSA_DH), lambda c: (c, 0, 0)),
        compiler_params=_cparams(("arbitrary",)),
        name="nsa_compress",
    )(xc, w1, b1.reshape(L, 2, 1, NSA_DH), w2)


def _split3(x):
    hi = x.astype(BF16)
    r = x - hi.astype(F32)
    mid = r.astype(BF16)
    lo = (r - mid.astype(F32)).astype(BF16)
    return hi, mid, lo


def _cmp_sel_kernel(sc_ref, q_ref, kc_ref, vc_ref, pq_ref, pc_ref, o_ref, sel_ref, *, tq, n_slc):
    g = pl.program_id(0)
    i = pl.program_id(1)
    nc = kc_ref.shape[0]
    q = q_ref[...] * jnp.asarray(NSA_DH ** -0.5, BF16)
    kc = kc_ref[...]
    vc = vc_ref[...]
    pq = pq_ref[...]
    qidx = i * tq + lax.broadcasted_iota(jnp.int32, (tq, nc), 0)
    cmp_end = lax.broadcasted_iota(jnp.int32, (tq, nc), 1) * CMP_STRIDE + (CMP_BLOCK - 1)
    vis = cmp_end <= qidx
    dist = jnp.abs(pq - pc_ref[...])
    p_sum = jnp.zeros((tq, nc), F32)
    for r in range(NSA_REP):
        slope = sc_ref[g * NSA_REP + r]
        s = _qk(q[:, r * NSA_DH:(r + 1) * NSA_DH], kc) - slope * dist
        s = jnp.where(vis, s, NEG)
        m = jnp.max(s, -1, keepdims=True)
        e = jnp.where(vis, jnp.exp(s - m), 0.0)
        d = jnp.sum(e, -1, keepdims=True)
        p = e / jnp.where(d > 0.0, d, 1.0)
        p_sum = p_sum + p
        o_ref[:, r * NSA_DH:(r + 1) * NSA_DH] = jnp.dot(
            p.astype(BF16), vc, preferred_element_type=F32).astype(o_ref.dtype)
    nn = lax.broadcasted_iota(jnp.int32, (nc, n_slc), 0)
    jj = lax.broadcasted_iota(jnp.int32, (nc, n_slc), 1)
    lo_n = jj * (SLC_BLOCK // CMP_STRIDE) - 1
    ovl = jnp.where((nn >= lo_n) & (nn <= lo_n + 4), 1.0, 0.0).astype(BF16)
    imp = sum(jnp.dot(part, ovl, preferred_element_type=F32) for part in _split3(p_sum))
    blk = lax.broadcasted_iota(jnp.int32, (tq, n_slc), 1).astype(F32)
    cur = ((i * tq + lax.broadcasted_iota(jnp.int32, (tq, n_slc), 0)) // SLC_BLOCK).astype(F32)
    forced = (blk == 0) | (blk == cur) | (blk == cur - 1)
    causal = blk <= cur
    score = jnp.where(forced, 3e38, jnp.where(causal, imp, -1.0))
    sel = jnp.zeros((tq, n_slc), F32)
    for _ in range(min(N_SELECT, n_slc)):
        mx = jnp.max(score, -1, keepdims=True)
        first = jnp.min(jnp.where(score == mx, blk, float(n_slc)), -1, keepdims=True)
        pick = blk == first
        sel = jnp.where(pick, 1.0, sel)
        score = jnp.where(pick, -2.0, score)
    sel_ref[...] = jnp.where(causal, sel, 0.0).astype(sel_ref.dtype)


def _cmp_sel(z, cmp_kv, pos_col, pos_cmp, scal):
    S = z.shape[0]
    nc = cmp_kv.shape[1]
    n_slc = S // SLC_BLOCK
    tq = 256
    gw = NSA_REP * NSA_DH
    return pl.pallas_call(
        functools.partial(_cmp_sel_kernel, tq=tq, n_slc=n_slc),
        out_shape=(jax.ShapeDtypeStruct((S, NSA_HEADS * NSA_DH), BF16),
                   jax.ShapeDtypeStruct((NSA_KV_HEADS, S, n_slc), BF16)),
        grid=(NSA_KV_HEADS, S // tq),
        in_specs=[pl.BlockSpec(memory_space=pltpu.SMEM),
                  pl.BlockSpec((tq, gw), lambda g, i: (i, Z_NQ // gw + g)),
                  pl.BlockSpec((None, nc, NSA_DH), lambda g, i: (g, 0, 0)),
                  pl.BlockSpec((None, nc, NSA_DH), lambda g, i: (2 + g, 0, 0)),
                  pl.BlockSpec((tq, 1), lambda g, i: (i, 0)),
                  pl.BlockSpec((1, nc), lambda g, i: (0, 0))],
        out_specs=(pl.BlockSpec((tq, gw), lambda g, i: (i, g)),
                   pl.BlockSpec((None, tq, n_slc), lambda g, i: (g, i, 0))),
        compiler_params=_cparams(("arbitrary", "arbitrary")),
        name="nsa_cmp_select",
    )(scal, z, cmp_kv, cmp_kv, pos_col, pos_cmp)


def _slc_attn_kernel(sc_ref, q_ref, k_ref, v_ref, sel_ref, pq_ref, pkb_ref, o_ref, vt_ref, *, tq, tk):
    g = pl.program_id(0)
    i = pl.program_id(1)

    @pl.when(i == 0)
    def _():
        _transpose_into(vt_ref, v_ref)

    n_slc = sel_ref.shape[1]
    nq = NSA_REP * tq
    q = (q_ref[...].astype(F32) * (LOG2E * NSA_DH ** -0.5)).astype(BF16)
    q4 = jnp.concatenate([q[:, r * NSA_DH:(r + 1) * NSA_DH] for r in range(NSA_REP)], 0)
    sel = sel_ref[...]
    pq4 = jnp.tile(pq_ref[...], (1, NSA_REP))
    slope4 = jnp.concatenate([jnp.full((1, tq), sc_ref[g * NSA_REP + r], F32) for r in range(NSA_REP)], 1)
    blk = lax.broadcasted_iota(jnp.int32, (tk, n_slc), 1)
    kblk0 = lax.shift_right_logical(lax.broadcasted_iota(jnp.int32, (tk, n_slc), 0),
                                    int(math.log2(SLC_BLOCK)))

    def tile(j, carry, masked):
        off = pl.multiple_of(j * tk, tk)
        s = _kq(k_ref[pl.ds(off, tk), :], q4)
        expand = jnp.where(blk == kblk0 + j * (tk // SLC_BLOCK), 1.0, 0.0).astype(BF16)
        mk = _kq(expand, sel)
        if masked:
            kidx = off + lax.broadcasted_iota(jnp.int32, (tk, tq), 0)
            qidx = i * tq + lax.broadcasted_iota(jnp.int32, (tk, tq), 1)
            mk = jnp.where(kidx <= qidx, mk, 0.0)
        negb = (mk - 1.0) * (-NEG)
        pkb = jnp.tile(pkb_ref[pl.ds(off, tk), :], (1, nq // LANE))
        s = s - slope4 * jnp.abs(pkb - pq4) + jnp.tile(negb, (1, NSA_REP))
        return _online_t(s, vt_ref[:, pl.ds(off, tk)], *carry)

    n_tiles = (i * tq + tq + tk - 1) // tk
    carry = lax.fori_loop(0, n_tiles - 1, lambda j, c: tile(j, c, False), _init_t(nq, NSA_DH))
    _, l, acc = tile(n_tiles - 1, carry, True)
    on = acc / l
    for r in range(NSA_REP):
        o_ref[:, r * NSA_DH:(r + 1) * NSA_DH] = on[:, r * tq:(r + 1) * tq].T.astype(o_ref.dtype)


def _slc_attn(z, sel, pos_row, pos_b, scal):
    S = z.shape[0]
    n_slc = S // SLC_BLOCK
    tq, tk = 256, 512
    gw = NSA_REP * NSA_DH
    kcol = Z_NKV // LANE + 4
    return pl.pallas_call(
        functools.partial(_slc_attn_kernel, tq=tq, tk=tk),
        out_shape=jax.ShapeDtypeStruct((S, NSA_HEADS * NSA_DH), BF16),
        grid=(NSA_KV_HEADS, S // tq),
        in_specs=[pl.BlockSpec(memory_space=pltpu.SMEM),
                  pl.BlockSpec((tq, gw), lambda g, i: (i, Z_NQ // gw + g)),
                  pl.BlockSpec((S, NSA_DH), lambda g, i: (0, kcol + g)),
                  pl.BlockSpec((S, NSA_DH), lambda g, i: (0, kcol + 2 + g)),
                  pl.BlockSpec((None, tq, n_slc), lambda g, i: (g, i, 0)),
                  pl.BlockSpec((1, tq), lambda g, i: (0, i)),
                  pl.BlockSpec((S, LANE), lambda g, i: (0, 0))],
        out_specs=pl.BlockSpec((tq, gw), lambda g, i: (i, g)),
        scratch_shapes=[pltpu.VMEM((NSA_DH, S), BF16)],
        compiler_params=_cparams(("arbitrary", "arbitrary")),
        name="nsa_slc_attn",
    )(scal, z, z, z, sel, pos_row, pos_b)


def _win_attn_kernel(sc_ref, q_ref, k_ref, v_ref, pq_ref, pk_ref, ocmp_ref, oslc_ref, gate_ref,
                     o_ref, *, t):
    g = pl.program_id(0)
    i = pl.program_id(1)
    q = q_ref[...] * jnp.asarray(NSA_DH ** -0.5, BF16)
    pq = pq_ref[...]
    slopes = [sc_ref[g * NSA_REP + r] for r in range(NSA_REP)]
    dif0 = (lax.broadcasted_iota(jnp.int32, (t, t), 0) - lax.broadcasted_iota(jnp.int32, (t, t), 1))

    def tile(jj, carry):
        j = i - jj
        off = pl.multiple_of(j * t, t)
        k = k_ref[pl.ds(off, t), :]
        v = v_ref[pl.ds(off, t), :]
        dif = dif0 + jj * t
        ok = (dif >= 0) & (dif < WINDOW)
        dist = jnp.abs(pq - pk_ref[:, pl.ds(off, t)])
        out = []
        for r in range(NSA_REP):
            s = _qk(q[:, r * NSA_DH:(r + 1) * NSA_DH], k) - slopes[r] * dist
            out.append(_online(jnp.where(ok, s, NEG), v, *carry[r], valid=ok))
        return tuple(out)

    n_back = (WINDOW + t - 1) // t
    carry = lax.fori_loop(0, jnp.minimum(i, n_back) + 1, tile,
                          tuple(_init(t, NSA_DH) for _ in range(NSA_REP)))
    gates = jax.nn.sigmoid(gate_ref[...].astype(F32))
    for r in range(NSA_REP):
        _, l, acc = carry[r]
        sl = slice(r * NSA_DH, (r + 1) * NSA_DH)

        def gcol(b):
            c0, c1 = 3 * r + b, 3 * (NSA_REP + r) + b
            return jnp.where(g == 0, gates[:, c0:c0 + 1], gates[:, c1:c1 + 1])

        o = (gcol(0) * ocmp_ref[:, sl].astype(F32) + gcol(1) * oslc_ref[:, sl].astype(F32)
             + gcol(2) * (acc / l))
        o_ref[:, sl] = o.astype(o_ref.dtype)


def _win_attn(z, o_cmp, o_slc, pos_col, pos_row, scal):
    S = z.shape[0]
    t = 256
    gw = NSA_REP * NSA_DH
    kcol = Z_NKV // LANE + 8
    return pl.pallas_call(
        functools.partial(_win_attn_kernel, t=t),
        out_shape=jax.ShapeDtypeStruct((S, NSA_HEADS * NSA_DH), BF16),
        grid=(NSA_KV_HEADS, S // t),
        in_specs=[pl.BlockSpec(memory_space=pltpu.SMEM),
                  pl.BlockSpec((t, gw), lambda g, i: (i, Z_NQ // gw + g)),
                  pl.BlockSpec((S, NSA_DH), lambda g, i: (0, kcol + g)),
                  pl.BlockSpec((S, NSA_DH), lambda g, i: (0, kcol + 2 + g)),
                  pl.BlockSpec((t, 1), lambda g, i: (i, 0)),
                  pl.BlockSpec((1, S), lambda g, i: (0, 0)),
                  pl.BlockSpec((t, gw), lambda g, i: (i, g)),
                  pl.BlockSpec((t, gw), lambda g, i: (i, g)),
                  pl.BlockSpec((t, LANE), lambda g, i: (i, Z_NG // LANE))],
        out_specs=pl.BlockSpec((t, gw), lambda g, i: (i, g)),
        compiler_params=_cparams(("arbitrary", "arbitrary")),
        name="nsa_win_attn",
    )(scal, z, z, z, pos_col, pos_row, o_cmp, o_slc, z)


def kernel(x, c, positions, w_ada, b_ada, w_in, mla_q_norm, mla_kv_norm, mla_w_uq, mla_w_ukv, diff_lambda, diff_subln, nsa_cmp_w1, nsa_cmp_b1, nsa_cmp_w2, w_branch, w_out, ln1_g, ln1_b, w_ff1, w_ff2, ln2_g, ln2_b):
    B, S, D = x.shape
    L = w_ada.shape[0]
    assert B == 1 and D == D_MODEL and S % 512 == 0 and S // SLC_BLOCK >= 3

    w_in_x = _relayout_w_in(w_in)
    w_uq_x = _relayout_w_uq(mla_w_uq)
    w_ukv_b = mla_w_ukv.astype(BF16)
    w_branch_b = w_branch.astype(BF16)
    w_out_b = w_out.astype(BF16)
    w_ff1_b = w_ff1.astype(BF16)
    w_ff2_b = w_ff2.astype(BF16)

    slopes = 2.0 ** (-8.0 * np.arange(1, N_ALIBI_HEADS + 1, dtype=np.float64) / N_ALIBI_HEADS)
    s_diff, s_nsa = slopes[0::2], slopes[1::2]
    nsa_scal = jnp.asarray(s_nsa, F32)
    nsa_scal2 = jnp.asarray(s_nsa * LOG2E, F32)

    pos_f = positions.astype(F32)
    pos_col = pos_f.reshape(S, 1)
    pos_row = pos_f.reshape(1, S)
    pos_b = jnp.broadcast_to(pos_col, (S, LANE))
    pos_cmp = jnp.pad(pos_f[0, CMP_BLOCK - 1::CMP_STRIDE], (0, 1)).reshape(1, S // CMP_STRIDE)

    mod = _ada(c, w_ada, b_ada)
    rope_t = _rope_table(pos_col)
    xs = x.reshape(S, D)

    for l in range(L):
        lam_init = 0.8 - 0.6 * math.exp(-0.3 * l)
        diff_scal = jnp.asarray(np.concatenate([s_diff * LOG2E, [lam_init]]), F32)

        h = _lnmod(xs, mod, l, 0, 1)
        z = _matmul(h, w_in_x, l, tm=1024, tn=1280, relu2=False, name="in_proj")

        qm, km, vm = _mla_proj(z, rope_t, mla_q_norm, mla_kv_norm, w_uq_x, w_ukv_b, l)
        o_mla = _mla_attn(qm, km, vm)

        o_diff = _diff_attn(z, pos_row, pos_b, diff_scal, diff_lambda, diff_subln, l)

        xc = z[:, Z_NKV:Z_NKV + 4 * NSA_DH].reshape(S // CMP_STRIDE, CMP_STRIDE, 4, NSA_DH)
        xc = xc.transpose(2, 0, 1, 3).reshape(4, S // CMP_STRIDE, CMP_STRIDE * NSA_DH)
        cmp_kv = _compress(xc, nsa_cmp_w1, nsa_cmp_b1, nsa_cmp_w2, l)
        o_cmp, sel = _cmp_sel(z, cmp_kv, pos_col, pos_cmp, nsa_scal)
        o_slc = _slc_attn(z, sel, pos_row, pos_b, nsa_scal2)
        o_nsa = _win_attn(z, o_cmp, o_slc, pos_col, pos_row, nsa_scal)

        merged = _merge(o_mla, o_diff, o_nsa, w_branch_b, z, l)
        xs = _matmul_res_ln(merged, w_out_b, xs, mod, ln1_g, ln1_b, l, 2, ALPHA, "out_proj_ln")

        h = _lnmod(xs, mod, l, 3, 4)
        u = _matmul(h, w_ff1_b, l, tm=1024, tn=1024, relu2=True, name="ff1")
        xs = _matmul_res_ln(u, w_ff2_b, xs, mod, ln2_g, ln2_b, l, 5, ALPHA, "ff2_ln")

    return xs.reshape(B, S, D)
```

```python
import functools
import math

import numpy as np
import jax
import jax.numpy as jnp
from jax import lax
from jax.experimental import pallas as pl
from jax.experimental.pallas import tpu as pltpu

F32 = jnp.float32
BF16 = jnp.bfloat16

D_MODEL = 2048
MLA_HEADS = 8
MLA_Q_LORA = 512
MLA_KV_LORA = 256
MLA_NOPE = 128
MLA_ROPE = 64
MLA_V = 128
ROPE_THETA = 10000.0
DIFF_HEADS = 8
DIFF_QK = 64
DIFF_V = 128
NSA_HEADS = 8
NSA_KV_HEADS = 2
NSA_REP = NSA_HEADS // NSA_KV_HEADS
NSA_DH = 128
CMP_BLOCK = 32
CMP_STRIDE = 16
SLC_BLOCK = 64
N_SELECT = 16
WINDOW = 512
N_ALIBI_HEADS = DIFF_HEADS + NSA_HEADS
N_BRANCH = 3
BRANCH_WIDTH = 1024
D_FF = 4 * D_MODEL
N_MOD = 6
DEPTH = 4
ALPHA = (2.0 * DEPTH) ** 0.25

LANE = 128
NEG = -1e30
LOG2E = math.log2(math.e)
T_CHUNK = 512
MASK_DIST = 1e33
VMEM_LIMIT = 48 * 1024 * 1024

Z_CQ = 0
Z_DQ = 512
Z_DK = 1536
Z_DV = 2560
Z_NQ = 3584
Z_NKV = 4608
Z_MG = 6144
Z_CKV = 12288
Z_KR = 12544
Z_NG = 12672
Z_W = 12800


def _cparams(sem):
    return pltpu.CompilerParams(dimension_semantics=sem, vmem_limit_bytes=VMEM_LIMIT)


def _rot_half(w):
    half = MLA_ROPE // 2
    return jnp.concatenate([-w[..., half:], w[..., :half]], axis=-1)


def _relayout_w_in(w_in):
    o = np.cumsum([0, 512, 256, 64, 1024, 1024, 1024, 1024, 1536, 24, N_BRANCH * D_MODEL])
    cq, ckv, kr, dq, dk, dv, nq, nkv, ng, mg = (w_in[..., o[k]:o[k + 1]] for k in range(10))
    pad = jnp.zeros(w_in.shape[:-1] + (Z_W - Z_NG - ng.shape[-1],), w_in.dtype)
    return jnp.concatenate([cq, dq, dk, dv, nq, nkv, mg, ckv, kr, _rot_half(kr), ng, pad],
                           axis=-1).astype(BF16)


def _relayout_w_uq(w_uq):
    L, K, _ = w_uq.shape
    w = w_uq.reshape(L, K, MLA_HEADS, MLA_NOPE + MLA_ROPE)
    rope = w[..., MLA_NOPE:]
    return jnp.concatenate([w, _rot_half(rope)], axis=-1).reshape(L, K, MLA_HEADS * 256).astype(BF16)


def _ada_kernel(c_ref, w_ref, b_ref, o_ref):
    c = c_ref[...]
    ca = c * jax.nn.sigmoid(c)
    ca8 = jnp.broadcast_to(ca, (8, c.shape[1])).astype(BF16)
    r = jnp.dot(ca8, w_ref[...].astype(BF16), preferred_element_type=F32)
    o_ref[...] = r[0:1, :] + b_ref[...]


def _ada(c, w_ada, b_ada):
    L, D, N = w_ada.shape
    tn = 1536
    return pl.pallas_call(
        _ada_kernel,
        out_shape=jax.ShapeDtypeStruct((L, 1, N), F32),
        grid=(L, N // tn),
        in_specs=[pl.BlockSpec((1, D), lambda l, j: (0, 0)),
                  pl.BlockSpec((None, D, tn), lambda l, j: (l, 0, j)),
                  pl.BlockSpec((None, 1, tn), lambda l, j: (l, 0, j))],
        out_specs=pl.BlockSpec((None, 1, tn), lambda l, j: (l, 0, j)),
        compiler_params=_cparams(("arbitrary", "arbitrary")),
        name="ada_mod",
    )(c, w_ada, b_ada.reshape(L, 1, N))


def _lnmod_kernel(x_ref, sh_ref, sc_ref, o_ref):
    x = x_ref[...]
    mu = jnp.mean(x, -1, keepdims=True)
    xc = x - mu
    var = jnp.mean(xc * xc, -1, keepdims=True)
    y = xc * lax.rsqrt(var + 1e-5)
    o_ref[...] = (y * (1.0 + sc_ref[...]) + sh_ref[...]).astype(o_ref.dtype)


def _lnmod(x, mod, l, k_shift, k_scale):
    S, D = x.shape
    tm = 512
    return pl.pallas_call(
        _lnmod_kernel,
        out_shape=jax.ShapeDtypeStruct((S, D), BF16),
        grid=(S // tm,),
        in_specs=[pl.BlockSpec((tm, D), lambda i: (i, 0)),
                  pl.BlockSpec((None, 1, D), lambda i: (l, 0, k_shift)),
                  pl.BlockSpec((None, 1, D), lambda i: (l, 0, k_scale))],
        out_specs=pl.BlockSpec((tm, D), lambda i: (i, 0)),
        compiler_params=_cparams(("arbitrary",)),
        name="ln_modulate",
    )(x, mod, mod)


def _mm_kernel(x_ref, w_ref, o_ref, *, relu2):
    r = jnp.dot(x_ref[...], w_ref[...], preferred_element_type=F32)
    if relu2:
        r = jnp.square(jnp.maximum(r, 0.0))
    o_ref[...] = r.astype(o_ref.dtype)


def _matmul(x, w, l, *, tm, tn, relu2, name):
    S, K = x.shape
    N = w.shape[-1]
    return pl.pallas_call(
        functools.partial(_mm_kernel, relu2=relu2),
        out_shape=jax.ShapeDtypeStruct((S, N), BF16),
        grid=(S // tm, N // tn),
        in_specs=[pl.BlockSpec((tm, K), lambda i, j: (i, 0)),
                  pl.BlockSpec((None, K, tn), lambda i, j: (l, 0, j))],
        out_specs=pl.BlockSpec((tm, tn), lambda i, j: (i, j)),
        compiler_params=_cparams(("arbitrary", "arbitrary")),
        name=name,
    )(x, w)


def _mm_res_ln_kernel(a_ref, w_ref, x_ref, gate_ref, lng_ref, lnb_ref, o_ref, *, nk, alpha):
    k = pl.program_id(1)
    part = jnp.dot(a_ref[...], w_ref[...], preferred_element_type=F32)

    @pl.when(k == 0)
    def _():
        o_ref[...] = part

    @pl.when(k > 0)
    def _():
        o_ref[...] += part

    @pl.when(k == nk - 1)
    def _():
        y = alpha * x_ref[...] + gate_ref[...] * o_ref[...]
        mu = jnp.mean(y, -1, keepdims=True)
        yc = y - mu
        var = jnp.mean(yc * yc, -1, keepdims=True)
        o_ref[...] = yc * lax.rsqrt(var + 1e-5) * lng_ref[...] + lnb_ref[...]


def _matmul_res_ln(a, w, x, mod, lng, lnb, l, k_gate, alpha, name):
    S, K = a.shape
    D = x.shape[1]
    L = lng.shape[0]
    tm, tk = 512, 2048
    nk = K // tk
    return pl.pallas_call(
        functools.partial(_mm_res_ln_kernel, nk=nk, alpha=alpha),
        out_shape=jax.ShapeDtypeStruct((S, D), F32),
        grid=(S // tm, nk),
        in_specs=[pl.BlockSpec((tm, tk), lambda i, k: (i, k)),
                  pl.BlockSpec((None, tk, D), lambda i, k: (l, k, 0)),
                  pl.BlockSpec((tm, D), lambda i, k: (i, 0)),
                  pl.BlockSpec((None, 1, D), lambda i, k: (l, 0, k_gate)),
                  pl.BlockSpec((None, 1, D), lambda i, k: (l, 0, 0)),
                  pl.BlockSpec((None, 1, D), lambda i, k: (l, 0, 0))],
        out_specs=pl.BlockSpec((tm, D), lambda i, k: (i, 0)),
        compiler_params=_cparams(("arbitrary", "arbitrary")),
        name=name,
    )(a, w, x, mod, lng.reshape(L, 1, D), lnb.reshape(L, 1, D))


def _merge_kernel(o0_ref, o1_ref, o2_ref, w_ref, g0_ref, g1_ref, g2_ref, out_ref):
    acc = None
    for n, (o_ref, g_ref) in enumerate(((o0_ref, g0_ref), (o1_ref, g1_ref), (o2_ref, g2_ref))):
        t = jax.nn.sigmoid(g_ref[...].astype(F32)) * jnp.dot(o_ref[...], w_ref[n],
                                                               preferred_element_type=F32)
        acc = t if acc is None else acc + t
    out_ref[...] = acc.astype(out_ref.dtype)


def _merge(o_mla, o_diff, o_nsa, w_branch, z, l):
    S = o_mla.shape[0]
    tm, tn = 1024, 512
    bspec = pl.BlockSpec((tm, BRANCH_WIDTH), lambda i, j: (i, 0))

    def gspec(n):
        return pl.BlockSpec((tm, tn), lambda i, j: (i, (Z_MG + D_MODEL * n) // tn + j))

    return pl.pallas_call(
        _merge_kernel,
        out_shape=jax.ShapeDtypeStruct((S, D_MODEL), BF16),
        grid=(S // tm, D_MODEL // tn),
        in_specs=[bspec, bspec, bspec,
                  pl.BlockSpec((None, N_BRANCH, BRANCH_WIDTH, tn), lambda i, j: (l, 0, 0, j)),
                  gspec(0), gspec(1), gspec(2)],
        out_specs=pl.BlockSpec((tm, tn), lambda i, j: (i, j)),
        compiler_params=_cparams(("arbitrary", "arbitrary")),
        name="branch_merge",
    )(o_mla, o_diff, o_nsa, w_branch, z, z, z)


def _rope_kernel(pos_ref, inv_ref, o_ref):
    ang = pos_ref[...] * inv_ref[...]
    lane = lax.broadcasted_iota(jnp.int32, ang.shape, 1)
    o_ref[...] = jnp.where(lane < MLA_ROPE, jnp.cos(ang), jnp.sin(ang))


def _rope_table(pos_col):
    S = pos_col.shape[0]
    inv = ROPE_THETA ** (-np.arange(0, MLA_ROPE, 2, dtype=np.float64) / MLA_ROPE)
    inv4 = jnp.asarray(np.tile(inv, 4)[None, :], F32)
    tm = 512
    return pl.pallas_call(
        _rope_kernel,
        out_shape=jax.ShapeDtypeStruct((S, LANE), F32),
        grid=(S // tm,),
        in_specs=[pl.BlockSpec((tm, 1), lambda i: (i, 0)),
                  pl.BlockSpec((1, LANE), lambda i: (0, 0))],
        out_specs=pl.BlockSpec((tm, LANE), lambda i: (i, 0)),
        compiler_params=_cparams(("arbitrary",)),
        name="rope_table",
    )(pos_col, inv4)


def _rms(x, g):
    return x * lax.rsqrt(jnp.mean(x * x, -1, keepdims=True) + 1e-6) * g


def _mla_proj_kernel(cq_ref, ckv_ref, kr_ref, t_ref, gq_ref, gkv_ref, wuq_ref, wukv_ref,
                     q_out, k_out, v_out):
    scale = LOG2E * (MLA_NOPE + MLA_ROPE) ** -0.5
    t = t_ref[...]
    nq = _rms(cq_ref[...].astype(F32), gq_ref[...]).astype(BF16)
    q = jnp.dot(nq, wuq_ref[...], preferred_element_type=F32)
    for h in range(MLA_HEADS):
        c = 256 * h
        q_out[:, c:c + 128] = (q[:, c:c + 128] * scale).astype(BF16)
        q_out[:, c + 128:c + 256] = (q[:, c + 128:c + 256] * t * scale).astype(BF16)
    nkv = _rms(ckv_ref[...].astype(F32), gkv_ref[...]).astype(BF16)
    kv = jnp.dot(nkv, wukv_ref[...], preferred_element_type=F32)
    kr = kr_ref[...].astype(F32) * t
    kr2 = (kr + pltpu.roll(kr, MLA_ROPE, 1)).astype(BF16)
    for h in range(MLA_HEADS):
        c = 256 * h
        k_out[:, c:c + 128] = kv[:, c:c + 128].astype(BF16)
        k_out[:, c + 128:c + 256] = kr2
        v_out[:, 128 * h:128 * h + 128] = kv[:, c + 128:c + 256].astype(BF16)


def _mla_proj(z, rope_t, gq, gkv, wuq, wukv, l):
    S = z.shape[0]
    L = gq.shape[0]
    tm = 512
    HW = MLA_HEADS * 256
    return pl.pallas_call(
        _mla_proj_kernel,
        out_shape=(jax.ShapeDtypeStruct((S, HW), BF16),
                   jax.ShapeDtypeStruct((S, HW), BF16),
                   jax.ShapeDtypeStruct((S, MLA_HEADS * MLA_V), BF16)),
        grid=(S // tm,),
        in_specs=[pl.BlockSpec((tm, MLA_Q_LORA), lambda i: (i, Z_CQ // MLA_Q_LORA)),
                  pl.BlockSpec((tm, MLA_KV_LORA), lambda i: (i, Z_CKV // MLA_KV_LORA)),
                  pl.BlockSpec((tm, LANE), lambda i: (i, Z_KR // LANE)),
                  pl.BlockSpec((tm, LANE), lambda i: (i, 0)),
                  pl.BlockSpec((None, 1, MLA_Q_LORA), lambda i: (l, 0, 0)),
                  pl.BlockSpec((None, 1, MLA_KV_LORA), lambda i: (l, 0, 0)),
                  pl.BlockSpec((None, MLA_Q_LORA, HW), lambda i: (l, 0, 0)),
                  pl.BlockSpec((None, MLA_KV_LORA, HW), lambda i: (l, 0, 0))],
        out_specs=(pl.BlockSpec((tm, HW), lambda i: (i, 0)),
                   pl.BlockSpec((tm, HW), lambda i: (i, 0)),
                   pl.BlockSpec((tm, MLA_HEADS * MLA_V), lambda i: (i, 0))),
        compiler_params=_cparams(("arbitrary",)),
        name="mla_proj",
    )(z, z, z, rope_t, gq.reshape(L, 1, -1), gkv.reshape(L, 1, -1), wuq, wukv)


def _qk(q, k):
    return lax.dot_general(q, k, (((1,), (1,)), ((), ())), preferred_element_type=F32)


def _online(s, v, m, l, acc, valid=None):
    m_new = jnp.maximum(m, jnp.max(s, -1, keepdims=True))
    a = jnp.exp(m - m_new)
    p = jnp.exp(s - m_new)
    if valid is not None:
        p = jnp.where(valid, p, 0.0)
    l = a * l + jnp.sum(p, -1, keepdims=True)
    acc = a * acc + jnp.dot(p.astype(BF16), v, preferred_element_type=F32)
    return m_new, l, acc


def _init(rows, dv):
    return (jnp.full((rows, 1), NEG, F32), jnp.zeros((rows, 1), F32), jnp.zeros((rows, dv), F32))


def _kq(k, q):
    return lax.dot_general(k, q, (((1,), (1,)), ((), ())), preferred_element_type=F32)


def _online_t(s, vt, m, l, acc):
    m_new = jnp.maximum(m, jnp.max(s, 0, keepdims=True))
    a = jnp.exp2(m - m_new)
    p = jnp.exp2(s - m_new)
    l = a * l + jnp.sum(p, 0, keepdims=True)
    acc = a * acc + jnp.dot(vt, p.astype(BF16), preferred_element_type=F32)
    return m_new, l, acc


def _init_t(nq, dv):
    return (jnp.full((1, nq), NEG, F32), jnp.zeros((1, nq), F32), jnp.zeros((dv, nq), F32))


def _transpose_into(vt_ref, v_ref):
    for c in range(v_ref.shape[0] // T_CHUNK):
        rows = slice(c * T_CHUNK, (c + 1) * T_CHUNK)
        vt_ref[:, rows] = v_ref[rows, :].astype(F32).T.astype(vt_ref.dtype)


def _sweep(n_pairs, score, update, carry, sa_ref, sb_ref):
    sa_ref[...] = score(0)

    def pair(p, carry):
        j = 2 * p
        sb_ref[...] = score(j + 1)
        carry = update(j, sa_ref, carry, False)
        sa_ref[...] = score(j + 2)
        return update(j + 1, sb_ref, carry, False)

    carry = lax.fori_loop(0, n_pairs, pair, carry)
    j = 2 * n_pairs
    sb_ref[...] = score(j + 1)
    carry = update(j, sa_ref, carry, True)
    return update(j + 1, sb_ref, carry, True)


def _mla_attn_kernel(q_ref, k_ref, v_ref, o_ref, vt_ref, sa_ref, sb_ref, *, tq, tk):
    i = pl.program_id(1)

    @pl.when(i == 0)
    def _():
        _transpose_into(vt_ref, v_ref)

    q = q_ref[...]

    def score(j):
        return _kq(k_ref[pl.ds(pl.multiple_of(j * tk, tk), tk), :], q)

    def update(j, s_ref, carry, masked):
        off = pl.multiple_of(j * tk, tk)
        s = s_ref[...]
        if masked:
            kidx = off + lax.broadcasted_iota(jnp.int32, (tk, tq), 0)
            qidx = i * tq + lax.broadcasted_iota(jnp.int32, (tk, tq), 1)
            s = jnp.where(kidx <= qidx, s, NEG)
        return _online_t(s, vt_ref[:, pl.ds(off, tk)], *carry)

    _, l, acc = _sweep(i, score, update, _init_t(tq, MLA_V), sa_ref, sb_ref)
    o_ref[...] = (acc / l).T.astype(o_ref.dtype)


def _mla_attn(qm, km, vm):
    S = qm.shape[0]
    tq, tk = 1024, 512
    return pl.pallas_call(
        functools.partial(_mla_attn_kernel, tq=tq, tk=tk),
        out_shape=jax.ShapeDtypeStruct((S, MLA_HEADS * MLA_V), BF16),
        grid=(MLA_HEADS, S // tq),
        in_specs=[pl.BlockSpec((tq, 256), lambda h, i: (i, h)),
                  pl.BlockSpec((S, 256), lambda h, i: (0, h)),
                  pl.BlockSpec((S, MLA_V), lambda h, i: (0, h))],
        out_specs=pl.BlockSpec((tq, MLA_V), lambda h, i: (i, h)),
        scratch_shapes=[pltpu.VMEM((MLA_V, S), BF16),
                        pltpu.VMEM((tk, tq), F32), pltpu.VMEM((tk, tq), F32)],
        compiler_params=_cparams(("arbitrary", "arbitrary")),
        name="mla_attn",
    )(qm, km, vm)


def _diff_attn_kernel(sc_ref, q_ref, k_ref, v_ref, pq_ref, pkb_ref, lam_ref, g_ref, o_ref,
                      vt_ref, sa_ref, sb_ref, *, tq, tk):
    h = pl.program_id(0)
    i = pl.program_id(1)

    @pl.when(i == 0)
    def _():
        _transpose_into(vt_ref, v_ref)

    slope = sc_ref[h]
    lam_init = sc_ref[DIFF_HEADS]
    qs = (q_ref[...].astype(F32) * (LOG2E * DIFF_QK ** -0.5)).astype(BF16)
    lane = lax.broadcasted_iota(jnp.int32, qs.shape, 1)
    zero = jnp.zeros_like(qs)
    q2 = jnp.concatenate([jnp.where(lane < DIFF_QK, qs, zero), jnp.where(lane >= DIFF_QK, qs, zero)], 0)
    pq = pq_ref[...]

    def score(j):
        return _kq(k_ref[pl.ds(pl.multiple_of(j * tk, tk), tk), :], q2)

    def update(j, s_ref, carry, masked):
        off = pl.multiple_of(j * tk, tk)
        vt = vt_ref[:, pl.ds(off, tk)]
        bias = slope * jnp.abs(jnp.tile(pkb_ref[pl.ds(off, tk), :], (1, tq // LANE)) - pq)
        if masked:
            kidx = off + lax.broadcasted_iota(jnp.int32, (tk, tq), 0)
            qidx = i * tq + lax.broadcasted_iota(jnp.int32, (tk, tq), 1)
            bias = jnp.where(kidx <= qidx, bias, -NEG)
        return tuple(_online_t(s_ref[:, c * tq:(c + 1) * tq] - bias, vt, *carry[c]) for c in range(2))

    (_, l0, a0), (_, l1, a1) = _sweep(i, score, update, (_init_t(tq, DIFF_V), _init_t(tq, DIFF_V)),
                                      sa_ref, sb_ref)
    lp = lam_ref[...]
    lam = (jnp.exp(jnp.sum(lp[0:1] * lp[1:2], -1, keepdims=True))
           - jnp.exp(jnp.sum(lp[2:3] * lp[3:4], -1, keepdims=True)) + lam_init)
    o = a0 / l0 - lam * (a1 / l1)
    o = o * lax.rsqrt(jnp.mean(o * o, 0, keepdims=True) + 1e-6)
    o_ref[...] = (o.T * g_ref[...] * (1.0 - lam_init)).astype(o_ref.dtype)


def _diff_attn(z, pos_row, pos_b, scal, lam_p, g_sub, l):
    S = z.shape[0]
    L = g_sub.shape[0]
    tq, tk = 1024, 512
    return pl.pallas_call(
        functools.partial(_diff_attn_kernel, tq=tq, tk=tk),
        out_shape=jax.ShapeDtypeStruct((S, DIFF_HEADS * DIFF_V), BF16),
        grid=(DIFF_HEADS, S // tq),
        in_specs=[pl.BlockSpec(memory_space=pltpu.SMEM),
                  pl.BlockSpec((tq, LANE), lambda h, i: (i, Z_DQ // LANE + h)),
                  pl.BlockSpec((S, LANE), lambda h, i: (0, Z_DK // LANE + h)),
                  pl.BlockSpec((S, LANE), lambda h, i: (0, Z_DV // LANE + h)),
                  pl.BlockSpec((1, tq), lambda h, i: (0, i)),
                  pl.BlockSpec((S, LANE), lambda h, i: (0, 0)),
                  pl.BlockSpec((None, 4, DIFF_QK), lambda h, i: (l, 0, 0)),
                  pl.BlockSpec((None, 1, DIFF_V), lambda h, i: (l, 0, 0))],
        out_specs=pl.BlockSpec((tq, DIFF_V), lambda h, i: (i, h)),
        scratch_shapes=[pltpu.VMEM((DIFF_V, S), BF16),
                        pltpu.VMEM((tk, 2 * tq), F32), pltpu.VMEM((tk, 2 * tq), F32)],
        compiler_params=_cparams(("arbitrary", "arbitrary")),
        name="diff_attn",
    )(scal, z, z, z, pos_row, pos_b, lam_p, g_sub.reshape(L, 1, -1))


def _compress_kernel(x_ref, w1_ref, b1_ref, w2_ref, o_ref):
    x = x_ref[...]
    half = CMP_STRIDE * NSA_DH
    a = jnp.dot(x, w1_ref[0:half, :].astype(BF16), preferred_element_type=F32)
    b = jnp.dot(x, w1_ref[half:2 * half, :].astype(BF16), preferred_element_type=F32)
    n = x.shape[0]
    pre = a + pltpu.roll(b, n - 1, 0) + b1_ref[...]
    hdn = 0.5 * pre * (1.0 + jnp.tanh(0.7978845608028654 * (pre + 0.044715 * pre * pre * pre)))
    o_ref[...] = jnp.dot(hdn.astype(BF16), w2_ref[...].astype(BF16),
                         preferred_element_type=F32).astype(o_ref.dtype)


def _compress(xc, w1, b1, w2, l):
    _, nb, kk = xc.shape
    L = w1.shape[0]
    return pl.pallas_call(
        _compress_kernel,
        out_shape=jax.ShapeDtypeStruct((4, nb, NSA_DH), BF16),
        grid=(4,),
        in_specs=[pl.BlockSpec((None, nb, kk), lambda c: (c, 0, 0)),
                  pl.BlockSpec((None, None, 2 * kk, NSA_DH), lambda c: (l, c // 2, 0, 0)),
                  pl.BlockSpec((None, None, 1, NSA_DH), lambda c: (l, c // 2, 0, 0)),
                  pl.BlockSpec((None, None, NSA_DH, NSA_DH), lambda c: (l, c // 2, 0, 0))],
        out_specs=pl.BlockSpec((None, nb, NSA_DH), lambda c: (c, 0, 0)),
        compiler_params=_cparams(("arbitrary",)),
        name="nsa_compress",
    )(xc, w1, b1.reshape(L, 2, 1, NSA_DH), w2)


def _split3(x):
    hi = x.astype(BF16)
    r = x - hi.astype(F32)
    mid = r.astype(BF16)
    lo = (r - mid.astype(F32)).astype(BF16)
    return hi, mid, lo


def _cmp_sel_kernel(sc_ref, q_ref, kc_ref, vc_ref, pq_ref, pc_ref, o_ref, sel_ref, *, tq, n_slc):
    g = pl.program_id(0)
    i = pl.program_id(1)
    nc = kc_ref.shape[0]
    q = q_ref[...] * jnp.asarray(NSA_DH ** -0.5, BF16)
    kc = kc_ref[...]
    vc = vc_ref[...]
    pq = pq_ref[...]
    qidx = i * tq + lax.broadcasted_iota(jnp.int32, (tq, nc), 0)
    cmp_end = lax.broadcasted_iota(jnp.int32, (tq, nc), 1) * CMP_STRIDE + (CMP_BLOCK - 1)
    vis = cmp_end <= qidx
    dist = jnp.abs(pq - pc_ref[...])
    p_sum = jnp.zeros((tq, nc), F32)
    for r in range(NSA_REP):
        slope = sc_ref[g * NSA_REP + r]
        s = _qk(q[:, r * NSA_DH:(r + 1) * NSA_DH], kc) - slope * dist
        s = jnp.where(vis, s, NEG)
        m = jnp.max(s, -1, keepdims=True)
        e = jnp.where(vis, jnp.exp(s - m), 0.0)
        d = jnp.sum(e, -1, keepdims=True)
        p = e / jnp.where(d > 0.0, d, 1.0)
        p_sum = p_sum + p
        o_ref[:, r * NSA_DH:(r + 1) * NSA_DH] = jnp.dot(
            p.astype(BF16), vc, preferred_element_type=F32).astype(o_ref.dtype)
    nn = lax.broadcasted_iota(jnp.int32, (nc, n_slc), 0)
    jj = lax.broadcasted_iota(jnp.int32, (nc, n_slc), 1)
    lo_n = jj * (SLC_BLOCK // CMP_STRIDE) - 1
    ovl = jnp.where((nn >= lo_n) & (nn <= lo_n + 4), 1.0, 0.0).astype(BF16)
    imp = sum(jnp.dot(part, ovl, preferred_element_type=F32) for part in _split3(p_sum))
    blk = lax.broadcasted_iota(jnp.int32, (tq, n_slc), 1).astype(F32)
    cur = ((i * tq + lax.broadcasted_iota(jnp.int32, (tq, n_slc), 0)) // SLC_BLOCK).astype(F32)
    forced = (blk == 0) | (blk == cur) | (blk == cur - 1)
    causal = blk <= cur
    score = jnp.where(forced, 3e38, jnp.where(causal, imp, -1.0))
    sel = jnp.zeros((tq, n_slc), F32)
    for _ in range(min(N_SELECT, n_slc)):
        mx = jnp.max(score, -1, keepdims=True)
        first = jnp.min(jnp.where(score == mx, blk, float(n_slc)), -1, keepdims=True)
        pick = blk == first
        sel = jnp.where(pick, 1.0, sel)
        score = jnp.where(pick, -2.0, score)
    sel_ref[...] = jnp.where(causal, sel, 0.0).astype(sel_ref.dtype)


def _cmp_sel(z, cmp_kv, pos_col, pos_cmp, scal):
    S = z.shape[0]
    nc = cmp_kv.shape[1]
    n_slc = S // SLC_BLOCK
    tq = 256
    gw = NSA_REP * NSA_DH
    return pl.pallas_call(
        functools.partial(_cmp_sel_kernel, tq=tq, n_slc=n_slc),
        out_shape=(jax.ShapeDtypeStruct((S, NSA_HEADS * NSA_DH), BF16),
                   jax.ShapeDtypeStruct((NSA_KV_HEADS, S, n_slc), BF16)),
        grid=(NSA_KV_HEADS, S // tq),
        in_specs=[pl.BlockSpec(memory_space=pltpu.SMEM),
                  pl.BlockSpec((tq, gw), lambda g, i: (i, Z_NQ // gw + g)),
                  pl.BlockSpec((None, nc, NSA_DH), lambda g, i: (g, 0, 0)),
                  pl.BlockSpec((None, nc, NSA_DH), lambda g, i: (2 + g, 0, 0)),
                  pl.BlockSpec((tq, 1), lambda g, i: (i, 0)),
                  pl.BlockSpec((1, nc), lambda g, i: (0, 0))],
        out_specs=(pl.BlockSpec((tq, gw), lambda g, i: (i, g)),
                   pl.BlockSpec((None, tq, n_slc), lambda g, i: (g, i, 0))),
        compiler_params=_cparams(("arbitrary", "arbitrary")),
        name="nsa_cmp_select",
    )(scal, z, cmp_kv, cmp_kv, pos_col, pos_cmp)


def _slc_attn_kernel(sc_ref, q_ref, k_ref, v_ref, sel_ref, pq_ref, pkb_ref, o_ref,
                     vt_ref, sa_ref, sb_ref, *, tq, tk):
    g = pl.program_id(0)
    i = pl.program_id(1)

    @pl.when(i == 0)
    def _():
        _transpose_into(vt_ref, v_ref)

    n_slc = sel_ref.shape[1]
    q = (q_ref[...].astype(F32) * (LOG2E * NSA_DH ** -0.5)).astype(BF16)
    q4 = jnp.concatenate([q[:, r * NSA_DH:(r + 1) * NSA_DH] for r in range(NSA_REP)], 0)
    sel = sel_ref[...]
    pq = pq_ref[...]
    slopes = [sc_ref[g * NSA_REP + r] for r in range(NSA_REP)]
    blk = lax.broadcasted_iota(jnp.int32, (tk, n_slc), 1)
    kblk0 = lax.shift_right_logical(lax.broadcasted_iota(jnp.int32, (tk, n_slc), 0),
                                    int(math.log2(SLC_BLOCK)))

    def score(j):
        return _kq(k_ref[pl.ds(pl.multiple_of(j * tk, tk), tk), :], q4)

    def update(j, s_ref, carry, masked):
        off = pl.multiple_of(j * tk, tk)
        vt = vt_ref[:, pl.ds(off, tk)]
        expand = jnp.where(blk == kblk0 + j * (tk // SLC_BLOCK), 1.0, 0.0).astype(BF16)
        mk = _kq(expand, sel)
        if masked:
            kidx = off + lax.broadcasted_iota(jnp.int32, (tk, tq), 0)
            qidx = i * tq + lax.broadcasted_iota(jnp.int32, (tk, tq), 1)
            mk = jnp.where(kidx <= qidx, mk, 0.0)
        dist = jnp.abs(jnp.tile(pkb_ref[pl.ds(off, tk), :], (1, tq // LANE)) - pq)
        dist = jnp.where(mk > 0.5, dist, MASK_DIST)
        return tuple(_online_t(s_ref[:, r * tq:(r + 1) * tq] - slopes[r] * dist, vt, *carry[r])
                     for r in range(NSA_REP))

    carry = _sweep(i, score, update, tuple(_init_t(tq, NSA_DH) for _ in range(NSA_REP)), sa_ref, sb_ref)
    for r in range(NSA_REP):
        _, l, acc = carry[r]
        o_ref[:, r * NSA_DH:(r + 1) * NSA_DH] = (acc / l).T.astype(o_ref.dtype)


def _slc_attn(z, sel, pos_row, pos_b, scal):
    S = z.shape[0]
    n_slc = S // SLC_BLOCK
    tq, tk = 512, 256
    gw = NSA_REP * NSA_DH
    kcol = Z_NKV // LANE + 4
    return pl.pallas_call(
        functools.partial(_slc_attn_kernel, tq=tq, tk=tk),
        out_shape=jax.ShapeDtypeStruct((S, NSA_HEADS * NSA_DH), BF16),
        grid=(NSA_KV_HEADS, S // tq),
        in_specs=[pl.BlockSpec(memory_space=pltpu.SMEM),
                  pl.BlockSpec((tq, gw), lambda g, i: (i, Z_NQ // gw + g)),
                  pl.BlockSpec((S, NSA_DH), lambda g, i: (0, kcol + g)),
                  pl.BlockSpec((S, NSA_DH), lambda g, i: (0, kcol + 2 + g)),
                  pl.BlockSpec((None, tq, n_slc), lambda g, i: (g, i, 0)),
                  pl.BlockSpec((1, tq), lambda g, i: (0, i)),
                  pl.BlockSpec((S, LANE), lambda g, i: (0, 0))],
        out_specs=pl.BlockSpec((tq, gw), lambda g, i: (i, g)),
        scratch_shapes=[pltpu.VMEM((NSA_DH, S), BF16),
                        pltpu.VMEM((tk, NSA_REP * tq), F32), pltpu.VMEM((tk, NSA_REP * tq), F32)],
        compiler_params=_cparams(("arbitrary", "arbitrary")),
        name="nsa_slc_attn",
    )(scal, z, z, z, sel, pos_row, pos_b)


def _win_attn_kernel(sc_ref, q_ref, k_ref, v_ref, pq_ref, pk_ref, ocmp_ref, oslc_ref, gate_ref,
                     o_ref, *, t):
    g = pl.program_id(0)
    i = pl.program_id(1)
    q = q_ref[...] * jnp.asarray(NSA_DH ** -0.5, BF16)
    pq = pq_ref[...]
    slopes = [sc_ref[g * NSA_REP + r] for r in range(NSA_REP)]
    dif0 = (lax.broadcasted_iota(jnp.int32, (t, t), 0) - lax.broadcasted_iota(jnp.int32, (t, t), 1))

    def tile(jj, carry):
        j = i - jj
        off = pl.multiple_of(j * t, t)
        k = k_ref[pl.ds(off, t), :]
        v = v_ref[pl.ds(off, t), :]
        dif = dif0 + jj * t
        ok = (dif >= 0) & (dif < WINDOW)
        dist = jnp.abs(pq - pk_ref[:, pl.ds(off, t)])
        out = []
        for r in range(NSA_REP):
            s = _qk(q[:, r * NSA_DH:(r + 1) * NSA_DH], k) - slopes[r] * dist
            out.append(_online(jnp.where(ok, s, NEG), v, *carry[r], valid=ok))
        return tuple(out)

    n_back = (WINDOW + t - 1) // t
    carry = lax.fori_loop(0, jnp.minimum(i, n_back) + 1, tile,
                          tuple(_init(t, NSA_DH) for _ in range(NSA_REP)))
    gates = jax.nn.sigmoid(gate_ref[...].astype(F32))
    for r in range(NSA_REP):
        _, l, acc = carry[r]
        sl = slice(r * NSA_DH, (r + 1) * NSA_DH)

        def gcol(b):
            c0, c1 = 3 * r + b, 3 * (NSA_REP + r) + b
            return jnp.where(g == 0, gates[:, c0:c0 + 1], gates[:, c1:c1 + 1])

        o = (gcol(0) * ocmp_ref[:, sl].astype(F32) + gcol(1) * oslc_ref[:, sl].astype(F32)
             + gcol(2) * (acc / l))
        o_ref[:, sl] = o.astype(o_ref.dtype)


def _win_attn(z, o_cmp, o_slc, pos_col, pos_row, scal):
    S = z.shape[0]
    t = 256
    gw = NSA_REP * NSA_DH
    kcol = Z_NKV // LANE + 8
    return pl.pallas_call(
        functools.partial(_win_attn_kernel, t=t),
        out_shape=jax.ShapeDtypeStruct((S, NSA_HEADS * NSA_DH), BF16),
        grid=(NSA_KV_HEADS, S // t),
        in_specs=[pl.BlockSpec(memory_space=pltpu.SMEM),
                  pl.BlockSpec((t, gw), lambda g, i: (i, Z_NQ // gw + g)),
                  pl.BlockSpec((S, NSA_DH), lambda g, i: (0, kcol + g)),
                  pl.BlockSpec((S, NSA_DH), lambda g, i: (0, kcol + 2 + g)),
                  pl.BlockSpec((t, 1), lambda g, i: (i, 0)),
                  pl.BlockSpec((1, S), lambda g, i: (0, 0)),
                  pl.BlockSpec((t, gw), lambda g, i: (i, g)),
                  pl.BlockSpec((t, gw), lambda g, i: (i, g)),
                  pl.BlockSpec((t, LANE), lambda g, i: (i, Z_NG // LANE))],
        out_specs=pl.BlockSpec((t, gw), lambda g, i: (i, g)),
        compiler_params=_cparams(("arbitrary", "arbitrary")),
        name="nsa_win_attn",
    )(scal, z, z, z, pos_col, pos_row, o_cmp, o_slc, z)


def kernel(x, c, positions, w_ada, b_ada, w_in, mla_q_norm, mla_kv_norm, mla_w_uq, mla_w_ukv, diff_lambda, diff_subln, nsa_cmp_w1, nsa_cmp_b1, nsa_cmp_w2, w_branch, w_out, ln1_g, ln1_b, w_ff1, w_ff2, ln2_g, ln2_b):
    B, S, D = x.shape
    L = w_ada.shape[0]
    assert B == 1 and D == D_MODEL and S % 512 == 0 and S // SLC_BLOCK >= 3

    w_in_x = _relayout_w_in(w_in)
    w_uq_x = _relayout_w_uq(mla_w_uq)
    w_ukv_b = mla_w_ukv.astype(BF16)
    w_branch_b = w_branch.astype(BF16)
    w_out_b = w_out.astype(BF16)
    w_ff1_b = w_ff1.astype(BF16)
    w_ff2_b = w_ff2.astype(BF16)

    slopes = 2.0 ** (-8.0 * np.arange(1, N_ALIBI_HEADS + 1, dtype=np.float64) / N_ALIBI_HEADS)
    s_diff, s_nsa = slopes[0::2], slopes[1::2]
    nsa_scal = jnp.asarray(s_nsa, F32)
    nsa_scal2 = jnp.asarray(s_nsa * LOG2E, F32)

    pos_f = positions.astype(F32)
    pos_col = pos_f.reshape(S, 1)
    pos_row = pos_f.reshape(1, S)
    pos_b = jnp.broadcast_to(pos_col, (S, LANE))
    pos_cmp = jnp.pad(pos_f[0, CMP_BLOCK - 1::CMP_STRIDE], (0, 1)).reshape(1, S // CMP_STRIDE)

    mod = _ada(c, w_ada, b_ada)
    rope_t = _rope_table(pos_col)
    xs = x.reshape(S, D)

    for l in range(L):
        lam_init = 0.8 - 0.6 * math.exp(-0.3 * l)
        diff_scal = jnp.asarray(np.concatenate([s_diff * LOG2E, [lam_init]]), F32)

        h = _lnmod(xs, mod, l, 0, 1)
        z = _matmul(h, w_in_x, l, tm=1024, tn=1280, relu2=False, name="in_proj")

        qm, km, vm = _mla_proj(z, rope_t, mla_q_norm, mla_kv_norm, w_uq_x, w_ukv_b, l)
        o_mla = _mla_attn(qm, km, vm)

        o_diff = _diff_attn(z, pos_row, pos_b, diff_scal, diff_lambda, diff_subln, l)

        xc = z[:, Z_NKV:Z_NKV + 4 * NSA_DH].reshape(S // CMP_STRIDE, CMP_STRIDE, 4, NSA_DH)
        xc = xc.transpose(2, 0, 1, 3).reshape(4, S // CMP_STRIDE, CMP_STRIDE * NSA_DH)
        cmp_kv = _compress(xc, nsa_cmp_w1, nsa_cmp_b1, nsa_cmp_w2, l)
        o_cmp, sel = _cmp_sel(z, cmp_kv, pos_col, pos_cmp, nsa_scal)
        o_slc = _slc_attn(z, sel, pos_row, pos_b, nsa_scal2)
        o_nsa = _win_attn(z, o_cmp, o_slc, pos_col, pos_row, nsa_scal)

        merged = _merge(o_mla, o_diff, o_nsa, w_branch_b, z, l)
        xs = _matmul_res_ln(merged, w_out_b, xs, mod, ln1_g, ln1_b, l, 2, ALPHA, "out_proj_ln")

        h = _lnmod(xs, mod, l, 3, 4)
        u = _matmul(h, w_ff1_b, l, tm=1024, tn=1024, relu2=True, name="ff1")
        xs = _matmul_res_ln(u, w_ff2_b, xs, mod, ln2_g, ln2_b, l, 5, ALPHA, "ff2_ln")

    return xs.reshape(B, S, D)
```

```python
import functools
import math

import numpy as np
import jax
import jax.numpy as jnp
from jax import lax
from jax.experimental import pallas as pl
from jax.experimental.pallas import tpu as pltpu

F32 = jnp.float32
BF16 = jnp.bfloat16

D_MODEL = 2048
MLA_HEADS = 8
MLA_Q_LORA = 512
MLA_KV_LORA = 256
MLA_NOPE = 128
MLA_ROPE = 64
MLA_V = 128
ROPE_THETA = 10000.0
DIFF_HEADS = 8
DIFF_QK = 64
DIFF_V = 128
NSA_HEADS = 8
NSA_KV_HEADS = 2
NSA_REP = NSA_HEADS // NSA_KV_HEADS
NSA_DH = 128
CMP_BLOCK = 32
CMP_STRIDE = 16
SLC_BLOCK = 64
N_SELECT = 16
WINDOW = 512
N_ALIBI_HEADS = DIFF_HEADS + NSA_HEADS
N_BRANCH = 3
BRANCH_WIDTH = 1024
D_FF = 4 * D_MODEL
N_MOD = 6
DEPTH = 4
ALPHA = (2.0 * DEPTH) ** 0.25

LANE = 128
NEG = -1e30
LOG2E = math.log2(math.e)
T_CHUNK = 512
MASK_DIST = 1e33
VMEM_LIMIT = 48 * 1024 * 1024

Z_CQ = 0
Z_DQ = 512
Z_DK = 1536
Z_DV = 2560
Z_NQ = 3584
Z_NKV = 4608
Z_MG = 6144
Z_CKV = 12288
Z_KR = 12544
Z_NG = 12672
Z_W = 12800


def _cparams(sem):
    return pltpu.CompilerParams(dimension_semantics=sem, vmem_limit_bytes=VMEM_LIMIT)


def _rot_half(w):
    half = MLA_ROPE // 2
    return jnp.concatenate([-w[..., half:], w[..., :half]], axis=-1)


def _relayout_w_in(w_in):
    o = np.cumsum([0, 512, 256, 64, 1024, 1024, 1024, 1024, 1536, 24, N_BRANCH * D_MODEL])
    cq, ckv, kr, dq, dk, dv, nq, nkv, ng, mg = (w_in[..., o[k]:o[k + 1]] for k in range(10))
    pad = jnp.zeros(w_in.shape[:-1] + (Z_W - Z_NG - ng.shape[-1],), w_in.dtype)
    return jnp.concatenate([cq, dq, dk, dv, nq, nkv, mg, ckv, kr, _rot_half(kr), ng, pad],
                           axis=-1).astype(BF16)


def _relayout_w_uq(w_uq):
    L, K, _ = w_uq.shape
    w = w_uq.reshape(L, K, MLA_HEADS, MLA_NOPE + MLA_ROPE)
    rope = w[..., MLA_NOPE:]
    return jnp.concatenate([w, _rot_half(rope)], axis=-1).reshape(L, K, MLA_HEADS * 256).astype(BF16)


def _ada_kernel(c_ref, w_ref, b_ref, o_ref):
    c = c_ref[...]
    ca = c * jax.nn.sigmoid(c)
    ca8 = jnp.broadcast_to(ca, (8, c.shape[1])).astype(BF16)
    r = jnp.dot(ca8, w_ref[...].astype(BF16), preferred_element_type=F32)
    o_ref[...] = r[0:1, :] + b_ref[...]


def _ada(c, w_ada, b_ada):
    L, D, N = w_ada.shape
    tn = 1536
    return pl.pallas_call(
        _ada_kernel,
        out_shape=jax.ShapeDtypeStruct((L, 1, N), F32),
        grid=(L, N // tn),
        in_specs=[pl.BlockSpec((1, D), lambda l, j: (0, 0)),
                  pl.BlockSpec((None, D, tn), lambda l, j: (l, 0, j)),
                  pl.BlockSpec((None, 1, tn), lambda l, j: (l, 0, j))],
        out_specs=pl.BlockSpec((None, 1, tn), lambda l, j: (l, 0, j)),
        compiler_params=_cparams(("arbitrary", "arbitrary")),
        name="ada_mod",
    )(c, w_ada, b_ada.reshape(L, 1, N))


def _lnmod_kernel(x_ref, sh_ref, sc_ref, o_ref):
    x = x_ref[...]
    mu = jnp.mean(x, -1, keepdims=True)
    xc = x - mu
    var = jnp.mean(xc * xc, -1, keepdims=True)
    y = xc * lax.rsqrt(var + 1e-5)
    o_ref[...] = (y * (1.0 + sc_ref[...]) + sh_ref[...]).astype(o_ref.dtype)


def _lnmod(x, mod, l, k_shift, k_scale):
    S, D = x.shape
    tm = 512
    return pl.pallas_call(
        _lnmod_kernel,
        out_shape=jax.ShapeDtypeStruct((S, D), BF16),
        grid=(S // tm,),
        in_specs=[pl.BlockSpec((tm, D), lambda i: (i, 0)),
                  pl.BlockSpec((None, 1, D), lambda i: (l, 0, k_shift)),
                  pl.BlockSpec((None, 1, D), lambda i: (l, 0, k_scale))],
        out_specs=pl.BlockSpec((tm, D), lambda i: (i, 0)),
        compiler_params=_cparams(("arbitrary",)),
        name="ln_modulate",
    )(x, mod, mod)


def _mm_kernel(x_ref, w_ref, o_ref, *, relu2):
    r = jnp.dot(x_ref[...], w_ref[...], preferred_element_type=F32)
    if relu2:
        r = jnp.square(jnp.maximum(r, 0.0))
    o_ref[...] = r.astype(o_ref.dtype)


def _matmul(x, w, l, *, tm, tn, relu2, name):
    S, K = x.shape
    N = w.shape[-1]
    return pl.pallas_call(
        functools.partial(_mm_kernel, relu2=relu2),
        out_shape=jax.ShapeDtypeStruct((S, N), BF16),
        grid=(S // tm, N // tn),
        in_specs=[pl.BlockSpec((tm, K), lambda i, j: (i, 0)),
                  pl.BlockSpec((None, K, tn), lambda i, j: (l, 0, j))],
        out_specs=pl.BlockSpec((tm, tn), lambda i, j: (i, j)),
        compiler_params=_cparams(("arbitrary", "arbitrary")),
        name=name,
    )(x, w)


def _mm_res_ln_kernel(a_ref, w_ref, x_ref, gate_ref, lng_ref, lnb_ref, o_ref, *, nk, alpha):
    k = pl.program_id(1)
    part = jnp.dot(a_ref[...], w_ref[...], preferred_element_type=F32)

    @pl.when(k == 0)
    def _():
        o_ref[...] = part

    @pl.when(k > 0)
    def _():
        o_ref[...] += part

    @pl.when(k == nk - 1)
    def _():
        y = alpha * x_ref[...] + gate_ref[...] * o_ref[...]
        mu = jnp.mean(y, -1, keepdims=True)
        yc = y - mu
        var = jnp.mean(yc * yc, -1, keepdims=True)
        o_ref[...] = yc * lax.rsqrt(var + 1e-5) * lng_ref[...] + lnb_ref[...]


def _matmul_res_ln(a, w, x, mod, lng, lnb, l, k_gate, alpha, name):
    S, K = a.shape
    D = x.shape[1]
    L = lng.shape[0]
    tm, tk = 512, 2048
    nk = K // tk
    return pl.pallas_call(
        functools.partial(_mm_res_ln_kernel, nk=nk, alpha=alpha),
        out_shape=jax.ShapeDtypeStruct((S, D), F32),
        grid=(S // tm, nk),
        in_specs=[pl.BlockSpec((tm, tk), lambda i, k: (i, k)),
                  pl.BlockSpec((None, tk, D), lambda i, k: (l, k, 0)),
                  pl.BlockSpec((tm, D), lambda i, k: (i, 0)),
                  pl.BlockSpec((None, 1, D), lambda i, k: (l, 0, k_gate)),
                  pl.BlockSpec((None, 1, D), lambda i, k: (l, 0, 0)),
                  pl.BlockSpec((None, 1, D), lambda i, k: (l, 0, 0))],
        out_specs=pl.BlockSpec((tm, D), lambda i, k: (i, 0)),
        compiler_params=_cparams(("arbitrary", "arbitrary")),
        name=name,
    )(a, w, x, mod, lng.reshape(L, 1, D), lnb.reshape(L, 1, D))


def _merge_kernel(o0_ref, o1_ref, o2_ref, w_ref, g0_ref, g1_ref, g2_ref, out_ref):
    acc = None
    for n, (o_ref, g_ref) in enumerate(((o0_ref, g0_ref), (o1_ref, g1_ref), (o2_ref, g2_ref))):
        t = jax.nn.sigmoid(g_ref[...].astype(F32)) * jnp.dot(o_ref[...], w_ref[n],
                                                               preferred_element_type=F32)
        acc = t if acc is None else acc + t
    out_ref[...] = acc.astype(out_ref.dtype)


def _merge(o_mla, o_diff, o_nsa, w_branch, z, l):
    S = o_mla.shape[0]
    tm, tn = 1024, 512
    bspec = pl.BlockSpec((tm, BRANCH_WIDTH), lambda i, j: (i, 0))

    def gspec(n):
        return pl.BlockSpec((tm, tn), lambda i, j: (i, (Z_MG + D_MODEL * n) // tn + j))

    return pl.pallas_call(
        _merge_kernel,
        out_shape=jax.ShapeDtypeStruct((S, D_MODEL), BF16),
        grid=(S // tm, D_MODEL // tn),
        in_specs=[bspec, bspec, bspec,
                  pl.BlockSpec((None, N_BRANCH, BRANCH_WIDTH, tn), lambda i, j: (l, 0, 0, j)),
                  gspec(0), gspec(1), gspec(2)],
        out_specs=pl.BlockSpec((tm, tn), lambda i, j: (i, j)),
        compiler_params=_cparams(("arbitrary", "arbitrary")),
        name="branch_merge",
    )(o_mla, o_diff, o_nsa, w_branch, z, z, z)


def _rope_kernel(pos_ref, inv_ref, o_ref):
    ang = pos_ref[...] * inv_ref[...]
    lane = lax.broadcasted_iota(jnp.int32, ang.shape, 1)
    o_ref[...] = jnp.where(lane < MLA_ROPE, jnp.cos(ang), jnp.sin(ang))


def _rope_table(pos_col):
    S = pos_col.shape[0]
    inv = ROPE_THETA ** (-np.arange(0, MLA_ROPE, 2, dtype=np.float64) / MLA_ROPE)
    inv4 = jnp.asarray(np.tile(inv, 4)[None, :], F32)
    tm = 512
    return pl.pallas_call(
        _rope_kernel,
        out_shape=jax.ShapeDtypeStruct((S, LANE), F32),
        grid=(S // tm,),
        in_specs=[pl.BlockSpec((tm, 1), lambda i: (i, 0)),
                  pl.BlockSpec((1, LANE), lambda i: (0, 0))],
        out_specs=pl.BlockSpec((tm, LANE), lambda i: (i, 0)),
        compiler_params=_cparams(("arbitrary",)),
        name="rope_table",
    )(pos_col, inv4)


def _rms(x, g):
    return x * lax.rsqrt(jnp.mean(x * x, -1, keepdims=True) + 1e-6) * g


def _mla_proj_kernel(cq_ref, ckv_ref, kr_ref, t_ref, gq_ref, gkv_ref, wuq_ref, wukv_ref,
                     q_out, k_out, v_out):
    scale = LOG2E * (MLA_NOPE + MLA_ROPE) ** -0.5
    t = t_ref[...]
    nq = _rms(cq_ref[...].astype(F32), gq_ref[...]).astype(BF16)
    q = jnp.dot(nq, wuq_ref[...], preferred_element_type=F32)
    for h in range(MLA_HEADS):
        c = 256 * h
        q_out[:, c:c + 128] = (q[:, c:c + 128] * scale).astype(BF16)
        q_out[:, c + 128:c + 256] = (q[:, c + 128:c + 256] * t * scale).astype(BF16)
    nkv = _rms(ckv_ref[...].astype(F32), gkv_ref[...]).astype(BF16)
    kv = jnp.dot(nkv, wukv_ref[...], preferred_element_type=F32)
    kr = kr_ref[...].astype(F32) * t
    kr2 = (kr + pltpu.roll(kr, MLA_ROPE, 1)).astype(BF16)
    for h in range(MLA_HEADS):
        c = 256 * h
        k_out[:, c:c + 128] = kv[:, c:c + 128].astype(BF16)
        k_out[:, c + 128:c + 256] = kr2
        v_out[:, 128 * h:128 * h + 128] = kv[:, c + 128:c + 256].astype(BF16)


def _mla_proj(z, rope_t, gq, gkv, wuq, wukv, l):
    S = z.shape[0]
    L = gq.shape[0]
    tm = 512
    HW = MLA_HEADS * 256
    return pl.pallas_call(
        _mla_proj_kernel,
        out_shape=(jax.ShapeDtypeStruct((S, HW), BF16),
                   jax.ShapeDtypeStruct((S, HW), BF16),
                   jax.ShapeDtypeStruct((S, MLA_HEADS * MLA_V), BF16)),
        grid=(S // tm,),
        in_specs=[pl.BlockSpec((tm, MLA_Q_LORA), lambda i: (i, Z_CQ // MLA_Q_LORA)),
                  pl.BlockSpec((tm, MLA_KV_LORA), lambda i: (i, Z_CKV // MLA_KV_LORA)),
                  pl.BlockSpec((tm, LANE), lambda i: (i, Z_KR // LANE)),
                  pl.BlockSpec((tm, LANE), lambda i: (i, 0)),
                  pl.BlockSpec((None, 1, MLA_Q_LORA), lambda i: (l, 0, 0)),
                  pl.BlockSpec((None, 1, MLA_KV_LORA), lambda i: (l, 0, 0)),
                  pl.BlockSpec((None, MLA_Q_LORA, HW), lambda i: (l, 0, 0)),
                  pl.BlockSpec((None, MLA_KV_LORA, HW), lambda i: (l, 0, 0))],
        out_specs=(pl.BlockSpec((tm, HW), lambda i: (i, 0)),
                   pl.BlockSpec((tm, HW), lambda i: (i, 0)),
                   pl.BlockSpec((tm, MLA_HEADS * MLA_V), lambda i: (i, 0))),
        compiler_params=_cparams(("arbitrary",)),
        name="mla_proj",
    )(z, z, z, rope_t, gq.reshape(L, 1, -1), gkv.reshape(L, 1, -1), wuq, wukv)


def _qk(q, k):
    return lax.dot_general(q, k, (((1,), (1,)), ((), ())), preferred_element_type=F32)


def _online(s, v, m, l, acc, valid=None):
    m_new = jnp.maximum(m, jnp.max(s, -1, keepdims=True))
    a = jnp.exp(m - m_new)
    p = jnp.exp(s - m_new)
    if valid is not None:
        p = jnp.where(valid, p, 0.0)
    l = a * l + jnp.sum(p, -1, keepdims=True)
    acc = a * acc + jnp.dot(p.astype(BF16), v, preferred_element_type=F32)
    return m_new, l, acc


def _init(rows, dv):
    return (jnp.full((rows, 1), NEG, F32), jnp.zeros((rows, 1), F32), jnp.zeros((rows, dv), F32))


def _kq(k, q):
    return lax.dot_general(k, q, (((1,), (1,)), ((), ())), preferred_element_type=F32)


def _online_t(s, vt, m, l, acc):
    m_new = jnp.maximum(m, jnp.max(s, 0, keepdims=True))
    a = jnp.exp2(m - m_new)
    p = jnp.exp2(s - m_new)
    l = a * l + jnp.sum(p, 0, keepdims=True)
    acc = a * acc + jnp.dot(vt, p.astype(BF16), preferred_element_type=F32)
    return m_new, l, acc


def _init_t(nq, dv):
    return (jnp.full((1, nq), NEG, F32), jnp.zeros((1, nq), F32), jnp.zeros((dv, nq), F32))


def _transpose_into(vt_ref, v_ref):
    for c in range(v_ref.shape[0] // T_CHUNK):
        rows = slice(c * T_CHUNK, (c + 1) * T_CHUNK)
        vt_ref[:, rows] = v_ref[rows, :].astype(F32).T.astype(vt_ref.dtype)


def _put_scores(buf, t, lanes=None):
    s_ref, m_ref = buf
    if lanes is None:
        s_ref[...] = t
        m_ref[...] = jnp.max(t, 0, keepdims=True)
    else:
        s_ref[:, lanes] = t
        m_ref[:, lanes] = jnp.max(t, 0, keepdims=True)


def _sweep(n_pairs, tk, score, vt_ref, carry, buf_a, buf_b):
    def update(j, buf, carry):
        s_ref, mt_ref = buf
        m, l, acc = carry
        m_new = jnp.maximum(m, mt_ref[...])
        a = jnp.exp2(m - m_new)
        p = jnp.exp2(s_ref[...] - m_new)
        l = a * l + jnp.sum(p, 0, keepdims=True)
        vt = vt_ref[:, pl.ds(pl.multiple_of(j * tk, tk), tk)]
        acc = a * acc + jnp.dot(vt, p.astype(BF16), preferred_element_type=F32)
        return m_new, l, acc

    jd = 2 * n_pairs
    score(jd, buf_a, True)
    score(jd + 1, buf_b, True)
    carry = update(jd, buf_a, carry)

    def pair(p, carry):
        j = 2 * p
        score(j, buf_a, False)
        carry = update(jnp.where(p == 0, jd + 1, j - 1), buf_b, carry)
        score(j + 1, buf_b, False)
        return update(j, buf_a, carry)

    carry = lax.fori_loop(0, n_pairs, pair, carry)
    return update(jd - 1 + 2 * (n_pairs == 0).astype(jnp.int32), buf_b, carry)


def _score_bufs(tk, nq):
    return [pltpu.VMEM((tk, nq), F32), pltpu.VMEM((1, nq), F32),
            pltpu.VMEM((tk, nq), F32), pltpu.VMEM((1, nq), F32)]


def _mla_attn_kernel(q_ref, k_ref, v_ref, o_ref, vt_ref, sa_ref, ma_ref, sb_ref, mb_ref, *, tq, tk):
    i = pl.program_id(1)

    @pl.when(i == 0)
    def _():
        _transpose_into(vt_ref, v_ref)

    q = q_ref[...]

    def score(j, buf, masked):
        off = pl.multiple_of(j * tk, tk)
        t = _kq(k_ref[pl.ds(off, tk), :], q)
        if masked:
            kidx = off + lax.broadcasted_iota(jnp.int32, (tk, tq), 0)
            qidx = i * tq + lax.broadcasted_iota(jnp.int32, (tk, tq), 1)
            t = jnp.where(kidx <= qidx, t, NEG)
        _put_scores(buf, t)

    _, l, acc = _sweep(i, tk, score, vt_ref, _init_t(tq, MLA_V), (sa_ref, ma_ref), (sb_ref, mb_ref))
    o_ref[...] = (acc / l).T.astype(o_ref.dtype)


def _mla_attn(qm, km, vm):
    S = qm.shape[0]
    tq, tk = 1024, 512
    return pl.pallas_call(
        functools.partial(_mla_attn_kernel, tq=tq, tk=tk),
        out_shape=jax.ShapeDtypeStruct((S, MLA_HEADS * MLA_V), BF16),
        grid=(MLA_HEADS, S // tq),
        in_specs=[pl.BlockSpec((tq, 256), lambda h, i: (i, h)),
                  pl.BlockSpec((S, 256), lambda h, i: (0, h)),
                  pl.BlockSpec((S, MLA_V), lambda h, i: (0, h))],
        out_specs=pl.BlockSpec((tq, MLA_V), lambda h, i: (i, h)),
        scratch_shapes=[pltpu.VMEM((MLA_V, S), BF16)] + _score_bufs(tk, tq),
        compiler_params=_cparams(("arbitrary", "arbitrary")),
        name="mla_attn",
    )(qm, km, vm)


def _diff_attn_kernel(sc_ref, q_ref, k_ref, v_ref, pq_ref, pkb_ref, lam_ref, g_ref, o_ref,
                      vt_ref, sa_ref, ma_ref, sb_ref, mb_ref, *, tq, tk):
    h = pl.program_id(0)
    i = pl.program_id(1)

    @pl.when(i == 0)
    def _():
        _transpose_into(vt_ref, v_ref)

    slope = sc_ref[h]
    lam_init = sc_ref[DIFF_HEADS]
    qs = (q_ref[...].astype(F32) * (LOG2E * DIFF_QK ** -0.5)).astype(BF16)
    lane = lax.broadcasted_iota(jnp.int32, qs.shape, 1)
    zero = jnp.zeros_like(qs)
    q2 = jnp.concatenate([jnp.where(lane < DIFF_QK, qs, zero), jnp.where(lane >= DIFF_QK, qs, zero)], 0)
    pq = pq_ref[...]

    def score(j, buf, masked):
        off = pl.multiple_of(j * tk, tk)
        s = _kq(k_ref[pl.ds(off, tk), :], q2)
        bias = slope * jnp.abs(jnp.tile(pkb_ref[pl.ds(off, tk), :], (1, tq // LANE)) - pq)
        if masked:
            kidx = off + lax.broadcasted_iota(jnp.int32, (tk, tq), 0)
            qidx = i * tq + lax.broadcasted_iota(jnp.int32, (tk, tq), 1)
            bias = jnp.where(kidx <= qidx, bias, -NEG)
        for c in range(2):
            lanes = slice(c * tq, (c + 1) * tq)
            _put_scores(buf, s[:, lanes] - bias, lanes)

    _, l, acc = _sweep(i, tk, score, vt_ref, _init_t(2 * tq, DIFF_V), (sa_ref, ma_ref), (sb_ref, mb_ref))
    on = acc / l
    lp = lam_ref[...]
    lam = (jnp.exp(jnp.sum(lp[0:1] * lp[1:2], -1, keepdims=True))
           - jnp.exp(jnp.sum(lp[2:3] * lp[3:4], -1, keepdims=True)) + lam_init)
    o = on[:, :tq] - lam * on[:, tq:]
    o = o * lax.rsqrt(jnp.mean(o * o, 0, keepdims=True) + 1e-6)
    o_ref[...] = (o.T * g_ref[...] * (1.0 - lam_init)).astype(o_ref.dtype)


def _diff_attn(z, pos_row, pos_b, scal, lam_p, g_sub, l):
    S = z.shape[0]
    L = g_sub.shape[0]
    tq, tk = 1024, 512
    return pl.pallas_call(
        functools.partial(_diff_attn_kernel, tq=tq, tk=tk),
        out_shape=jax.ShapeDtypeStruct((S, DIFF_HEADS * DIFF_V), BF16),
        grid=(DIFF_HEADS, S // tq),
        in_specs=[pl.BlockSpec(memory_space=pltpu.SMEM),
                  pl.BlockSpec((tq, LANE), lambda h, i: (i, Z_DQ // LANE + h)),
                  pl.BlockSpec((S, LANE), lambda h, i: (0, Z_DK // LANE + h)),
                  pl.BlockSpec((S, LANE), lambda h, i: (0, Z_DV // LANE + h)),
                  pl.BlockSpec((1, tq), lambda h, i: (0, i)),
                  pl.BlockSpec((S, LANE), lambda h, i: (0, 0)),
                  pl.BlockSpec((None, 4, DIFF_QK), lambda h, i: (l, 0, 0)),
                  pl.BlockSpec((None, 1, DIFF_V), lambda h, i: (l, 0, 0))],
        out_specs=pl.BlockSpec((tq, DIFF_V), lambda h, i: (i, h)),
        scratch_shapes=[pltpu.VMEM((DIFF_V, S), BF16)] + _score_bufs(tk, 2 * tq),
        compiler_params=_cparams(("arbitrary", "arbitrary")),
        name="diff_attn",
    )(scal, z, z, z, pos_row, pos_b, lam_p, g_sub.reshape(L, 1, -1))


def _compress_kernel(x_ref, w1_ref, b1_ref, w2_ref, o_ref):
    x = x_ref[...]
    half = CMP_STRIDE * NSA_DH
    a = jnp.dot(x, w1_ref[0:half, :].astype(BF16), preferred_element_type=F32)
    b = jnp.dot(x, w1_ref[half:2 * half, :].astype(BF16), preferred_element_type=F32)
    n = x.shape[0]
    pre = a + pltpu.roll(b, n - 1, 0) + b1_ref[...]
    hdn = 0.5 * pre * (1.0 + jnp.tanh(0.7978845608028654 * (pre + 0.044715 * pre * pre * pre)))
    o_ref[...] = jnp.dot(hdn.astype(BF16), w2_ref[...].astype(BF16),
                         preferred_element_type=F32).astype(o_ref.dtype)


def _compress(xc, w1, b1, w2, l):
    _, nb, kk = xc.shape
    L = w1.shape[0]
    return pl.pallas_call(
        _compress_kernel,
        out_shape=jax.ShapeDtypeStruct((4, nb, NSA_DH), BF16),
        grid=(4,),
        in_specs=[pl.BlockSpec((None, nb, kk), lambda c: (c, 0, 0)),
                  pl.BlockSpec((None, None, 2 * kk, NSA_DH), lambda c: (l, c // 2, 0, 0)),
                  pl.BlockSpec((None, None, 1, NSA_DH), lambda c: (l, c // 2, 0, 0)),
                  pl.BlockSpec((None, None, NSA_DH, NSA_DH), lambda c: (l, c // 2, 0, 0))],
        out_specs=pl.BlockSpec((None, nb, NSA_DH), lambda c: (c, 0, 0)),
        compiler_params=_cparams(("arbitrary",)),
        name="nsa_compress",
    )(xc, w1, b1.reshape(L, 2, 1, NSA_DH), w2)


def _split3(x):
    hi = x.astype(BF16)
    r = x - hi.astype(F32)
    mid = r.astype(BF16)
    lo = (r - mid.astype(F32)).astype(BF16)
    return hi, mid, lo


def _cmp_sel_kernel(sc_ref, q_ref, kc_ref, vc_ref, pq_ref, pc_ref, o_ref, sel_ref, *, tq, n_slc):
    g = pl.program_id(0)
    i = pl.program_id(1)
    nc = kc_ref.shape[0]
    q = q_ref[...] * jnp.asarray(NSA_DH ** -0.5, BF16)
    kc = kc_ref[...]
    vc = vc_ref[...]
    pq = pq_ref[...]
    qidx = i * tq + lax.broadcasted_iota(jnp.int32, (tq, nc), 0)
    cmp_end = lax.broadcasted_iota(jnp.int32, (tq, nc), 1) * CMP_STRIDE + (CMP_BLOCK - 1)
    vis = cmp_end <= qidx
    dist = jnp.abs(pq - pc_ref[...])
    p_sum = jnp.zeros((tq, nc), F32)
    for r in range(NSA_REP):
        slope = sc_ref[g * NSA_REP + r]
        s = _qk(q[:, r * NSA_DH:(r + 1) * NSA_DH], kc) - slope * dist
        s = jnp.where(vis, s, NEG)
        m = jnp.max(s, -1, keepdims=True)
        e = jnp.where(vis, jnp.exp(s - m), 0.0)
        d = jnp.sum(e, -1, keepdims=True)
        p = e / jnp.where(d > 0.0, d, 1.0)
        p_sum = p_sum + p
        o_ref[:, r * NSA_DH:(r + 1) * NSA_DH] = jnp.dot(
            p.astype(BF16), vc, preferred_element_type=F32).astype(o_ref.dtype)
    nn = lax.broadcasted_iota(jnp.int32, (nc, n_slc), 0)
    jj = lax.broadcasted_iota(jnp.int32, (nc, n_slc), 1)
    lo_n = jj * (SLC_BLOCK // CMP_STRIDE) - 1
    ovl = jnp.where((nn >= lo_n) & (nn <= lo_n + 4), 1.0, 0.0).astype(BF16)
    imp = sum(jnp.dot(part, ovl, preferred_element_type=F32) for part in _split3(p_sum))
    blk = lax.broadcasted_iota(jnp.int32, (tq, n_slc), 1).astype(F32)
    cur = ((i * tq + lax.broadcasted_iota(jnp.int32, (tq, n_slc), 0)) // SLC_BLOCK).astype(F32)
    forced = (blk == 0) | (blk == cur) | (blk == cur - 1)
    causal = blk <= cur
    score = jnp.where(forced, 3e38, jnp.where(causal, imp, -1.0))
    sel = jnp.zeros((tq, n_slc), F32)
    for _ in range(min(N_SELECT, n_slc)):
        mx = jnp.max(score, -1, keepdims=True)
        first = jnp.min(jnp.where(score == mx, blk, float(n_slc)), -1, keepdims=True)
        pick = blk == first
        sel = jnp.where(pick, 1.0, sel)
        score = jnp.where(pick, -2.0, score)
    sel_ref[...] = jnp.where(causal, sel, 0.0).astype(sel_ref.dtype)


def _cmp_sel(z, cmp_kv, pos_col, pos_cmp, scal):
    S = z.shape[0]
    nc = cmp_kv.shape[1]
    n_slc = S // SLC_BLOCK
    tq = 256
    gw = NSA_REP * NSA_DH
    return pl.pallas_call(
        functools.partial(_cmp_sel_kernel, tq=tq, n_slc=n_slc),
        out_shape=(jax.ShapeDtypeStruct((S, NSA_HEADS * NSA_DH), BF16),
                   jax.ShapeDtypeStruct((NSA_KV_HEADS, S, n_slc), BF16)),
        grid=(NSA_KV_HEADS, S // tq),
        in_specs=[pl.BlockSpec(memory_space=pltpu.SMEM),
                  pl.BlockSpec((tq, gw), lambda g, i: (i, Z_NQ // gw + g)),
                  pl.BlockSpec((None, nc, NSA_DH), lambda g, i: (g, 0, 0)),
                  pl.BlockSpec((None, nc, NSA_DH), lambda g, i: (2 + g, 0, 0)),
                  pl.BlockSpec((tq, 1), lambda g, i: (i, 0)),
                  pl.BlockSpec((1, nc), lambda g, i: (0, 0))],
        out_specs=(pl.BlockSpec((tq, gw), lambda g, i: (i, g)),
                   pl.BlockSpec((None, tq, n_slc), lambda g, i: (g, i, 0))),
        compiler_params=_cparams(("arbitrary", "arbitrary")),
        name="nsa_cmp_select",
    )(scal, z, cmp_kv, cmp_kv, pos_col, pos_cmp)


def _slc_attn_kernel(sc_ref, q_ref, k_ref, v_ref, sel_ref, pq_ref, pkb_ref, o_ref,
                     vt_ref, sa_ref, ma_ref, sb_ref, mb_ref, *, tq, tk):
    g = pl.program_id(0)
    i = pl.program_id(1)

    @pl.when(i == 0)
    def _():
        _transpose_into(vt_ref, v_ref)

    n_slc = sel_ref.shape[1]
    q = (q_ref[...].astype(F32) * (LOG2E * NSA_DH ** -0.5)).astype(BF16)
    q4 = jnp.concatenate([q[:, r * NSA_DH:(r + 1) * NSA_DH] for r in range(NSA_REP)], 0)
    sel = sel_ref[...]
    pq = pq_ref[...]
    slopes = [sc_ref[g * NSA_REP + r] for r in range(NSA_REP)]
    blk = lax.broadcasted_iota(jnp.int32, (tk, n_slc), 1)
    kblk0 = lax.shift_right_logical(lax.broadcasted_iota(jnp.int32, (tk, n_slc), 0),
                                    int(math.log2(SLC_BLOCK)))

    def score(j, buf, masked):
        off = pl.multiple_of(j * tk, tk)
        s = _kq(k_ref[pl.ds(off, tk), :], q4)
        expand = jnp.where(blk == kblk0 + j * (tk // SLC_BLOCK), 1.0, 0.0).astype(BF16)
        mk = _kq(expand, sel)
        if masked:
            kidx = off + lax.broadcasted_iota(jnp.int32, (tk, tq), 0)
            qidx = i * tq + lax.broadcasted_iota(jnp.int32, (tk, tq), 1)
            mk = jnp.where(kidx <= qidx, mk, 0.0)
        dist = jnp.abs(jnp.tile(pkb_ref[pl.ds(off, tk), :], (1, tq // LANE)) - pq)
        dist = jnp.where(mk > 0.5, dist, MASK_DIST)
        for r in range(NSA_REP):
            lanes = slice(r * tq, (r + 1) * tq)
            _put_scores(buf, s[:, lanes] - slopes[r] * dist, lanes)

    _, l, acc = _sweep(i, tk, score, vt_ref, _init_t(NSA_REP * tq, NSA_DH), (sa_ref, ma_ref), (sb_ref, mb_ref))
    on = acc / l
    for r in range(NSA_REP):
        o_ref[:, r * NSA_DH:(r + 1) * NSA_DH] = on[:, r * tq:(r + 1) * tq].T.astype(o_ref.dtype)


def _slc_attn(z, sel, pos_row, pos_b, scal):
    S = z.shape[0]
    n_slc = S // SLC_BLOCK
    tq, tk = 512, 256
    gw = NSA_REP * NSA_DH
    kcol = Z_NKV // LANE + 4
    return pl.pallas_call(
        functools.partial(_slc_attn_kernel, tq=tq, tk=tk),
        out_shape=jax.ShapeDtypeStruct((S, NSA_HEADS * NSA_DH), BF16),
        grid=(NSA_KV_HEADS, S // tq),
        in_specs=[pl.BlockSpec(memory_space=pltpu.SMEM),
                  pl.BlockSpec((tq, gw), lambda g, i: (i, Z_NQ // gw + g)),
                  pl.BlockSpec((S, NSA_DH), lambda g, i: (0, kcol + g)),
                  pl.BlockSpec((S, NSA_DH), lambda g, i: (0, kcol + 2 + g)),
                  pl.BlockSpec((None, tq, n_slc), lambda g, i: (g, i, 0)),
                  pl.BlockSpec((1, tq), lambda g, i: (0, i)),
                  pl.BlockSpec((S, LANE), lambda g, i: (0, 0))],
        out_specs=pl.BlockSpec((tq, gw), lambda g, i: (i, g)),
        scratch_shapes=[pltpu.VMEM((NSA_DH, S), BF16)] + _score_bufs(tk, NSA_REP * tq),
        compiler_params=_cparams(("arbitrary", "arbitrary")),
        name="nsa_slc_attn",
    )(scal, z, z, z, sel, pos_row, pos_b)


def _win_attn_kernel(sc_ref, q_ref, k_ref, v_ref, pq_ref, pk_ref, ocmp_ref, oslc_ref, gate_ref,
                     o_ref, *, t):
    g = pl.program_id(0)
    i = pl.program_id(1)
    q = q_ref[...] * jnp.asarray(NSA_DH ** -0.5, BF16)
    pq = pq_ref[...]
    slopes = [sc_ref[g * NSA_REP + r] for r in range(NSA_REP)]
    dif0 = (lax.broadcasted_iota(jnp.int32, (t, t), 0) - lax.broadcasted_iota(jnp.int32, (t, t), 1))

    def tile(jj, carry):
        j = i - jj
        off = pl.multiple_of(j * t, t)
        k = k_ref[pl.ds(off, t), :]
        v = v_ref[pl.ds(off, t), :]
        dif = dif0 + jj * t
        ok = (dif >= 0) & (dif < WINDOW)
        dist = jnp.abs(pq - pk_ref[:, pl.ds(off, t)])
        out = []
        for r in range(NSA_REP):
            s = _qk(q[:, r * NSA_DH:(r + 1) * NSA_DH], k) - slopes[r] * dist
            out.append(_online(jnp.where(ok, s, NEG), v, *carry[r], valid=ok))
        return tuple(out)

    n_back = (WINDOW + t - 1) // t
    carry = lax.fori_loop(0, jnp.minimum(i, n_back) + 1, tile,
                          tuple(_init(t, NSA_DH) for _ in range(NSA_REP)))
    gates = jax.nn.sigmoid(gate_ref[...].astype(F32))
    for r in range(NSA_REP):
        _, l, acc = carry[r]
        sl = slice(r * NSA_DH, (r + 1) * NSA_DH)

        def gcol(b):
            c0, c1 = 3 * r + b, 3 * (NSA_REP + r) + b
            return jnp.where(g == 0, gates[:, c0:c0 + 1], gates[:, c1:c1 + 1])

        o = (gcol(0) * ocmp_ref[:, sl].astype(F32) + gcol(1) * oslc_ref[:, sl].astype(F32)
             + gcol(2) * (acc / l))
        o_ref[:, sl] = o.astype(o_ref.dtype)


def _win_attn(z, o_cmp, o_slc, pos_col, pos_row, scal):
    S = z.shape[0]
    t = 256
    gw = NSA_REP * NSA_DH
    kcol = Z_NKV // LANE + 8
    return pl.pallas_call(
        functools.partial(_win_attn_kernel, t=t),
        out_shape=jax.ShapeDtypeStruct((S, NSA_HEADS * NSA_DH), BF16),
        grid=(NSA_KV_HEADS, S // t),
        in_specs=[pl.BlockSpec(memory_space=pltpu.SMEM),
                  pl.BlockSpec((t, gw), lambda g, i: (i, Z_NQ // gw + g)),
                  pl.BlockSpec((S, NSA_DH), lambda g, i: (0, kcol + g)),
                  pl.BlockSpec((S, NSA_DH), lambda g, i: (0, kcol + 2 + g)),
                  pl.BlockSpec((t, 1), lambda g, i: (i, 0)),
                  pl.BlockSpec((1, S), lambda g, i: (0, 0)),
                  pl.BlockSpec((t, gw), lambda g, i: (i, g)),
                  pl.BlockSpec((t, gw), lambda g, i: (i, g)),
                  pl.BlockSpec((t, LANE), lambda g, i: (i, Z_NG // LANE))],
        out_specs=pl.BlockSpec((t, gw), lambda g, i: (i, g)),
        compiler_params=_cparams(("arbitrary", "arbitrary")),
        name="nsa_win_attn",
    )(scal, z, z, z, pos_col, pos_row, o_cmp, o_slc, z)


def kernel(x, c, positions, w_ada, b_ada, w_in, mla_q_norm, mla_kv_norm, mla_w_uq, mla_w_ukv, diff_lambda, diff_subln, nsa_cmp_w1, nsa_cmp_b1, nsa_cmp_w2, w_branch, w_out, ln1_g, ln1_b, w_ff1, w_ff2, ln2_g, ln2_b):
    B, S, D = x.shape
    L = w_ada.shape[0]
    assert B == 1 and D == D_MODEL and S % 512 == 0 and S // SLC_BLOCK >= 3

    w_in_x = _relayout_w_in(w_in)
    w_uq_x = _relayout_w_uq(mla_w_uq)
    w_ukv_b = mla_w_ukv.astype(BF16)
    w_branch_b = w_branch.astype(BF16)
    w_out_b = w_out.astype(BF16)
    w_ff1_b = w_ff1.astype(BF16)
    w_ff2_b = w_ff2.astype(BF16)

    slopes = 2.0 ** (-8.0 * np.arange(1, N_ALIBI_HEADS + 1, dtype=np.float64) / N_ALIBI_HEADS)
    s_diff, s_nsa = slopes[0::2], slopes[1::2]
    nsa_scal = jnp.asarray(s_nsa, F32)
    nsa_scal2 = jnp.asarray(s_nsa * LOG2E, F32)

    pos_f = positions.astype(F32)
    pos_col = pos_f.reshape(S, 1)
    pos_row = pos_f.reshape(1, S)
    pos_b = jnp.broadcast_to(pos_col, (S, LANE))
    pos_cmp = jnp.pad(pos_f[0, CMP_BLOCK - 1::CMP_STRIDE], (0, 1)).reshape(1, S // CMP_STRIDE)

    mod = _ada(c, w_ada, b_ada)
    rope_t = _rope_table(pos_col)
    xs = x.reshape(S, D)

    for l in range(L):
        lam_init = 0.8 - 0.6 * math.exp(-0.3 * l)
        diff_scal = jnp.asarray(np.concatenate([s_diff * LOG2E, [lam_init]]), F32)

        h = _lnmod(xs, mod, l, 0, 1)
        z = _matmul(h, w_in_x, l, tm=1024, tn=1280, relu2=False, name="in_proj")

        qm, km, vm = _mla_proj(z, rope_t, mla_q_norm, mla_kv_norm, w_uq_x, w_ukv_b, l)
        o_mla = _mla_attn(qm, km, vm)

        o_diff = _diff_attn(z, pos_row, pos_b, diff_scal, diff_lambda, diff_subln, l)

        xc = z[:, Z_NKV:Z_NKV + 4 * NSA_DH].reshape(S // CMP_STRIDE, CMP_STRIDE, 4, NSA_DH)
        xc = xc.transpose(2, 0, 1, 3).reshape(4, S // CMP_STRIDE, CMP_STRIDE * NSA_DH)
        cmp_kv = _compress(xc, nsa_cmp_w1, nsa_cmp_b1, nsa_cmp_w2, l)
        o_cmp, sel = _cmp_sel(z, cmp_kv, pos_col, pos_cmp, nsa_scal)
        o_slc = _slc_attn(z, sel, pos_row, pos_b, nsa_scal2)
        o_nsa = _win_attn(z, o_cmp, o_slc, pos_col, pos_row, nsa_scal)

        merged = _merge(o_mla, o_diff, o_nsa, w_branch_b, z, l)
        xs = _matmul_res_ln(merged, w_out_b, xs, mod, ln1_g, ln1_b, l, 2, ALPHA, "out_proj_ln")

        h = _lnmod(xs, mod, l, 3, 4)
        u = _matmul(h, w_ff1_b, l, tm=1024, tn=1024, relu2=True, name="ff1")
        xs = _matmul_res_ln(u, w_ff2_b, xs, mod, ln2_g, ln2_b, l, 5, ALPHA, "ff2_ln")

    return xs.reshape(B, S, D)
```

```python
import functools
import math

import numpy as np
import jax
import jax.numpy as jnp
from jax import lax
from jax.experimental import pallas as pl
from jax.experimental.pallas import tpu as pltpu

F32 = jnp.float32
BF16 = jnp.bfloat16

D_MODEL = 2048
MLA_HEADS = 8
MLA_Q_LORA = 512
MLA_KV_LORA = 256
MLA_NOPE = 128
MLA_ROPE = 64
MLA_V = 128
ROPE_THETA = 10000.0
DIFF_HEADS = 8
DIFF_QK = 64
DIFF_V = 128
NSA_HEADS = 8
NSA_KV_HEADS = 2
NSA_REP = NSA_HEADS // NSA_KV_HEADS
NSA_DH = 128
CMP_BLOCK = 32
CMP_STRIDE = 16
SLC_BLOCK = 64
N_SELECT = 16
WINDOW = 512
N_ALIBI_HEADS = DIFF_HEADS + NSA_HEADS
N_BRANCH = 3
BRANCH_WIDTH = 1024
D_FF = 4 * D_MODEL
N_MOD = 6
DEPTH = 4
ALPHA = (2.0 * DEPTH) ** 0.25

LANE = 128
NEG = -1e30
LOG2E = math.log2(math.e)
T_CHUNK = 512
N_SPLIT = 3
VMEM_LIMIT = 48 * 1024 * 1024

Z_CQ = 0
Z_DQ = 512
Z_DK = 1536
Z_DV = 2560
Z_NQ = 3584
Z_NKV = 4608
Z_MG = 6144
Z_CKV = 12288
Z_KR = 12544
Z_NG = 12672
Z_W = 12800


def _cparams(sem):
    return pltpu.CompilerParams(dimension_semantics=sem, vmem_limit_bytes=VMEM_LIMIT)


def _rot_half(w):
    half = MLA_ROPE // 2
    return jnp.concatenate([-w[..., half:], w[..., :half]], axis=-1)


def _relayout_w_in(w_in):
    o = np.cumsum([0, 512, 256, 64, 1024, 1024, 1024, 1024, 1536, 24, N_BRANCH * D_MODEL])
    cq, ckv, kr, dq, dk, dv, nq, nkv, ng, mg = (w_in[..., o[k]:o[k + 1]] for k in range(10))
    pad = jnp.zeros(w_in.shape[:-1] + (Z_W - Z_NG - ng.shape[-1],), w_in.dtype)
    return jnp.concatenate([cq, dq, dk, dv, nq, nkv, mg, ckv, kr, _rot_half(kr), ng, pad],
                           axis=-1).astype(BF16)


def _relayout_w_uq(w_uq):
    L, K, _ = w_uq.shape
    w = w_uq.reshape(L, K, MLA_HEADS, MLA_NOPE + MLA_ROPE)
    rope = w[..., MLA_NOPE:]
    return jnp.concatenate([w, _rot_half(rope)], axis=-1).reshape(L, K, MLA_HEADS * 256).astype(BF16)


def _ada_kernel(c_ref, w_ref, b_ref, o_ref):
    c = c_ref[...]
    ca = c * jax.nn.sigmoid(c)
    ca8 = jnp.broadcast_to(ca, (8, c.shape[1])).astype(BF16)
    r = jnp.dot(ca8, w_ref[...].astype(BF16), preferred_element_type=F32)
    o_ref[...] = r[0:1, :] + b_ref[...]


def _ada(c, w_ada, b_ada):
    L, D, N = w_ada.shape
    tn = 1536
    return pl.pallas_call(
        _ada_kernel,
        out_shape=jax.ShapeDtypeStruct((L, 1, N), F32),
        grid=(L, N // tn),
        in_specs=[pl.BlockSpec((1, D), lambda l, j: (0, 0)),
                  pl.BlockSpec((None, D, tn), lambda l, j: (l, 0, j)),
                  pl.BlockSpec((None, 1, tn), lambda l, j: (l, 0, j))],
        out_specs=pl.BlockSpec((None, 1, tn), lambda l, j: (l, 0, j)),
        compiler_params=_cparams(("arbitrary", "arbitrary")),
        name="ada_mod",
    )(c, w_ada, b_ada.reshape(L, 1, N))


def _lnmod_kernel(x_ref, sh_ref, sc_ref, o_ref):
    x = x_ref[...]
    mu = jnp.mean(x, -1, keepdims=True)
    xc = x - mu
    var = jnp.mean(xc * xc, -1, keepdims=True)
    y = xc * lax.rsqrt(var + 1e-5)
    o_ref[...] = (y * (1.0 + sc_ref[...]) + sh_ref[...]).astype(o_ref.dtype)


def _lnmod(x, mod, l, k_shift, k_scale):
    S, D = x.shape
    tm = 512
    return pl.pallas_call(
        _lnmod_kernel,
        out_shape=jax.ShapeDtypeStruct((S, D), BF16),
        grid=(S // tm,),
        in_specs=[pl.BlockSpec((tm, D), lambda i: (i, 0)),
                  pl.BlockSpec((None, 1, D), lambda i: (l, 0, k_shift)),
                  pl.BlockSpec((None, 1, D), lambda i: (l, 0, k_scale))],
        out_specs=pl.BlockSpec((tm, D), lambda i: (i, 0)),
        compiler_params=_cparams(("arbitrary",)),
        name="ln_modulate",
    )(x, mod, mod)


def _mm_kernel(x_ref, w_ref, o_ref, *, relu2):
    r = jnp.dot(x_ref[...], w_ref[...], preferred_element_type=F32)
    if relu2:
        r = jnp.square(jnp.maximum(r, 0.0))
    o_ref[...] = r.astype(o_ref.dtype)


def _matmul(x, w, l, *, tm, tn, relu2, name):
    S, K = x.shape
    N = w.shape[-1]
    return pl.pallas_call(
        functools.partial(_mm_kernel, relu2=relu2),
        out_shape=jax.ShapeDtypeStruct((S, N), BF16),
        grid=(S // tm, N // tn),
        in_specs=[pl.BlockSpec((tm, K), lambda i, j: (i, 0)),
                  pl.BlockSpec((None, K, tn), lambda i, j: (l, 0, j))],
        out_specs=pl.BlockSpec((tm, tn), lambda i, j: (i, j)),
        compiler_params=_cparams(("arbitrary", "arbitrary")),
        name=name,
    )(x, w)


def _mm_res_ln_kernel(a_ref, w_ref, x_ref, gate_ref, lng_ref, lnb_ref, o_ref, *, nk, alpha):
    k = pl.program_id(1)
    part = jnp.dot(a_ref[...], w_ref[...], preferred_element_type=F32)

    @pl.when(k == 0)
    def _():
        o_ref[...] = part

    @pl.when(k > 0)
    def _():
        o_ref[...] += part

    @pl.when(k == nk - 1)
    def _():
        y = alpha * x_ref[...] + gate_ref[...] * o_ref[...]
        mu = jnp.mean(y, -1, keepdims=True)
        yc = y - mu
        var = jnp.mean(yc * yc, -1, keepdims=True)
        o_ref[...] = yc * lax.rsqrt(var + 1e-5) * lng_ref[...] + lnb_ref[...]


def _matmul_res_ln(a, w, x, mod, lng, lnb, l, k_gate, alpha, name):
    S, K = a.shape
    D = x.shape[1]
    L = lng.shape[0]
    tm, tk = 512, 2048
    nk = K // tk
    return pl.pallas_call(
        functools.partial(_mm_res_ln_kernel, nk=nk, alpha=alpha),
        out_shape=jax.ShapeDtypeStruct((S, D), F32),
        grid=(S // tm, nk),
        in_specs=[pl.BlockSpec((tm, tk), lambda i, k: (i, k)),
                  pl.BlockSpec((None, tk, D), lambda i, k: (l, k, 0)),
                  pl.BlockSpec((tm, D), lambda i, k: (i, 0)),
                  pl.BlockSpec((None, 1, D), lambda i, k: (l, 0, k_gate)),
                  pl.BlockSpec((None, 1, D), lambda i, k: (l, 0, 0)),
                  pl.BlockSpec((None, 1, D), lambda i, k: (l, 0, 0))],
        out_specs=pl.BlockSpec((tm, D), lambda i, k: (i, 0)),
        compiler_params=_cparams(("arbitrary", "arbitrary")),
        name=name,
    )(a, w, x, mod, lng.reshape(L, 1, D), lnb.reshape(L, 1, D))


def _merge_kernel(o0_ref, o1_ref, o2_ref, w_ref, g0_ref, g1_ref, g2_ref, out_ref):
    acc = None
    for n, (o_ref, g_ref) in enumerate(((o0_ref, g0_ref), (o1_ref, g1_ref), (o2_ref, g2_ref))):
        t = jax.nn.sigmoid(g_ref[...].astype(F32)) * jnp.dot(o_ref[...], w_ref[n],
                                                               preferred_element_type=F32)
        acc = t if acc is None else acc + t
    out_ref[...] = acc.astype(out_ref.dtype)


def _merge(o_mla, o_diff, o_nsa, w_branch, z, l):
    S = o_mla.shape[0]
    tm, tn = 1024, 512
    bspec = pl.BlockSpec((tm, BRANCH_WIDTH), lambda i, j: (i, 0))

    def gspec(n):
        return pl.BlockSpec((tm, tn), lambda i, j: (i, (Z_MG + D_MODEL * n) // tn + j))

    return pl.pallas_call(
        _merge_kernel,
        out_shape=jax.ShapeDtypeStruct((S, D_MODEL), BF16),
        grid=(S // tm, D_MODEL // tn),
        in_specs=[bspec, bspec, bspec,
                  pl.BlockSpec((None, N_BRANCH, BRANCH_WIDTH, tn), lambda i, j: (l, 0, 0, j)),
                  gspec(0), gspec(1), gspec(2)],
        out_specs=pl.BlockSpec((tm, tn), lambda i, j: (i, j)),
        compiler_params=_cparams(("arbitrary", "arbitrary")),
        name="branch_merge",
    )(o_mla, o_diff, o_nsa, w_branch, z, z, z)


def _rope_kernel(pos_ref, inv_ref, o_ref):
    ang = pos_ref[...] * inv_ref[...]
    lane = lax.broadcasted_iota(jnp.int32, ang.shape, 1)
    o_ref[...] = jnp.where(lane < MLA_ROPE, jnp.cos(ang), jnp.sin(ang))


def _rope_table(pos_col):
    S = pos_col.shape[0]
    inv = ROPE_THETA ** (-np.arange(0, MLA_ROPE, 2, dtype=np.float64) / MLA_ROPE)
    inv4 = jnp.asarray(np.tile(inv, 4)[None, :], F32)
    tm = 512
    return pl.pallas_call(
        _rope_kernel,
        out_shape=jax.ShapeDtypeStruct((S, LANE), F32),
        grid=(S // tm,),
        in_specs=[pl.BlockSpec((tm, 1), lambda i: (i, 0)),
                  pl.BlockSpec((1, LANE), lambda i: (0, 0))],
        out_specs=pl.BlockSpec((tm, LANE), lambda i: (i, 0)),
        compiler_params=_cparams(("arbitrary",)),
        name="rope_table",
    )(pos_col, inv4)


def _rms(x, g):
    return x * lax.rsqrt(jnp.mean(x * x, -1, keepdims=True) + 1e-6) * g


def _mla_proj_kernel(cq_ref, ckv_ref, kr_ref, t_ref, gq_ref, gkv_ref, wuq_ref, wukv_ref,
                     q_out, k_out, v_out):
    scale = LOG2E * (MLA_NOPE + MLA_ROPE) ** -0.5
    t = t_ref[...]
    nq = _rms(cq_ref[...].astype(F32), gq_ref[...]).astype(BF16)
    q = jnp.dot(nq, wuq_ref[...], preferred_element_type=F32)
    for h in range(MLA_HEADS):
        c = 256 * h
        q_out[:, c:c + 128] = (q[:, c:c + 128] * scale).astype(BF16)
        q_out[:, c + 128:c + 256] = (q[:, c + 128:c + 256] * t * scale).astype(BF16)
    nkv = _rms(ckv_ref[...].astype(F32), gkv_ref[...]).astype(BF16)
    kv = jnp.dot(nkv, wukv_ref[...], preferred_element_type=F32)
    kr = kr_ref[...].astype(F32) * t
    kr2 = (kr + pltpu.roll(kr, MLA_ROPE, 1)).astype(BF16)
    for h in range(MLA_HEADS):
        c = 256 * h
        k_out[:, c:c + 128] = kv[:, c:c + 128].astype(BF16)
        k_out[:, c + 128:c + 256] = kr2
        v_out[:, 128 * h:128 * h + 128] = kv[:, c + 128:c + 256].astype(BF16)


def _mla_proj(z, rope_t, gq, gkv, wuq, wukv, l):
    S = z.shape[0]
    L = gq.shape[0]
    tm = 512
    HW = MLA_HEADS * 256
    return pl.pallas_call(
        _mla_proj_kernel,
        out_shape=(jax.ShapeDtypeStruct((S, HW), BF16),
                   jax.ShapeDtypeStruct((S, HW), BF16),
                   jax.ShapeDtypeStruct((S, MLA_HEADS * MLA_V), BF16)),
        grid=(S // tm,),
        in_specs=[pl.BlockSpec((tm, MLA_Q_LORA), lambda i: (i, Z_CQ // MLA_Q_LORA)),
                  pl.BlockSpec((tm, MLA_KV_LORA), lambda i: (i, Z_CKV // MLA_KV_LORA)),
                  pl.BlockSpec((tm, LANE), lambda i: (i, Z_KR // LANE)),
                  pl.BlockSpec((tm, LANE), lambda i: (i, 0)),
                  pl.BlockSpec((None, 1, MLA_Q_LORA), lambda i: (l, 0, 0)),
                  pl.BlockSpec((None, 1, MLA_KV_LORA), lambda i: (l, 0, 0)),
                  pl.BlockSpec((None, MLA_Q_LORA, HW), lambda i: (l, 0, 0)),
                  pl.BlockSpec((None, MLA_KV_LORA, HW), lambda i: (l, 0, 0))],
        out_specs=(pl.BlockSpec((tm, HW), lambda i: (i, 0)),
                   pl.BlockSpec((tm, HW), lambda i: (i, 0)),
                   pl.BlockSpec((tm, MLA_HEADS * MLA_V), lambda i: (i, 0))),
        compiler_params=_cparams(("arbitrary",)),
        name="mla_proj",
    )(z, z, z, rope_t, gq.reshape(L, 1, -1), gkv.reshape(L, 1, -1), wuq, wukv)


def _qk(q, k):
    return lax.dot_general(q, k, (((1,), (1,)), ((), ())), preferred_element_type=F32)


def _online(s, v, m, l, acc, valid=None):
    m_new = jnp.maximum(m, jnp.max(s, -1, keepdims=True))
    a = jnp.exp(m - m_new)
    p = jnp.exp(s - m_new)
    if valid is not None:
        p = jnp.where(valid, p, 0.0)
    l = a * l + jnp.sum(p, -1, keepdims=True)
    acc = a * acc + jnp.dot(p.astype(BF16), v, preferred_element_type=F32)
    return m_new, l, acc


def _init(rows, dv):
    return (jnp.full((rows, 1), NEG, F32), jnp.zeros((rows, 1), F32), jnp.zeros((rows, dv), F32))


def _kq(k, q):
    return lax.dot_general(k, q, (((1,), (1,)), ((), ())), preferred_element_type=F32)


def _online_t(s, vt, m, l, acc):
    m_new = jnp.maximum(m, jnp.max(s, 0, keepdims=True))
    a = jnp.exp2(m - m_new)
    p = jnp.exp2(s - m_new)
    l = a * l + jnp.sum(p, 0, keepdims=True)
    acc = a * acc + jnp.dot(vt, p.astype(BF16), preferred_element_type=F32)
    return m_new, l, acc


def _init_t(nq, dv):
    return (jnp.full((1, nq), NEG, F32), jnp.zeros((1, nq), F32), jnp.zeros((dv, nq), F32))


def _transpose_into(vt_ref, v_ref):
    for c in range(v_ref.shape[0] // T_CHUNK):
        rows = slice(c * T_CHUNK, (c + 1) * T_CHUNK)
        vt_ref[:, rows] = v_ref[rows, :].astype(F32).T.astype(vt_ref.dtype)


def _put_scores(buf, t, lanes=None):
    s_ref, m_ref = buf
    if lanes is None:
        s_ref[...] = t
        m_ref[...] = jnp.max(t, 0, keepdims=True)
    else:
        s_ref[:, lanes] = t
        m_ref[:, lanes] = jnp.max(t, 0, keepdims=True)


def _sweep(n_pairs, tk, score, vt_ref, carry, buf_a, buf_b):
    def update(j, buf, carry):
        s_ref, mt_ref = buf
        m, l, acc = carry
        m_new = jnp.maximum(m, mt_ref[...])
        a = jnp.exp2(m - m_new)
        p = jnp.exp2(s_ref[...] - m_new)
        l = a * l + jnp.sum(p, 0, keepdims=True)
        vt = vt_ref[:, pl.ds(pl.multiple_of(j * tk, tk), tk)]
        acc = a * acc + jnp.dot(vt, p.astype(BF16), preferred_element_type=F32)
        return m_new, l, acc

    jd = 2 * n_pairs
    score(jd, buf_a, True)
    score(jd + 1, buf_b, True)
    carry = update(jd, buf_a, carry)

    def pair(p, carry):
        j = 2 * p
        score(j, buf_a, False)
        carry = update(jnp.where(p == 0, jd + 1, j - 1), buf_b, carry)
        score(j + 1, buf_b, False)
        return update(j, buf_a, carry)

    carry = lax.fori_loop(0, n_pairs, pair, carry)
    return update(jd - 1 + 2 * (n_pairs == 0).astype(jnp.int32), buf_b, carry)


def _score_bufs(tk, nq):
    return [pltpu.VMEM((tk, nq), F32), pltpu.VMEM((1, nq), F32),
            pltpu.VMEM((tk, nq), F32), pltpu.VMEM((1, nq), F32)]


def _mla_attn_kernel(q_ref, k_ref, v_ref, o_ref, vt_ref, sa_ref, ma_ref, sb_ref, mb_ref, *, tq, tk):
    i = pl.program_id(1)

    @pl.when(i == 0)
    def _():
        _transpose_into(vt_ref, v_ref)

    q = q_ref[...]

    def score(j, buf, masked):
        off = pl.multiple_of(j * tk, tk)
        t = _kq(k_ref[pl.ds(off, tk), :], q)
        if masked:
            kidx = off + lax.broadcasted_iota(jnp.int32, (tk, tq), 0)
            qidx = i * tq + lax.broadcasted_iota(jnp.int32, (tk, tq), 1)
            t = jnp.where(kidx <= qidx, t, NEG)
        _put_scores(buf, t)

    _, l, acc = _sweep(i, tk, score, vt_ref, _init_t(tq, MLA_V), (sa_ref, ma_ref), (sb_ref, mb_ref))
    o_ref[...] = (acc / l).T.astype(o_ref.dtype)


def _mla_attn(qm, km, vm):
    S = qm.shape[0]
    tq, tk = 1024, 512
    return pl.pallas_call(
        functools.partial(_mla_attn_kernel, tq=tq, tk=tk),
        out_shape=jax.ShapeDtypeStruct((S, MLA_HEADS * MLA_V), BF16),
        grid=(MLA_HEADS, S // tq),
        in_specs=[pl.BlockSpec((tq, 256), lambda h, i: (i, h)),
                  pl.BlockSpec((S, 256), lambda h, i: (0, h)),
                  pl.BlockSpec((S, MLA_V), lambda h, i: (0, h))],
        out_specs=pl.BlockSpec((tq, MLA_V), lambda h, i: (i, h)),
        scratch_shapes=[pltpu.VMEM((MLA_V, S), BF16)] + _score_bufs(tk, tq),
        compiler_params=_cparams(("arbitrary", "arbitrary")),
        name="mla_attn",
    )(qm, km, vm)


def _alibi_key_features(kf_ref, pkb_ref, coef):
    for c in range(pkb_ref.shape[0] // T_CHUNK):
        rows = slice(c * T_CHUNK, (c + 1) * T_CHUNK)
        hi, mid, lo = _split3(coef * pkb_ref[rows, :])
        lane = lax.broadcasted_iota(jnp.int32, hi.shape, 1)
        zero = jnp.zeros_like(hi)
        kf_ref[rows, :] = jnp.where(lane == 0, hi, jnp.where(lane == 1, mid, jnp.where(lane == 2, lo, zero)))


def _diff_attn_kernel(sc_ref, q_ref, k_ref, v_ref, pkb_ref, lam_ref, g_ref, o_ref,
                      vt_ref, kf_ref, sa_ref, ma_ref, sb_ref, mb_ref, *, tq, tk):
    h = pl.program_id(0)
    i = pl.program_id(1)

    @pl.when(i == 0)
    def _():
        _transpose_into(vt_ref, v_ref)
        _alibi_key_features(kf_ref, pkb_ref, sc_ref[h])

    lam_init = sc_ref[DIFF_HEADS]
    qs = (q_ref[...].astype(F32) * (LOG2E * DIFF_QK ** -0.5)).astype(BF16)
    lane = lax.broadcasted_iota(jnp.int32, qs.shape, 1)
    zero = jnp.zeros_like(qs)
    ones = jnp.where(lane < N_SPLIT, 1.0, 0.0).astype(BF16)
    q2 = jnp.concatenate([jnp.concatenate([jnp.where(lane < DIFF_QK, qs, zero), ones], 1),
                          jnp.concatenate([jnp.where(lane >= DIFF_QK, qs, zero), ones], 1)], 0)

    def score(j, buf, masked):
        off = pl.multiple_of(j * tk, tk)
        kx = jnp.concatenate([k_ref[pl.ds(off, tk), :], kf_ref[pl.ds(off, tk), :]], 1)
        t = _kq(kx, q2)
        if masked:
            kidx = off + lax.broadcasted_iota(jnp.int32, (tk, tq), 0)
            qidx = i * tq + lax.broadcasted_iota(jnp.int32, (tk, tq), 1)
            ok = kidx <= qidx
            for c in range(2):
                lanes = slice(c * tq, (c + 1) * tq)
                _put_scores(buf, jnp.where(ok, t[:, lanes], NEG), lanes)
        else:
            _put_scores(buf, t)

    _, l, acc = _sweep(i, tk, score, vt_ref, _init_t(2 * tq, DIFF_V), (sa_ref, ma_ref), (sb_ref, mb_ref))
    on = acc / l
    lp = lam_ref[...]
    lam = (jnp.exp(jnp.sum(lp[0:1] * lp[1:2], -1, keepdims=True))
           - jnp.exp(jnp.sum(lp[2:3] * lp[3:4], -1, keepdims=True)) + lam_init)
    o = on[:, :tq] - lam * on[:, tq:]
    o = o * lax.rsqrt(jnp.mean(o * o, 0, keepdims=True) + 1e-6)
    o_ref[...] = (o.T * g_ref[...] * (1.0 - lam_init)).astype(o_ref.dtype)


def _diff_attn(z, pos_b, scal, lam_p, g_sub, l):
    S = z.shape[0]
    L = g_sub.shape[0]
    tq, tk = 1024, 512
    return pl.pallas_call(
        functools.partial(_diff_attn_kernel, tq=tq, tk=tk),
        out_shape=jax.ShapeDtypeStruct((S, DIFF_HEADS * DIFF_V), BF16),
        grid=(DIFF_HEADS, S // tq),
        in_specs=[pl.BlockSpec(memory_space=pltpu.SMEM),
                  pl.BlockSpec((tq, LANE), lambda h, i: (i, Z_DQ // LANE + h)),
                  pl.BlockSpec((S, LANE), lambda h, i: (0, Z_DK // LANE + h)),
                  pl.BlockSpec((S, LANE), lambda h, i: (0, Z_DV // LANE + h)),
                  pl.BlockSpec((S, LANE), lambda h, i: (0, 0)),
                  pl.BlockSpec((None, 4, DIFF_QK), lambda h, i: (l, 0, 0)),
                  pl.BlockSpec((None, 1, DIFF_V), lambda h, i: (l, 0, 0))],
        out_specs=pl.BlockSpec((tq, DIFF_V), lambda h, i: (i, h)),
        scratch_shapes=[pltpu.VMEM((DIFF_V, S), BF16), pltpu.VMEM((S, LANE), BF16)]
        + _score_bufs(tk, 2 * tq),
        compiler_params=_cparams(("arbitrary", "arbitrary")),
        name="diff_attn",
    )(scal, z, z, z, pos_b, lam_p, g_sub.reshape(L, 1, -1))


def _compress_kernel(x_ref, w1_ref, b1_ref, w2_ref, o_ref):
    x = x_ref[...]
    half = CMP_STRIDE * NSA_DH
    a = jnp.dot(x, w1_ref[0:half, :].astype(BF16), preferred_element_type=F32)
    b = jnp.dot(x, w1_ref[half:2 * half, :].astype(BF16), preferred_element_type=F32)
    n = x.shape[0]
    pre = a + pltpu.roll(b, n - 1, 0) + b1_ref[...]
    hdn = 0.5 * pre * (1.0 + jnp.tanh(0.7978845608028654 * (pre + 0.044715 * pre * pre * pre)))
    o_ref[...] = jnp.dot(hdn.astype(BF16), w2_ref[...].astype(BF16),
                         preferred_element_type=F32).astype(o_ref.dtype)


def _compress(xc, w1, b1, w2, l):
    _, nb, kk = xc.shape
    L = w1.shape[0]
    return pl.pallas_call(
        _compress_kernel,
        out_shape=jax.ShapeDtypeStruct((4, nb, NSA_DH), BF16),
        grid=(4,),
        in_specs=[pl.BlockSpec((None, nb, kk), lambda c: (c, 0, 0)),
                  pl.BlockSpec((None, None, 2 * kk, NSA_DH), lambda c: (l, c // 2, 0, 0)),
                  pl.BlockSpec((None, None, 1, NSA_DH), lambda c: (l, c // 2, 0, 0)),
                  pl.BlockSpec((None, None, NSA_DH, NSA_DH), lambda c: (l, c // 2, 0, 0))],
        out_specs=pl.BlockSpec((None, nb, NSA_DH), lambda c: (c, 0, 0)),
        compiler_params=_cparams(("arbitrary",)),
        name="nsa_compress",
    )(xc, w1, b1.reshape(L, 2, 1, NSA_DH), w2)


def _split3(x):
    hi = x.astype(BF16)
    r = x - hi.astype(F32)
    mid = r.astype(BF16)
    lo = (r - mid.astype(F32)).astype(BF16)
    return hi, mid, lo


def _cmp_sel_kernel(sc_ref, q_ref, kc_ref, vc_ref, pq_ref, pc_ref, o_ref, sel_ref, *, tq, n_slc):
    g = pl.program_id(0)
    i = pl.program_id(1)
    nc = kc_ref.shape[0]
    q = q_ref[...] * jnp.asarray(NSA_DH ** -0.5, BF16)
    kc = kc_ref[...]
    vc = vc_ref[...]
    pq = pq_ref[...]
    qidx = i * tq + lax.broadcasted_iota(jnp.int32, (tq, nc), 0)
    cmp_end = lax.broadcasted_iota(jnp.int32, (tq, nc), 1) * CMP_STRIDE + (CMP_BLOCK - 1)
    vis = cmp_end <= qidx
    dist = jnp.abs(pq - pc_ref[...])
    p_sum = jnp.zeros((tq, nc), F32)
    for r in range(NSA_REP):
        slope = sc_ref[g * NSA_REP + r]
        s = _qk(q[:, r * NSA_DH:(r + 1) * NSA_DH], kc) - slope * dist
        s = jnp.where(vis, s, NEG)
        m = jnp.max(s, -1, keepdims=True)
        e = jnp.where(vis, jnp.exp(s - m), 0.0)
        d = jnp.sum(e, -1, keepdims=True)
        p = e / jnp.where(d > 0.0, d, 1.0)
        p_sum = p_sum + p
        o_ref[:, r * NSA_DH:(r + 1) * NSA_DH] = jnp.dot(
            p.astype(BF16), vc, preferred_element_type=F32).astype(o_ref.dtype)
    nn = lax.broadcasted_iota(jnp.int32, (nc, n_slc), 0)
    jj = lax.broadcasted_iota(jnp.int32, (nc, n_slc), 1)
    lo_n = jj * (SLC_BLOCK // CMP_STRIDE) - 1
    ovl = jnp.where((nn >= lo_n) & (nn <= lo_n + 4), 1.0, 0.0).astype(BF16)
    imp = sum(jnp.dot(part, ovl, preferred_element_type=F32) for part in _split3(p_sum))
    blk = lax.broadcasted_iota(jnp.int32, (tq, n_slc), 1).astype(F32)
    cur = ((i * tq + lax.broadcasted_iota(jnp.int32, (tq, n_slc), 0)) // SLC_BLOCK).astype(F32)
    forced = (blk == 0) | (blk == cur) | (blk == cur - 1)
    causal = blk <= cur
    score = jnp.where(forced, 3e38, jnp.where(causal, imp, -1.0))
    sel = jnp.zeros((tq, n_slc), F32)
    for _ in range(min(N_SELECT, n_slc)):
        mx = jnp.max(score, -1, keepdims=True)
        first = jnp.min(jnp.where(score == mx, blk, float(n_slc)), -1, keepdims=True)
        pick = blk == first
        sel = jnp.where(pick, 1.0, sel)
        score = jnp.where(pick, -2.0, score)
    sel_ref[...] = jnp.where(causal, sel, 0.0).astype(sel_ref.dtype)


def _cmp_sel(z, cmp_kv, pos_col, pos_cmp, scal):
    S = z.shape[0]
    nc = cmp_kv.shape[1]
    n_slc = S // SLC_BLOCK
    tq = 256
    gw = NSA_REP * NSA_DH
    return pl.pallas_call(
        functools.partial(_cmp_sel_kernel, tq=tq, n_slc=n_slc),
        out_shape=(jax.ShapeDtypeStruct((S, NSA_HEADS * NSA_DH), BF16),
                   jax.ShapeDtypeStruct((NSA_KV_HEADS, S, n_slc), BF16)),
        grid=(NSA_KV_HEADS, S // tq),
        in_specs=[pl.BlockSpec(memory_space=pltpu.SMEM),
                  pl.BlockSpec((tq, gw), lambda g, i: (i, Z_NQ // gw + g)),
                  pl.BlockSpec((None, nc, NSA_DH), lambda g, i: (g, 0, 0)),
                  pl.BlockSpec((None, nc, NSA_DH), lambda g, i: (2 + g, 0, 0)),
                  pl.BlockSpec((tq, 1), lambda g, i: (i, 0)),
                  pl.BlockSpec((1, nc), lambda g, i: (0, 0))],
        out_specs=(pl.BlockSpec((tq, gw), lambda g, i: (i, g)),
                   pl.BlockSpec((None, tq, n_slc), lambda g, i: (g, i, 0))),
        compiler_params=_cparams(("arbitrary", "arbitrary")),
        name="nsa_cmp_select",
    )(scal, z, cmp_kv, cmp_kv, pos_col, pos_cmp)


def _slc_attn_kernel(sc_ref, q_ref, k_ref, v_ref, sel_ref, pkb_ref, o_ref,
                     vt_ref, kf_ref, sa_ref, ma_ref, sb_ref, mb_ref, *, tq, tk):
    g = pl.program_id(0)
    i = pl.program_id(1)

    @pl.when(i == 0)
    def _():
        _transpose_into(vt_ref, v_ref)
        _alibi_key_features(kf_ref, pkb_ref, LOG2E)

    n_slc = sel_ref.shape[1]
    q = (q_ref[...].astype(F32) * (LOG2E * NSA_DH ** -0.5)).astype(BF16)
    lane = lax.broadcasted_iota(jnp.int32, (tq, NSA_DH), 1)
    q4 = jnp.concatenate(
        [jnp.concatenate([q[:, r * NSA_DH:(r + 1) * NSA_DH],
                          jnp.where(lane < N_SPLIT, sc_ref[g * NSA_REP + r], 0.0).astype(BF16)], 1)
         for r in range(NSA_REP)], 0)
    sel = sel_ref[...]
    blk = lax.broadcasted_iota(jnp.int32, (tk, n_slc), 1)
    kblk0 = lax.shift_right_logical(lax.broadcasted_iota(jnp.int32, (tk, n_slc), 0),
                                    int(math.log2(SLC_BLOCK)))

    def score(j, buf, masked):
        off = pl.multiple_of(j * tk, tk)
        kx = jnp.concatenate([k_ref[pl.ds(off, tk), :], kf_ref[pl.ds(off, tk), :]], 1)
        t = _kq(kx, q4)
        expand = jnp.where(blk == kblk0 + j * (tk // SLC_BLOCK), 1.0, 0.0).astype(BF16)
        ok = _kq(expand, sel) > 0.5
        if masked:
            kidx = off + lax.broadcasted_iota(jnp.int32, (tk, tq), 0)
            qidx = i * tq + lax.broadcasted_iota(jnp.int32, (tk, tq), 1)
            ok = ok & (kidx <= qidx)
        for r in range(NSA_REP):
            lanes = slice(r * tq, (r + 1) * tq)
            _put_scores(buf, jnp.where(ok, t[:, lanes], NEG), lanes)

    _, l, acc = _sweep(i, tk, score, vt_ref, _init_t(NSA_REP * tq, NSA_DH), (sa_ref, ma_ref), (sb_ref, mb_ref))
    on = acc / l
    for r in range(NSA_REP):
        o_ref[:, r * NSA_DH:(r + 1) * NSA_DH] = on[:, r * tq:(r + 1) * tq].T.astype(o_ref.dtype)


def _slc_attn(z, sel, pos_b, scal):
    S = z.shape[0]
    n_slc = S // SLC_BLOCK
    tq, tk = 512, 256
    gw = NSA_REP * NSA_DH
    kcol = Z_NKV // LANE + 4
    return pl.pallas_call(
        functools.partial(_slc_attn_kernel, tq=tq, tk=tk),
        out_shape=jax.ShapeDtypeStruct((S, NSA_HEADS * NSA_DH), BF16),
        grid=(NSA_KV_HEADS, S // tq),
        in_specs=[pl.BlockSpec(memory_space=pltpu.SMEM),
                  pl.BlockSpec((tq, gw), lambda g, i: (i, Z_NQ // gw + g)),
                  pl.BlockSpec((S, NSA_DH), lambda g, i: (0, kcol + g)),
                  pl.BlockSpec((S, NSA_DH), lambda g, i: (0, kcol + 2 + g)),
                  pl.BlockSpec((None, tq, n_slc), lambda g, i: (g, i, 0)),
                  pl.BlockSpec((S, LANE), lambda g, i: (0, 0))],
        out_specs=pl.BlockSpec((tq, gw), lambda g, i: (i, g)),
        scratch_shapes=[pltpu.VMEM((NSA_DH, S), BF16), pltpu.VMEM((S, LANE), BF16)]
        + _score_bufs(tk, NSA_REP * tq),
        compiler_params=_cparams(("arbitrary", "arbitrary")),
        name="nsa_slc_attn",
    )(scal, z, z, z, sel, pos_b)


def _win_attn_kernel(sc_ref, q_ref, k_ref, v_ref, pq_ref, pk_ref, ocmp_ref, oslc_ref, gate_ref,
                     o_ref, *, t):
    g = pl.program_id(0)
    i = pl.program_id(1)
    q = q_ref[...] * jnp.asarray(NSA_DH ** -0.5, BF16)
    pq = pq_ref[...]
    slopes = [sc_ref[g * NSA_REP + r] for r in range(NSA_REP)]
    dif0 = (lax.broadcasted_iota(jnp.int32, (t, t), 0) - lax.broadcasted_iota(jnp.int32, (t, t), 1))

    def tile(jj, carry):
        j = i - jj
        off = pl.multiple_of(j * t, t)
        k = k_ref[pl.ds(off, t), :]
        v = v_ref[pl.ds(off, t), :]
        dif = dif0 + jj * t
        ok = (dif >= 0) & (dif < WINDOW)
        dist = jnp.abs(pq - pk_ref[:, pl.ds(off, t)])
        out = []
        for r in range(NSA_REP):
            s = _qk(q[:, r * NSA_DH:(r + 1) * NSA_DH], k) - slopes[r] * dist
            out.append(_online(jnp.where(ok, s, NEG), v, *carry[r], valid=ok))
        return tuple(out)

    n_back = (WINDOW + t - 1) // t
    carry = lax.fori_loop(0, jnp.minimum(i, n_back) + 1, tile,
                          tuple(_init(t, NSA_DH) for _ in range(NSA_REP)))
    gates = jax.nn.sigmoid(gate_ref[...].astype(F32))
    for r in range(NSA_REP):
        _, l, acc = carry[r]
        sl = slice(r * NSA_DH, (r + 1) * NSA_DH)

        def gcol(b):
            c0, c1 = 3 * r + b, 3 * (NSA_REP + r) + b
            return jnp.where(g == 0, gates[:, c0:c0 + 1], gates[:, c1:c1 + 1])

        o = (gcol(0) * ocmp_ref[:, sl].astype(F32) + gcol(1) * oslc_ref[:, sl].astype(F32)
             + gcol(2) * (acc / l))
        o_ref[:, sl] = o.astype(o_ref.dtype)


def _win_attn(z, o_cmp, o_slc, pos_col, pos_row, scal):
    S = z.shape[0]
    t = 256
    gw = NSA_REP * NSA_DH
    kcol = Z_NKV // LANE + 8
    return pl.pallas_call(
        functools.partial(_win_attn_kernel, t=t),
        out_shape=jax.ShapeDtypeStruct((S, NSA_HEADS * NSA_DH), BF16),
        grid=(NSA_KV_HEADS, S // t),
        in_specs=[pl.BlockSpec(memory_space=pltpu.SMEM),
                  pl.BlockSpec((t, gw), lambda g, i: (i, Z_NQ // gw + g)),
                  pl.BlockSpec((S, NSA_DH), lambda g, i: (0, kcol + g)),
                  pl.BlockSpec((S, NSA_DH), lambda g, i: (0, kcol + 2 + g)),
                  pl.BlockSpec((t, 1), lambda g, i: (i, 0)),
                  pl.BlockSpec((1, S), lambda g, i: (0, 0)),
                  pl.BlockSpec((t, gw), lambda g, i: (i, g)),
                  pl.BlockSpec((t, gw), lambda g, i: (i, g)),
                  pl.BlockSpec((t, LANE), lambda g, i: (i, Z_NG // LANE))],
        out_specs=pl.BlockSpec((t, gw), lambda g, i: (i, g)),
        compiler_params=_cparams(("arbitrary", "arbitrary")),
        name="nsa_win_attn",
    )(scal, z, z, z, pos_col, pos_row, o_cmp, o_slc, z)


def kernel(x, c, positions, w_ada, b_ada, w_in, mla_q_norm, mla_kv_norm, mla_w_uq, mla_w_ukv, diff_lambda, diff_subln, nsa_cmp_w1, nsa_cmp_b1, nsa_cmp_w2, w_branch, w_out, ln1_g, ln1_b, w_ff1, w_ff2, ln2_g, ln2_b):
    B, S, D = x.shape
    L = w_ada.shape[0]
    assert B == 1 and D == D_MODEL and S % 512 == 0 and S // SLC_BLOCK >= 3

    w_in_x = _relayout_w_in(w_in)
    w_uq_x = _relayout_w_uq(mla_w_uq)
    w_ukv_b = mla_w_ukv.astype(BF16)
    w_branch_b = w_branch.astype(BF16)
    w_out_b = w_out.astype(BF16)
    w_ff1_b = w_ff1.astype(BF16)
    w_ff2_b = w_ff2.astype(BF16)

    slopes = 2.0 ** (-8.0 * np.arange(1, N_ALIBI_HEADS + 1, dtype=np.float64) / N_ALIBI_HEADS)
    s_diff, s_nsa = slopes[0::2], slopes[1::2]
    nsa_scal = jnp.asarray(s_nsa, F32)

    pos_f = positions.astype(F32)
    pos_col = pos_f.reshape(S, 1)
    pos_row = pos_f.reshape(1, S)
    pos_b = jnp.broadcast_to(pos_col, (S, LANE))
    pos_cmp = jnp.pad(pos_f[0, CMP_BLOCK - 1::CMP_STRIDE], (0, 1)).reshape(1, S // CMP_STRIDE)

    mod = _ada(c, w_ada, b_ada)
    rope_t = _rope_table(pos_col)
    xs = x.reshape(S, D)

    for l in range(L):
        lam_init = 0.8 - 0.6 * math.exp(-0.3 * l)
        diff_scal = jnp.asarray(np.concatenate([s_diff * LOG2E, [lam_init]]), F32)

        h = _lnmod(xs, mod, l, 0, 1)
        z = _matmul(h, w_in_x, l, tm=1024, tn=1280, relu2=False, name="in_proj")

        qm, km, vm = _mla_proj(z, rope_t, mla_q_norm, mla_kv_norm, w_uq_x, w_ukv_b, l)
        o_mla = _mla_attn(qm, km, vm)

        o_diff = _diff_attn(z, pos_b, diff_scal, diff_lambda, diff_subln, l)

        xc = z[:, Z_NKV:Z_NKV + 4 * NSA_DH].reshape(S // CMP_STRIDE, CMP_STRIDE, 4, NSA_DH)
        xc = xc.transpose(2, 0, 1, 3).reshape(4, S // CMP_STRIDE, CMP_STRIDE * NSA_DH)
        cmp_kv = _compress(xc, nsa_cmp_w1, nsa_cmp_b1, nsa_cmp_w2, l)
        o_cmp, sel = _cmp_sel(z, cmp_kv, pos_col, pos_cmp, nsa_scal)
        o_slc = _slc_attn(z, sel, pos_b, nsa_scal)
        o_nsa = _win_attn(z, o_cmp, o_slc, pos_col, pos_row, nsa_scal)

        merged = _merge(o_mla, o_diff, o_nsa, w_branch_b, z, l)
        xs = _matmul_res_ln(merged, w_out_b, xs, mod, ln1_g, ln1_b, l, 2, ALPHA, "out_proj_ln")

        h = _lnmod(xs, mod, l, 3, 4)
        u = _matmul(h, w_ff1_b, l, tm=1024, tn=1024, relu2=True, name="ff1")
        xs = _matmul_res_ln(u, w_ff2_b, xs, mod, ln2_g, ln2_b, l, 5, ALPHA, "ff2_ln")

    return xs.reshape(B, S, D)
```

```python
import functools
import math

import numpy as np
import jax
import jax.numpy as jnp
from jax import lax
from jax.experimental import pallas as pl
from jax.experimental.pallas import tpu as pltpu

F32 = jnp.float32
BF16 = jnp.bfloat16

D_MODEL = 2048
MLA_HEADS = 8
MLA_Q_LORA = 512
MLA_KV_LORA = 256
MLA_NOPE = 128
MLA_ROPE = 64
MLA_V = 128
ROPE_THETA = 10000.0
DIFF_HEADS = 8
DIFF_QK = 64
DIFF_V = 128
NSA_HEADS = 8
NSA_KV_HEADS = 2
NSA_REP = NSA_HEADS // NSA_KV_HEADS
NSA_DH = 128
CMP_BLOCK = 32
CMP_STRIDE = 16
SLC_BLOCK = 64
N_SELECT = 16
WINDOW = 512
N_ALIBI_HEADS = DIFF_HEADS + NSA_HEADS
N_BRANCH = 3
BRANCH_WIDTH = 1024
D_FF = 4 * D_MODEL
N_MOD = 6
DEPTH = 4
ALPHA = (2.0 * DEPTH) ** 0.25

LANE = 128
NEG = -1e30
LOG2E = math.log2(math.e)
T_CHUNK = 512
N_SPLIT = 3
PICKED = -2.0
VMEM_LIMIT = 48 * 1024 * 1024

Z_CQ = 0
Z_DQ = 512
Z_DK = 1536
Z_DV = 2560
Z_NQ = 3584
Z_NKV = 4608
Z_MG = 6144
Z_CKV = 12288
Z_KR = 12544
Z_NG = 12672
Z_W = 12800


def _cparams(sem):
    return pltpu.CompilerParams(dimension_semantics=sem, vmem_limit_bytes=VMEM_LIMIT)


def _rot_half(w):
    half = MLA_ROPE // 2
    return jnp.concatenate([-w[..., half:], w[..., :half]], axis=-1)


def _relayout_w_in(w_in):
    o = np.cumsum([0, 512, 256, 64, 1024, 1024, 1024, 1024, 1536, 24, N_BRANCH * D_MODEL])
    cq, ckv, kr, dq, dk, dv, nq, nkv, ng, mg = (w_in[..., o[k]:o[k + 1]] for k in range(10))
    pad = jnp.zeros(w_in.shape[:-1] + (Z_W - Z_NG - ng.shape[-1],), w_in.dtype)
    return jnp.concatenate([cq, dq, dk, dv, nq, nkv, mg, ckv, kr, _rot_half(kr), ng, pad],
                           axis=-1).astype(BF16)


def _relayout_w_uq(w_uq):
    L, K, _ = w_uq.shape
    w = w_uq.reshape(L, K, MLA_HEADS, MLA_NOPE + MLA_ROPE)
    rope = w[..., MLA_NOPE:]
    return jnp.concatenate([w, _rot_half(rope)], axis=-1).reshape(L, K, MLA_HEADS * 256).astype(BF16)


def _ada_kernel(c_ref, w_ref, b_ref, o_ref):
    c = c_ref[...]
    ca = c * jax.nn.sigmoid(c)
    ca8 = jnp.broadcast_to(ca, (8, c.shape[1])).astype(BF16)
    r = jnp.dot(ca8, w_ref[...].astype(BF16), preferred_element_type=F32)
    o_ref[...] = r[0:1, :] + b_ref[...]


def _ada(c, w_ada, b_ada):
    L, D, N = w_ada.shape
    tn = 1536
    return pl.pallas_call(
        _ada_kernel,
        out_shape=jax.ShapeDtypeStruct((L, 1, N), F32),
        grid=(L, N // tn),
        in_specs=[pl.BlockSpec((1, D), lambda l, j: (0, 0)),
                  pl.BlockSpec((None, D, tn), lambda l, j: (l, 0, j)),
                  pl.BlockSpec((None, 1, tn), lambda l, j: (l, 0, j))],
        out_specs=pl.BlockSpec((None, 1, tn), lambda l, j: (l, 0, j)),
        compiler_params=_cparams(("arbitrary", "arbitrary")),
        name="ada_mod",
    )(c, w_ada, b_ada.reshape(L, 1, N))


def _lnmod_kernel(x_ref, sh_ref, sc_ref, o_ref):
    x = x_ref[...]
    mu = jnp.mean(x, -1, keepdims=True)
    xc = x - mu
    var = jnp.mean(xc * xc, -1, keepdims=True)
    y = xc * lax.rsqrt(var + 1e-5)
    o_ref[...] = (y * (1.0 + sc_ref[...]) + sh_ref[...]).astype(o_ref.dtype)


def _lnmod(x, mod, l, k_shift, k_scale):
    S, D = x.shape
    tm = 512
    return pl.pallas_call(
        _lnmod_kernel,
        out_shape=jax.ShapeDtypeStruct((S, D), BF16),
        grid=(S // tm,),
        in_specs=[pl.BlockSpec((tm, D), lambda i: (i, 0)),
                  pl.BlockSpec((None, 1, D), lambda i: (l, 0, k_shift)),
                  pl.BlockSpec((None, 1, D), lambda i: (l, 0, k_scale))],
        out_specs=pl.BlockSpec((tm, D), lambda i: (i, 0)),
        compiler_params=_cparams(("arbitrary",)),
        name="ln_modulate",
    )(x, mod, mod)


def _mm_kernel(x_ref, w_ref, o_ref, *, relu2):
    r = jnp.dot(x_ref[...], w_ref[...], preferred_element_type=F32)
    if relu2:
        r = jnp.square(jnp.maximum(r, 0.0))
    o_ref[...] = r.astype(o_ref.dtype)


def _matmul(x, w, l, *, tm, tn, relu2, name):
    S, K = x.shape
    N = w.shape[-1]
    return pl.pallas_call(
        functools.partial(_mm_kernel, relu2=relu2),
        out_shape=jax.ShapeDtypeStruct((S, N), BF16),
        grid=(S // tm, N // tn),
        in_specs=[pl.BlockSpec((tm, K), lambda i, j: (i, 0)),
                  pl.BlockSpec((None, K, tn), lambda i, j: (l, 0, j))],
        out_specs=pl.BlockSpec((tm, tn), lambda i, j: (i, j)),
        compiler_params=_cparams(("arbitrary", "arbitrary")),
        name=name,
    )(x, w)


def _mm_res_ln_kernel(a_ref, w_ref, x_ref, gate_ref, lng_ref, lnb_ref, o_ref, *, nk, alpha):
    k = pl.program_id(1)
    part = jnp.dot(a_ref[...], w_ref[...], preferred_element_type=F32)

    @pl.when(k == 0)
    def _():
        o_ref[...] = part

    @pl.when(k > 0)
    def _():
        o_ref[...] += part

    @pl.when(k == nk - 1)
    def _():
        y = alpha * x_ref[...] + gate_ref[...] * o_ref[...]
        mu = jnp.mean(y, -1, keepdims=True)
        yc = y - mu
        var = jnp.mean(yc * yc, -1, keepdims=True)
        o_ref[...] = yc * lax.rsqrt(var + 1e-5) * lng_ref[...] + lnb_ref[...]


def _matmul_res_ln(a, w, x, mod, lng, lnb, l, k_gate, alpha, name):
    S, K = a.shape
    D = x.shape[1]
    L = lng.shape[0]
    tm, tk = 512, 2048
    nk = K // tk
    return pl.pallas_call(
        functools.partial(_mm_res_ln_kernel, nk=nk, alpha=alpha),
        out_shape=jax.ShapeDtypeStruct((S, D), F32),
        grid=(S // tm, nk),
        in_specs=[pl.BlockSpec((tm, tk), lambda i, k: (i, k)),
                  pl.BlockSpec((None, tk, D), lambda i, k: (l, k, 0)),
                  pl.BlockSpec((tm, D), lambda i, k: (i, 0)),
                  pl.BlockSpec((None, 1, D), lambda i, k: (l, 0, k_gate)),
                  pl.BlockSpec((None, 1, D), lambda i, k: (l, 0, 0)),
                  pl.BlockSpec((None, 1, D), lambda i, k: (l, 0, 0))],
        out_specs=pl.BlockSpec((tm, D), lambda i, k: (i, 0)),
        compiler_params=_cparams(("arbitrary", "arbitrary")),
        name=name,
    )(a, w, x, mod, lng.reshape(L, 1, D), lnb.reshape(L, 1, D))


def _merge_kernel(o0_ref, o1_ref, o2_ref, w_ref, g0_ref, g1_ref, g2_ref, out_ref):
    acc = None
    for n, (o_ref, g_ref) in enumerate(((o0_ref, g0_ref), (o1_ref, g1_ref), (o2_ref, g2_ref))):
        t = jax.nn.sigmoid(g_ref[...].astype(F32)) * jnp.dot(o_ref[...], w_ref[n],
                                                               preferred_element_type=F32)
        acc = t if acc is None else acc + t
    out_ref[...] = acc.astype(out_ref.dtype)


def _merge(o_mla, o_diff, o_nsa, w_branch, z, l):
    S = o_mla.shape[0]
    tm, tn = 1024, 512
    bspec = pl.BlockSpec((tm, BRANCH_WIDTH), lambda i, j: (i, 0))

    def gspec(n):
        return pl.BlockSpec((tm, tn), lambda i, j: (i, (Z_MG + D_MODEL * n) // tn + j))

    return pl.pallas_call(
        _merge_kernel,
        out_shape=jax.ShapeDtypeStruct((S, D_MODEL), BF16),
        grid=(S // tm, D_MODEL // tn),
        in_specs=[bspec, bspec, bspec,
                  pl.BlockSpec((None, N_BRANCH, BRANCH_WIDTH, tn), lambda i, j: (l, 0, 0, j)),
                  gspec(0), gspec(1), gspec(2)],
        out_specs=pl.BlockSpec((tm, tn), lambda i, j: (i, j)),
        compiler_params=_cparams(("arbitrary", "arbitrary")),
        name="branch_merge",
    )(o_mla, o_diff, o_nsa, w_branch, z, z, z)


def _rope_kernel(pos_ref, inv_ref, o_ref):
    ang = pos_ref[...] * inv_ref[...]
    lane = lax.broadcasted_iota(jnp.int32, ang.shape, 1)
    o_ref[...] = jnp.where(lane < MLA_ROPE, jnp.cos(ang), jnp.sin(ang))


def _rope_table(pos_col):
    S = pos_col.shape[0]
    inv = ROPE_THETA ** (-np.arange(0, MLA_ROPE, 2, dtype=np.float64) / MLA_ROPE)
    inv4 = jnp.asarray(np.tile(inv, 4)[None, :], F32)
    tm = 512
    return pl.pallas_call(
        _rope_kernel,
        out_shape=jax.ShapeDtypeStruct((S, LANE), F32),
        grid=(S // tm,),
        in_specs=[pl.BlockSpec((tm, 1), lambda i: (i, 0)),
                  pl.BlockSpec((1, LANE), lambda i: (0, 0))],
        out_specs=pl.BlockSpec((tm, LANE), lambda i: (i, 0)),
        compiler_params=_cparams(("arbitrary",)),
        name="rope_table",
    )(pos_col, inv4)


def _rms(x, g):
    return x * lax.rsqrt(jnp.mean(x * x, -1, keepdims=True) + 1e-6) * g


def _mla_proj_kernel(cq_ref, ckv_ref, kr_ref, t_ref, gq_ref, gkv_ref, wuq_ref, wukv_ref,
                     q_out, k_out, v_out):
    scale = LOG2E * (MLA_NOPE + MLA_ROPE) ** -0.5
    t = t_ref[...]
    nq = _rms(cq_ref[...].astype(F32), gq_ref[...]).astype(BF16)
    q = jnp.dot(nq, wuq_ref[...], preferred_element_type=F32)
    for h in range(MLA_HEADS):
        c = 256 * h
        q_out[:, c:c + 128] = (q[:, c:c + 128] * scale).astype(BF16)
        q_out[:, c + 128:c + 256] = (q[:, c + 128:c + 256] * t * scale).astype(BF16)
    nkv = _rms(ckv_ref[...].astype(F32), gkv_ref[...]).astype(BF16)
    kv = jnp.dot(nkv, wukv_ref[...], preferred_element_type=F32)
    kr = kr_ref[...].astype(F32) * t
    kr2 = (kr + pltpu.roll(kr, MLA_ROPE, 1)).astype(BF16)
    for h in range(MLA_HEADS):
        c = 256 * h
        k_out[:, c:c + 128] = kv[:, c:c + 128].astype(BF16)
        k_out[:, c + 128:c + 256] = kr2
        v_out[:, 128 * h:128 * h + 128] = kv[:, c + 128:c + 256].astype(BF16)


def _mla_proj(z, rope_t, gq, gkv, wuq, wukv, l):
    S = z.shape[0]
    L = gq.shape[0]
    tm = 512
    HW = MLA_HEADS * 256
    return pl.pallas_call(
        _mla_proj_kernel,
        out_shape=(jax.ShapeDtypeStruct((S, HW), BF16),
                   jax.ShapeDtypeStruct((S, HW), BF16),
                   jax.ShapeDtypeStruct((S, MLA_HEADS * MLA_V), BF16)),
        grid=(S // tm,),
        in_specs=[pl.BlockSpec((tm, MLA_Q_LORA), lambda i: (i, Z_CQ // MLA_Q_LORA)),
                  pl.BlockSpec((tm, MLA_KV_LORA), lambda i: (i, Z_CKV // MLA_KV_LORA)),
                  pl.BlockSpec((tm, LANE), lambda i: (i, Z_KR // LANE)),
                  pl.BlockSpec((tm, LANE), lambda i: (i, 0)),
                  pl.BlockSpec((None, 1, MLA_Q_LORA), lambda i: (l, 0, 0)),
                  pl.BlockSpec((None, 1, MLA_KV_LORA), lambda i: (l, 0, 0)),
                  pl.BlockSpec((None, MLA_Q_LORA, HW), lambda i: (l, 0, 0)),
                  pl.BlockSpec((None, MLA_KV_LORA, HW), lambda i: (l, 0, 0))],
        out_specs=(pl.BlockSpec((tm, HW), lambda i: (i, 0)),
                   pl.BlockSpec((tm, HW), lambda i: (i, 0)),
                   pl.BlockSpec((tm, MLA_HEADS * MLA_V), lambda i: (i, 0))),
        compiler_params=_cparams(("arbitrary",)),
        name="mla_proj",
    )(z, z, z, rope_t, gq.reshape(L, 1, -1), gkv.reshape(L, 1, -1), wuq, wukv)


def _qk(q, k):
    return lax.dot_general(q, k, (((1,), (1,)), ((), ())), preferred_element_type=F32)


def _online(s, v, m, l, acc, valid=None):
    m_new = jnp.maximum(m, jnp.max(s, -1, keepdims=True))
    a = jnp.exp(m - m_new)
    p = jnp.exp(s - m_new)
    if valid is not None:
        p = jnp.where(valid, p, 0.0)
    l = a * l + jnp.sum(p, -1, keepdims=True)
    acc = a * acc + jnp.dot(p.astype(BF16), v, preferred_element_type=F32)
    return m_new, l, acc


def _init(rows, dv):
    return (jnp.full((rows, 1), NEG, F32), jnp.zeros((rows, 1), F32), jnp.zeros((rows, dv), F32))


def _kq(k, q):
    return lax.dot_general(k, q, (((1,), (1,)), ((), ())), preferred_element_type=F32)


def _online_t(s, vt, m, l, acc):
    m_new = jnp.maximum(m, jnp.max(s, 0, keepdims=True))
    a = jnp.exp2(m - m_new)
    p = jnp.exp2(s - m_new)
    l = a * l + jnp.sum(p, 0, keepdims=True)
    acc = a * acc + jnp.dot(vt, p.astype(BF16), preferred_element_type=F32)
    return m_new, l, acc


def _init_t(nq, dv):
    return (jnp.full((1, nq), NEG, F32), jnp.zeros((1, nq), F32), jnp.zeros((dv, nq), F32))


def _transpose_into(vt_ref, v_ref):
    for c in range(v_ref.shape[0] // T_CHUNK):
        rows = slice(c * T_CHUNK, (c + 1) * T_CHUNK)
        vt_ref[:, rows] = v_ref[rows, :].astype(F32).T.astype(vt_ref.dtype)


def _put_scores(buf, t, lanes=None):
    s_ref, m_ref = buf
    if lanes is None:
        s_ref[...] = t
        m_ref[...] = jnp.max(t, 0, keepdims=True)
    else:
        s_ref[:, lanes] = t
        m_ref[:, lanes] = jnp.max(t, 0, keepdims=True)


def _sweep(n_pairs, tk, score, vt_ref, carry, buf_a, buf_b):
    def update(j, buf, carry):
        s_ref, mt_ref = buf
        m, l, acc = carry
        m_new = jnp.maximum(m, mt_ref[...])
        a = jnp.exp2(m - m_new)
        p = jnp.exp2(s_ref[...] - m_new)
        l = a * l + jnp.sum(p, 0, keepdims=True)
        vt = vt_ref[:, pl.ds(pl.multiple_of(j * tk, tk), tk)]
        acc = a * acc + jnp.dot(vt, p.astype(BF16), preferred_element_type=F32)
        return m_new, l, acc

    jd = 2 * n_pairs
    score(jd, buf_a, True)
    score(jd + 1, buf_b, True)
    carry = update(jd, buf_a, carry)

    def pair(p, carry):
        j = 2 * p
        score(j, buf_a, False)
        carry = update(jnp.where(p == 0, jd + 1, j - 1), buf_b, carry)
        score(j + 1, buf_b, False)
        return update(j, buf_a, carry)

    carry = lax.fori_loop(0, n_pairs, pair, carry)
    return update(jd - 1 + 2 * (n_pairs == 0).astype(jnp.int32), buf_b, carry)


def _score_bufs(tk, nq):
    return [pltpu.VMEM((tk, nq), F32), pltpu.VMEM((1, nq), F32),
            pltpu.VMEM((tk, nq), F32), pltpu.VMEM((1, nq), F32)]


def _mla_attn_kernel(q_ref, k_ref, v_ref, o_ref, vt_ref, sa_ref, ma_ref, sb_ref, mb_ref, *, tq, tk):
    i = pl.program_id(1)

    @pl.when(i == 0)
    def _():
        _transpose_into(vt_ref, v_ref)

    q = q_ref[...]

    def score(j, buf, masked):
        off = pl.multiple_of(j * tk, tk)
        t = _kq(k_ref[pl.ds(off, tk), :], q)
        if masked:
            kidx = off + lax.broadcasted_iota(jnp.int32, (tk, tq), 0)
            qidx = i * tq + lax.broadcasted_iota(jnp.int32, (tk, tq), 1)
            t = jnp.where(kidx <= qidx, t, NEG)
        _put_scores(buf, t)

    _, l, acc = _sweep(i, tk, score, vt_ref, _init_t(tq, MLA_V), (sa_ref, ma_ref), (sb_ref, mb_ref))
    o_ref[...] = (acc / l).T.astype(o_ref.dtype)


def _mla_attn(qm, km, vm):
    S = qm.shape[0]
    tq, tk = 1024, 512
    return pl.pallas_call(
        functools.partial(_mla_attn_kernel, tq=tq, tk=tk),
        out_shape=jax.ShapeDtypeStruct((S, MLA_HEADS * MLA_V), BF16),
        grid=(MLA_HEADS, S // tq),
        in_specs=[pl.BlockSpec((tq, 256), lambda h, i: (i, h)),
                  pl.BlockSpec((S, 256), lambda h, i: (0, h)),
                  pl.BlockSpec((S, MLA_V), lambda h, i: (0, h))],
        out_specs=pl.BlockSpec((tq, MLA_V), lambda h, i: (i, h)),
        scratch_shapes=[pltpu.VMEM((MLA_V, S), BF16)] + _score_bufs(tk, tq),
        compiler_params=_cparams(("arbitrary", "arbitrary")),
        name="mla_attn",
    )(qm, km, vm)


def _alibi_features(pkb, coef):
    hi, mid, lo = _split3(coef * pkb)
    lane = lax.broadcasted_iota(jnp.int32, hi.shape, 1)
    zero = jnp.zeros_like(hi)
    return jnp.where(lane == 0, hi, jnp.where(lane == 1, mid, jnp.where(lane == 2, lo, zero)))


def _alibi_key_features(kf_ref, pkb_ref, coef):
    for c in range(pkb_ref.shape[0] // T_CHUNK):
        rows = slice(c * T_CHUNK, (c + 1) * T_CHUNK)
        kf_ref[rows, :] = _alibi_features(pkb_ref[rows, :], coef)


def _diff_attn_kernel(sc_ref, q_ref, k_ref, v_ref, pkb_ref, lam_ref, g_ref, o_ref,
                      vt_ref, kf_ref, sa_ref, ma_ref, sb_ref, mb_ref, *, tq, tk):
    h = pl.program_id(0)
    i = pl.program_id(1)

    @pl.when(i == 0)
    def _():
        _transpose_into(vt_ref, v_ref)
        _alibi_key_features(kf_ref, pkb_ref, sc_ref[h])

    lam_init = sc_ref[DIFF_HEADS]
    qs = (q_ref[...].astype(F32) * (LOG2E * DIFF_QK ** -0.5)).astype(BF16)
    lane = lax.broadcasted_iota(jnp.int32, qs.shape, 1)
    zero = jnp.zeros_like(qs)
    ones = jnp.where(lane < N_SPLIT, 1.0, 0.0).astype(BF16)
    q2 = jnp.concatenate([jnp.concatenate([jnp.where(lane < DIFF_QK, qs, zero), ones], 1),
                          jnp.concatenate([jnp.where(lane >= DIFF_QK, qs, zero), ones], 1)], 0)

    def score(j, buf, masked):
        off = pl.multiple_of(j * tk, tk)
        kx = jnp.concatenate([k_ref[pl.ds(off, tk), :], kf_ref[pl.ds(off, tk), :]], 1)
        t = _kq(kx, q2)
        if masked:
            kidx = off + lax.broadcasted_iota(jnp.int32, (tk, tq), 0)
            qidx = i * tq + lax.broadcasted_iota(jnp.int32, (tk, tq), 1)
            ok = kidx <= qidx
            for c in range(2):
                lanes = slice(c * tq, (c + 1) * tq)
                _put_scores(buf, jnp.where(ok, t[:, lanes], NEG), lanes)
        else:
            _put_scores(buf, t)

    _, l, acc = _sweep(i, tk, score, vt_ref, _init_t(2 * tq, DIFF_V), (sa_ref, ma_ref), (sb_ref, mb_ref))
    on = acc / l
    lp = lam_ref[...]
    lam = (jnp.exp(jnp.sum(lp[0:1] * lp[1:2], -1, keepdims=True))
           - jnp.exp(jnp.sum(lp[2:3] * lp[3:4], -1, keepdims=True)) + lam_init)
    o = on[:, :tq] - lam * on[:, tq:]
    o = o * lax.rsqrt(jnp.mean(o * o, 0, keepdims=True) + 1e-6)
    o_ref[...] = (o.T * g_ref[...] * (1.0 - lam_init)).astype(o_ref.dtype)


def _diff_attn(z, pos_b, scal, lam_p, g_sub, l):
    S = z.shape[0]
    L = g_sub.shape[0]
    tq, tk = 1024, 512
    return pl.pallas_call(
        functools.partial(_diff_attn_kernel, tq=tq, tk=tk),
        out_shape=jax.ShapeDtypeStruct((S, DIFF_HEADS * DIFF_V), BF16),
        grid=(DIFF_HEADS, S // tq),
        in_specs=[pl.BlockSpec(memory_space=pltpu.SMEM),
                  pl.BlockSpec((tq, LANE), lambda h, i: (i, Z_DQ // LANE + h)),
                  pl.BlockSpec((S, LANE), lambda h, i: (0, Z_DK // LANE + h)),
                  pl.BlockSpec((S, LANE), lambda h, i: (0, Z_DV // LANE + h)),
                  pl.BlockSpec((S, LANE), lambda h, i: (0, 0)),
                  pl.BlockSpec((None, 4, DIFF_QK), lambda h, i: (l, 0, 0)),
                  pl.BlockSpec((None, 1, DIFF_V), lambda h, i: (l, 0, 0))],
        out_specs=pl.BlockSpec((tq, DIFF_V), lambda h, i: (i, h)),
        scratch_shapes=[pltpu.VMEM((DIFF_V, S), BF16), pltpu.VMEM((S, LANE), BF16)]
        + _score_bufs(tk, 2 * tq),
        compiler_params=_cparams(("arbitrary", "arbitrary")),
        name="diff_attn",
    )(scal, z, z, z, pos_b, lam_p, g_sub.reshape(L, 1, -1))


def _compress_kernel(x_ref, w1_ref, b1_ref, w2_ref, o_ref):
    x = x_ref[...]
    half = CMP_STRIDE * NSA_DH
    a = jnp.dot(x, w1_ref[0:half, :].astype(BF16), preferred_element_type=F32)
    b = jnp.dot(x, w1_ref[half:2 * half, :].astype(BF16), preferred_element_type=F32)
    n = x.shape[0]
    pre = a + pltpu.roll(b, n - 1, 0) + b1_ref[...]
    hdn = 0.5 * pre * (1.0 + jnp.tanh(0.7978845608028654 * (pre + 0.044715 * pre * pre * pre)))
    o_ref[...] = jnp.dot(hdn.astype(BF16), w2_ref[...].astype(BF16),
                         preferred_element_type=F32).astype(o_ref.dtype)


def _compress(xc, w1, b1, w2, l):
    _, nb, kk = xc.shape
    L = w1.shape[0]
    return pl.pallas_call(
        _compress_kernel,
        out_shape=jax.ShapeDtypeStruct((4, nb, NSA_DH), BF16),
        grid=(4,),
        in_specs=[pl.BlockSpec((None, nb, kk), lambda c: (c, 0, 0)),
                  pl.BlockSpec((None, None, 2 * kk, NSA_DH), lambda c: (l, c // 2, 0, 0)),
                  pl.BlockSpec((None, None, 1, NSA_DH), lambda c: (l, c // 2, 0, 0)),
                  pl.BlockSpec((None, None, NSA_DH, NSA_DH), lambda c: (l, c // 2, 0, 0))],
        out_specs=pl.BlockSpec((None, nb, NSA_DH), lambda c: (c, 0, 0)),
        compiler_params=_cparams(("arbitrary",)),
        name="nsa_compress",
    )(xc, w1, b1.reshape(L, 2, 1, NSA_DH), w2)


def _split3(x):
    hi = x.astype(BF16)
    r = x - hi.astype(F32)
    mid = r.astype(BF16)
    lo = (r - mid.astype(F32)).astype(BF16)
    return hi, mid, lo


def _cmp_sel_kernel(sc_ref, q_ref, kc_ref, vc_ref, pcb_ref, o_ref, selt_ref, *, tq, n_slc):
    g = pl.program_id(0)
    i = pl.program_id(1)
    nc = kc_ref.shape[0]
    q = (q_ref[...].astype(F32) * (LOG2E * NSA_DH ** -0.5)).astype(BF16)
    lane = lax.broadcasted_iota(jnp.int32, (tq, NSA_DH), 1)
    q4 = jnp.concatenate(
        [jnp.concatenate([q[:, r * NSA_DH:(r + 1) * NSA_DH],
                          jnp.where(lane < N_SPLIT, sc_ref[g * NSA_REP + r], 0.0).astype(BF16)], 1)
         for r in range(NSA_REP)], 0)
    kx = jnp.concatenate([kc_ref[...], _alibi_features(pcb_ref[...], LOG2E)], 1)
    t = _kq(kx, q4)
    vct = vc_ref[...].astype(F32).T.astype(BF16)
    qidx = i * tq + lax.broadcasted_iota(jnp.int32, (nc, tq), 1)
    cmp_end = lax.broadcasted_iota(jnp.int32, (nc, tq), 0) * CMP_STRIDE + (CMP_BLOCK - 1)
    vis = cmp_end <= qidx
    p_sum = jnp.zeros((nc, tq), F32)
    for r in range(NSA_REP):
        s = jnp.where(vis, t[:, r * tq:(r + 1) * tq], NEG)
        e = jnp.where(vis, jnp.exp2(s - jnp.max(s, 0, keepdims=True)), 0.0)
        d = jnp.sum(e, 0, keepdims=True)
        p = e * (1.0 / jnp.where(d > 0.0, d, 1.0))
        p_sum = p_sum + p
        o = jnp.dot(vct, p.astype(BF16), preferred_element_type=F32)
        o_ref[:, r * NSA_DH:(r + 1) * NSA_DH] = o.T.astype(o_ref.dtype)
    jj = lax.broadcasted_iota(jnp.int32, (n_slc, nc), 0)
    nn = lax.broadcasted_iota(jnp.int32, (n_slc, nc), 1)
    lo_n = jj * (SLC_BLOCK // CMP_STRIDE) - 1
    ovl = jnp.where((nn >= lo_n) & (nn <= lo_n + 4), 1.0, 0.0).astype(BF16)
    imp = sum(jnp.dot(ovl, part, preferred_element_type=F32) for part in _split3(p_sum))
    blk = lax.broadcasted_iota(jnp.int32, (n_slc, tq), 0).astype(F32)
    cur = lax.shift_right_logical(i * tq + lax.broadcasted_iota(jnp.int32, (n_slc, tq), 1),
                                  int(math.log2(SLC_BLOCK))).astype(F32)
    forced = (blk == 0.0) | (blk == cur) | (blk == cur - 1.0)
    causal = blk <= cur
    score = jnp.where(forced, 3e38, jnp.where(causal, imp, -1.0))
    for _ in range(min(N_SELECT, n_slc)):
        mx = jnp.max(score, 0, keepdims=True)
        first = jnp.min(jnp.where(score == mx, blk, float(n_slc)), 0, keepdims=True)
        score = jnp.where(blk == first, PICKED, score)
    selt_ref[...] = jnp.where(causal & (score == PICKED), 1.0, 0.0).astype(selt_ref.dtype)


def _cmp_sel(z, cmp_kv, pos_cmp_b, scal):
    S = z.shape[0]
    nc = cmp_kv.shape[1]
    n_slc = S // SLC_BLOCK
    tq = 512
    gw = NSA_REP * NSA_DH
    return pl.pallas_call(
        functools.partial(_cmp_sel_kernel, tq=tq, n_slc=n_slc),
        out_shape=(jax.ShapeDtypeStruct((S, NSA_HEADS * NSA_DH), BF16),
                   jax.ShapeDtypeStruct((NSA_KV_HEADS, n_slc, S), BF16)),
        grid=(NSA_KV_HEADS, S // tq),
        in_specs=[pl.BlockSpec(memory_space=pltpu.SMEM),
                  pl.BlockSpec((tq, gw), lambda g, i: (i, Z_NQ // gw + g)),
                  pl.BlockSpec((None, nc, NSA_DH), lambda g, i: (g, 0, 0)),
                  pl.BlockSpec((None, nc, NSA_DH), lambda g, i: (2 + g, 0, 0)),
                  pl.BlockSpec((nc, LANE), lambda g, i: (0, 0))],
        out_specs=(pl.BlockSpec((tq, gw), lambda g, i: (i, g)),
                   pl.BlockSpec((None, n_slc, tq), lambda g, i: (g, 0, i))),
        compiler_params=_cparams(("arbitrary", "arbitrary")),
        name="nsa_cmp_select",
    )(scal, z, cmp_kv, cmp_kv, pos_cmp_b)


def _slc_attn_kernel(sc_ref, q_ref, k_ref, v_ref, sel_ref, pkb_ref, o_ref,
                     vt_ref, kf_ref, sa_ref, ma_ref, sb_ref, mb_ref, *, tq, tk):
    g = pl.program_id(0)
    i = pl.program_id(1)

    @pl.when(i == 0)
    def _():
        _transpose_into(vt_ref, v_ref)
        _alibi_key_features(kf_ref, pkb_ref, LOG2E)

    n_slc = sel_ref.shape[0]
    q = (q_ref[...].astype(F32) * (LOG2E * NSA_DH ** -0.5)).astype(BF16)
    lane = lax.broadcasted_iota(jnp.int32, (tq, NSA_DH), 1)
    q4 = jnp.concatenate(
        [jnp.concatenate([q[:, r * NSA_DH:(r + 1) * NSA_DH],
                          jnp.where(lane < N_SPLIT, sc_ref[g * NSA_REP + r], 0.0).astype(BF16)], 1)
         for r in range(NSA_REP)], 0)
    sel = sel_ref[...]
    blk = lax.broadcasted_iota(jnp.int32, (tk, n_slc), 1)
    kblk0 = lax.shift_right_logical(lax.broadcasted_iota(jnp.int32, (tk, n_slc), 0),
                                    int(math.log2(SLC_BLOCK)))

    def score(j, buf, masked):
        off = pl.multiple_of(j * tk, tk)
        kx = jnp.concatenate([k_ref[pl.ds(off, tk), :], kf_ref[pl.ds(off, tk), :]], 1)
        t = _kq(kx, q4)
        expand = jnp.where(blk == kblk0 + j * (tk // SLC_BLOCK), 1.0, 0.0).astype(BF16)
        ok = jnp.dot(expand, sel, preferred_element_type=F32) > 0.5
        if masked:
            kidx = off + lax.broadcasted_iota(jnp.int32, (tk, tq), 0)
            qidx = i * tq + lax.broadcasted_iota(jnp.int32, (tk, tq), 1)
            ok = ok & (kidx <= qidx)
        for r in range(NSA_REP):
            lanes = slice(r * tq, (r + 1) * tq)
            _put_scores(buf, jnp.where(ok, t[:, lanes], NEG), lanes)

    _, l, acc = _sweep(i, tk, score, vt_ref, _init_t(NSA_REP * tq, NSA_DH), (sa_ref, ma_ref), (sb_ref, mb_ref))
    on = acc / l
    for r in range(NSA_REP):
        o_ref[:, r * NSA_DH:(r + 1) * NSA_DH] = on[:, r * tq:(r + 1) * tq].T.astype(o_ref.dtype)


def _slc_attn(z, sel, pos_b, scal):
    S = z.shape[0]
    n_slc = S // SLC_BLOCK
    tq, tk = 512, 256
    gw = NSA_REP * NSA_DH
    kcol = Z_NKV // LANE + 4
    return pl.pallas_call(
        functools.partial(_slc_attn_kernel, tq=tq, tk=tk),
        out_shape=jax.ShapeDtypeStruct((S, NSA_HEADS * NSA_DH), BF16),
        grid=(NSA_KV_HEADS, S // tq),
        in_specs=[pl.BlockSpec(memory_space=pltpu.SMEM),
                  pl.BlockSpec((tq, gw), lambda g, i: (i, Z_NQ // gw + g)),
                  pl.BlockSpec((S, NSA_DH), lambda g, i: (0, kcol + g)),
                  pl.BlockSpec((S, NSA_DH), lambda g, i: (0, kcol + 2 + g)),
                  pl.BlockSpec((None, n_slc, tq), lambda g, i: (g, 0, i)),
                  pl.BlockSpec((S, LANE), lambda g, i: (0, 0))],
        out_specs=pl.BlockSpec((tq, gw), lambda g, i: (i, g)),
        scratch_shapes=[pltpu.VMEM((NSA_DH, S), BF16), pltpu.VMEM((S, LANE), BF16)]
        + _score_bufs(tk, NSA_REP * tq),
        compiler_params=_cparams(("arbitrary", "arbitrary")),
        name="nsa_slc_attn",
    )(scal, z, z, z, sel, pos_b)


def _win_attn_kernel(sc_ref, q_ref, k_ref, v_ref, pkb_ref, ocmp_ref, oslc_ref, gate_ref,
                     o_ref, vt_ref, kf_ref, *, t):
    g = pl.program_id(0)
    i = pl.program_id(1)

    @pl.when(i == 0)
    def _():
        _transpose_into(vt_ref, v_ref)
        _alibi_key_features(kf_ref, pkb_ref, LOG2E)

    span = t + WINDOW
    n_t = k_ref.shape[0] // t
    start = pl.multiple_of(jnp.clip(i - WINDOW // t, 0, n_t - span // t) * t, t)
    q = (q_ref[...].astype(F32) * (LOG2E * NSA_DH ** -0.5)).astype(BF16)
    lane = lax.broadcasted_iota(jnp.int32, (t, NSA_DH), 1)
    q4 = jnp.concatenate(
        [jnp.concatenate([q[:, r * NSA_DH:(r + 1) * NSA_DH],
                          jnp.where(lane < N_SPLIT, sc_ref[g * NSA_REP + r], 0.0).astype(BF16)], 1)
         for r in range(NSA_REP)], 0)
    kx = jnp.concatenate([k_ref[pl.ds(start, span), :], kf_ref[pl.ds(start, span), :]], 1)
    s = _kq(kx, q4)
    dif = (i * t + lax.broadcasted_iota(jnp.int32, (span, t), 1)
           - start - lax.broadcasted_iota(jnp.int32, (span, t), 0))
    ok = (dif >= 0) & (dif < WINDOW)
    vt = vt_ref[:, pl.ds(start, span)]
    gates = jax.nn.sigmoid(gate_ref[...].astype(F32))
    for r in range(NSA_REP):
        sr = jnp.where(ok, s[:, r * t:(r + 1) * t], NEG)
        e = jnp.exp2(sr - jnp.max(sr, 0, keepdims=True))
        d = jnp.sum(e, 0, keepdims=True)
        o_win = (jnp.dot(vt, e.astype(BF16), preferred_element_type=F32) * (1.0 / d)).T
        sl = slice(r * NSA_DH, (r + 1) * NSA_DH)

        def gcol(b):
            c0, c1 = 3 * r + b, 3 * (NSA_REP + r) + b
            return jnp.where(g == 0, gates[:, c0:c0 + 1], gates[:, c1:c1 + 1])

        o = (gcol(0) * ocmp_ref[:, sl].astype(F32) + gcol(1) * oslc_ref[:, sl].astype(F32)
             + gcol(2) * o_win)
        o_ref[:, sl] = o.astype(o_ref.dtype)


def _win_attn(z, o_cmp, o_slc, pos_b, scal):
    S = z.shape[0]
    t = 256
    gw = NSA_REP * NSA_DH
    kcol = Z_NKV // LANE + 8
    return pl.pallas_call(
        functools.partial(_win_attn_kernel, t=t),
        out_shape=jax.ShapeDtypeStruct((S, NSA_HEADS * NSA_DH), BF16),
        grid=(NSA_KV_HEADS, S // t),
        in_specs=[pl.BlockSpec(memory_space=pltpu.SMEM),
                  pl.BlockSpec((t, gw), lambda g, i: (i, Z_NQ // gw + g)),
                  pl.BlockSpec((S, NSA_DH), lambda g, i: (0, kcol + g)),
                  pl.BlockSpec((S, NSA_DH), lambda g, i: (0, kcol + 2 + g)),
                  pl.BlockSpec((S, LANE), lambda g, i: (0, 0)),
                  pl.BlockSpec((t, gw), lambda g, i: (i, g)),
                  pl.BlockSpec((t, gw), lambda g, i: (i, g)),
                  pl.BlockSpec((t, LANE), lambda g, i: (i, Z_NG // LANE))],
        out_specs=pl.BlockSpec((t, gw), lambda g, i: (i, g)),
        scratch_shapes=[pltpu.VMEM((NSA_DH, S), BF16), pltpu.VMEM((S, LANE), BF16)],
        compiler_params=_cparams(("arbitrary", "arbitrary")),
        name="nsa_win_attn",
    )(scal, z, z, z, pos_b, o_cmp, o_slc, z)


def kernel(x, c, positions, w_ada, b_ada, w_in, mla_q_norm, mla_kv_norm, mla_w_uq, mla_w_ukv, diff_lambda, diff_subln, nsa_cmp_w1, nsa_cmp_b1, nsa_cmp_w2, w_branch, w_out, ln1_g, ln1_b, w_ff1, w_ff2, ln2_g, ln2_b):
    B, S, D = x.shape
    L = w_ada.shape[0]
    assert B == 1 and D == D_MODEL and S % 512 == 0 and S // SLC_BLOCK >= 3

    w_in_x = _relayout_w_in(w_in)
    w_uq_x = _relayout_w_uq(mla_w_uq)
    w_ukv_b = mla_w_ukv.astype(BF16)
    w_branch_b = w_branch.astype(BF16)
    w_out_b = w_out.astype(BF16)
    w_ff1_b = w_ff1.astype(BF16)
    w_ff2_b = w_ff2.astype(BF16)

    slopes = 2.0 ** (-8.0 * np.arange(1, N_ALIBI_HEADS + 1, dtype=np.float64) / N_ALIBI_HEADS)
    s_diff, s_nsa = slopes[0::2], slopes[1::2]
    nsa_scal = jnp.asarray(s_nsa, F32)

    pos_f = positions.astype(F32)
    pos_col = pos_f.reshape(S, 1)
    pos_b = jnp.broadcast_to(pos_col, (S, LANE))
    pos_cmp_b = jnp.broadcast_to(jnp.pad(pos_f[0, CMP_BLOCK - 1::CMP_STRIDE], (0, 1))[:, None],
                                 (S // CMP_STRIDE, LANE))

    mod = _ada(c, w_ada, b_ada)
    rope_t = _rope_table(pos_col)
    xs = x.reshape(S, D)

    for l in range(L):
        lam_init = 0.8 - 0.6 * math.exp(-0.3 * l)
        diff_scal = jnp.asarray(np.concatenate([s_diff * LOG2E, [lam_init]]), F32)

        h = _lnmod(xs, mod, l, 0, 1)
        z = _matmul(h, w_in_x, l, tm=1024, tn=1280, relu2=False, name="in_proj")

        qm, km, vm = _mla_proj(z, rope_t, mla_q_norm, mla_kv_norm, w_uq_x, w_ukv_b, l)
        o_mla = _mla_attn(qm, km, vm)

        o_diff = _diff_attn(z, pos_b, diff_scal, diff_lambda, diff_subln, l)

        xc = z[:, Z_NKV:Z_NKV + 4 * NSA_DH].reshape(S // CMP_STRIDE, CMP_STRIDE, 4, NSA_DH)
        xc = xc.transpose(2, 0, 1, 3).reshape(4, S // CMP_STRIDE, CMP_STRIDE * NSA_DH)
        cmp_kv = _compress(xc, nsa_cmp_w1, nsa_cmp_b1, nsa_cmp_w2, l)
        o_cmp, sel = _cmp_sel(z, cmp_kv, pos_cmp_b, nsa_scal)
        o_slc = _slc_attn(z, sel, pos_b, nsa_scal)
        o_nsa = _win_attn(z, o_cmp, o_slc, pos_b, nsa_scal)

        merged = _merge(o_mla, o_diff, o_nsa, w_branch_b, z, l)
        xs = _matmul_res_ln(merged, w_out_b, xs, mod, ln1_g, ln1_b, l, 2, ALPHA, "out_proj_ln")

        h = _lnmod(xs, mod, l, 3, 4)
        u = _matmul(h, w_ff1_b, l, tm=1024, tn=1024, relu2=True, name="ff1")
        xs = _matmul_res_ln(u, w_ff2_b, xs, mod, ln2_g, ln2_b, l, 5, ALPHA, "ff2_ln")

    return xs.reshape(B, S, D)
```

```python
import functools
import math

import numpy as np
import jax
import jax.numpy as jnp
from jax import lax
from jax.experimental import pallas as pl
from jax.experimental.pallas import tpu as pltpu

F32 = jnp.float32
BF16 = jnp.bfloat16

D_MODEL = 2048
MLA_HEADS = 8
MLA_Q_LORA = 512
MLA_KV_LORA = 256
MLA_NOPE = 128
MLA_ROPE = 64
MLA_V = 128
ROPE_THETA = 10000.0
DIFF_HEADS = 8
DIFF_QK = 64
DIFF_V = 128
NSA_HEADS = 8
NSA_KV_HEADS = 2
NSA_REP = NSA_HEADS // NSA_KV_HEADS
NSA_DH = 128
CMP_BLOCK = 32
CMP_STRIDE = 16
SLC_BLOCK = 64
N_SELECT = 16
WINDOW = 512
N_ALIBI_HEADS = DIFF_HEADS + NSA_HEADS
N_BRANCH = 3
BRANCH_WIDTH = 1024
D_FF = 4 * D_MODEL
N_MOD = 6
DEPTH = 4
ALPHA = (2.0 * DEPTH) ** 0.25

LANE = 128
NEG = -1e30
LOG2E = math.log2(math.e)
T_CHUNK = 512
N_SPLIT = 3
PICKED = -2.0
VMEM_LIMIT = 48 * 1024 * 1024

Z_CQ = 0
Z_DQ = 512
Z_DK = 1536
Z_DV = 2560
Z_NQ = 3584
Z_NKV = 4608
Z_MG = 6144
Z_CKV = 12288
Z_KR = 12544
Z_NG = 12672
Z_W = 12800


def _cparams(sem, vmem=VMEM_LIMIT):
    return pltpu.CompilerParams(dimension_semantics=sem, vmem_limit_bytes=vmem)


def _rot_half(w):
    half = MLA_ROPE // 2
    return jnp.concatenate([-w[..., half:], w[..., :half]], axis=-1)


def _relayout_w_in_kernel(wt_ref, o_ref):
    o = np.cumsum([0, 512, 256, 64, 1024, 1024, 1024, 1024, 1536, 24, N_BRANCH * D_MODEL])
    x = wt_ref[...]
    cq, ckv, kr, dq, dk, dv, nq, nkv, ng, mg = (x[o[k]:o[k + 1], :] for k in range(10))
    half = MLA_ROPE // 2
    kr_rot = jnp.concatenate([-kr[half:, :], kr[:half, :]], axis=0)
    pad = jnp.zeros((Z_W - Z_NG - ng.shape[0], x.shape[1]), x.dtype)
    o_ref[...] = jnp.concatenate([cq, dq, dk, dv, nq, nkv, mg, ckv, kr, kr_rot, ng, pad],
                                 axis=0).astype(o_ref.dtype)


def _relayout_w_in(w_in):
    L, D, N = w_in.shape
    tc = 256
    return pl.pallas_call(
        _relayout_w_in_kernel,
        out_shape=jax.ShapeDtypeStruct((L, Z_W, D), BF16),
        grid=(L, D // tc),
        in_specs=[pl.BlockSpec((None, N, tc), lambda l, i: (l, 0, i))],
        out_specs=pl.BlockSpec((None, Z_W, tc), lambda l, i: (l, 0, i)),
        compiler_params=_cparams(("arbitrary", "arbitrary")),
        name="w_in_relayout",
    )(jnp.swapaxes(w_in, 1, 2))


def _relayout_w_uq(w_uq):
    L, K, _ = w_uq.shape
    w = w_uq.reshape(L, K, MLA_HEADS, MLA_NOPE + MLA_ROPE)
    rope = w[..., MLA_NOPE:]
    return jnp.concatenate([w, _rot_half(rope)], axis=-1).reshape(L, K, MLA_HEADS * 256).astype(BF16)


def _ada_kernel(c_ref, w_ref, b_ref, o_ref):
    c = c_ref[...]
    ca = c * jax.nn.sigmoid(c)
    ca8 = jnp.broadcast_to(ca, (8, c.shape[1])).astype(BF16)
    r = jnp.dot(ca8, w_ref[...].astype(BF16), preferred_element_type=F32)
    o_ref[...] = r[0:1, :] + b_ref[...]


def _ada(c, w_ada, b_ada):
    L, D, N = w_ada.shape
    tn = 1536
    return pl.pallas_call(
        _ada_kernel,
        out_shape=jax.ShapeDtypeStruct((L, 1, N), F32),
        grid=(L, N // tn),
        in_specs=[pl.BlockSpec((1, D), lambda l, j: (0, 0)),
                  pl.BlockSpec((None, D, tn), lambda l, j: (l, 0, j)),
                  pl.BlockSpec((None, 1, tn), lambda l, j: (l, 0, j))],
        out_specs=pl.BlockSpec((None, 1, tn), lambda l, j: (l, 0, j)),
        compiler_params=_cparams(("arbitrary", "arbitrary")),
        name="ada_mod",
    )(c, w_ada, b_ada.reshape(L, 1, N))


def _lnmod_kernel(x_ref, sh_ref, sc_ref, o_ref):
    x = x_ref[...]
    mu = jnp.mean(x, -1, keepdims=True)
    xc = x - mu
    var = jnp.mean(xc * xc, -1, keepdims=True)
    y = xc * lax.rsqrt(var + 1e-5)
    o_ref[...] = (y * (1.0 + sc_ref[...]) + sh_ref[...]).astype(o_ref.dtype)


def _lnmod(x, mod, l, k_shift, k_scale):
    S, D = x.shape
    tm = 512
    return pl.pallas_call(
        _lnmod_kernel,
        out_shape=jax.ShapeDtypeStruct((S, D), BF16),
        grid=(S // tm,),
        in_specs=[pl.BlockSpec((tm, D), lambda i: (i, 0)),
                  pl.BlockSpec((None, 1, D), lambda i: (l, 0, k_shift)),
                  pl.BlockSpec((None, 1, D), lambda i: (l, 0, k_scale))],
        out_specs=pl.BlockSpec((tm, D), lambda i: (i, 0)),
        compiler_params=_cparams(("arbitrary",)),
        name="ln_modulate",
    )(x, mod, mod)


def _mm_kernel(x_ref, w_ref, o_ref, *, relu2, w_transposed):
    w = w_ref[...].astype(BF16)
    contract = (((1,), (1,)), ((), ())) if w_transposed else (((1,), (0,)), ((), ()))
    r = lax.dot_general(x_ref[...], w, contract, preferred_element_type=F32)
    if relu2:
        r = jnp.square(jnp.maximum(r, 0.0))
    o_ref[...] = r.astype(o_ref.dtype)


def _matmul(x, w, l, *, tm, tn, relu2, name, w_transposed=False):
    S, K = x.shape
    if w_transposed:
        N = w.shape[1]
        w_spec = pl.BlockSpec((None, tn, K), lambda i, j: (l, j, 0))
    else:
        N = w.shape[2]
        w_spec = pl.BlockSpec((None, K, tn), lambda i, j: (l, 0, j))
    return pl.pallas_call(
        functools.partial(_mm_kernel, relu2=relu2, w_transposed=w_transposed),
        out_shape=jax.ShapeDtypeStruct((S, N), BF16),
        grid=(S // tm, N // tn),
        in_specs=[pl.BlockSpec((tm, K), lambda i, j: (i, 0)), w_spec],
        out_specs=pl.BlockSpec((tm, tn), lambda i, j: (i, j)),
        compiler_params=_cparams(("arbitrary", "arbitrary")),
        name=name,
    )(x, w)


def _ln_plain(y):
    mu = jnp.mean(y, -1, keepdims=True)
    yc = y - mu
    return yc * lax.rsqrt(jnp.mean(yc * yc, -1, keepdims=True) + 1e-5)


def _mm_res_ln_kernel(a_ref, w_ref, x_ref, gate_ref, lng_ref, lnb_ref, *rest, nk, alpha, with_next):
    o_ref = rest[2] if with_next else rest[0]
    k = pl.program_id(1)
    part = jnp.dot(a_ref[...], w_ref[...], preferred_element_type=F32)

    @pl.when(k == 0)
    def _():
        o_ref[...] = part

    @pl.when(k > 0)
    def _():
        o_ref[...] += part

    @pl.when(k == nk - 1)
    def _():
        x_new = _ln_plain(alpha * x_ref[...] + gate_ref[...] * o_ref[...]) * lng_ref[...] + lnb_ref[...]
        o_ref[...] = x_new
        if with_next:
            sh_ref, sc_ref, _, h_ref = rest
            h_ref[...] = (_ln_plain(x_new) * (1.0 + sc_ref[...]) + sh_ref[...]).astype(h_ref.dtype)


def _matmul_res_ln(a, w, x, mod, lng, lnb, l, k_gate, alpha, name, next_mod=None):
    S, K = a.shape
    D = x.shape[1]
    L = lng.shape[0]
    tm, tk = 512, 2048
    nk = K // tk
    vmem = VMEM_LIMIT + (2 * tm * D * 2 if next_mod is not None else 0)
    vec =lambda ll, kk: pl.BlockSpec((None, 1, D), lambda i, k: (ll, 0, kk))
    row = pl.BlockSpec((tm, D), lambda i, k: (i, 0))
    in_specs = [pl.BlockSpec((tm, tk), lambda i, k: (i, k)),
                pl.BlockSpec((None, tk, D), lambda i, k: (l, k, 0)),
                row, vec(l, k_gate), vec(l, 0), vec(l, 0)]
    args = [a, w, x, mod, lng.reshape(L, 1, D), lnb.reshape(L, 1, D)]
    out_shape, out_specs = jax.ShapeDtypeStruct((S, D), F32), row
    if next_mod is not None:
        in_specs += [vec(next_mod[0], next_mod[1]), vec(next_mod[0], next_mod[2])]
        args += [mod, mod]
        out_shape, out_specs = (out_shape, jax.ShapeDtypeStruct((S, D), BF16)), (row, row)
    return pl.pallas_call(
        functools.partial(_mm_res_ln_kernel, nk=nk, alpha=alpha, with_next=next_mod is not None),
        out_shape=out_shape,
        grid=(S // tm, nk),
        in_specs=in_specs,
        out_specs=out_specs,
        compiler_params=_cparams(("arbitrary", "arbitrary"), vmem),
        name=name,
    )(*args)


def _merge_kernel(o0_ref, o1_ref, o2_ref, w_ref, g0_ref, g1_ref, g2_ref, out_ref):
    acc = None
    for n, (o_ref, g_ref) in enumerate(((o0_ref, g0_ref), (o1_ref, g1_ref), (o2_ref, g2_ref))):
        t = jax.nn.sigmoid(g_ref[...].astype(F32)) * jnp.dot(o_ref[...], w_ref[n].astype(BF16),
                                                               preferred_element_type=F32)
        acc = t if acc is None else acc + t
    out_ref[...] = acc.astype(out_ref.dtype)


def _merge(o_mla, o_diff, o_nsa, w_branch, z, l):
    S = o_mla.shape[0]
    tm, tn = 1024, 512
    bspec = pl.BlockSpec((tm, BRANCH_WIDTH), lambda i, j: (i, 0))

    def gspec(n):
        return pl.BlockSpec((tm, tn), lambda i, j: (i, (Z_MG + D_MODEL * n) // tn + j))

    return pl.pallas_call(
        _merge_kernel,
        out_shape=jax.ShapeDtypeStruct((S, D_MODEL), BF16),
        grid=(S // tm, D_MODEL // tn),
        in_specs=[bspec, bspec, bspec,
                  pl.BlockSpec((None, N_BRANCH, BRANCH_WIDTH, tn), lambda i, j: (l, 0, 0, j)),
                  gspec(0), gspec(1), gspec(2)],
        out_specs=pl.BlockSpec((tm, tn), lambda i, j: (i, j)),
        compiler_params=_cparams(("arbitrary", "arbitrary")),
        name="branch_merge",
    )(o_mla, o_diff, o_nsa, w_branch, z, z, z)


def _rope_kernel(pos_ref, inv_ref, o_ref):
    ang = pos_ref[...] * inv_ref[...]
    lane = lax.broadcasted_iota(jnp.int32, ang.shape, 1)
    o_ref[...] = jnp.where(lane < MLA_ROPE, jnp.cos(ang), jnp.sin(ang))


def _rope_table(pos_col):
    S = pos_col.shape[0]
    inv = ROPE_THETA ** (-np.arange(0, MLA_ROPE, 2, dtype=np.float64) / MLA_ROPE)
    inv4 = jnp.asarray(np.tile(inv, 4)[None, :], F32)
    tm = 512
    return pl.pallas_call(
        _rope_kernel,
        out_shape=jax.ShapeDtypeStruct((S, LANE), F32),
        grid=(S // tm,),
        in_specs=[pl.BlockSpec((tm, 1), lambda i: (i, 0)),
                  pl.BlockSpec((1, LANE), lambda i: (0, 0))],
        out_specs=pl.BlockSpec((tm, LANE), lambda i: (i, 0)),
        compiler_params=_cparams(("arbitrary",)),
        name="rope_table",
    )(pos_col, inv4)


def _rms(x, g):
    return x * lax.rsqrt(jnp.mean(x * x, -1, keepdims=True) + 1e-6) * g


def _mla_proj_kernel(cq_ref, ckv_ref, kr_ref, t_ref, gq_ref, gkv_ref, wuq_ref, wukv_ref,
                     q_out, k_out, v_out):
    scale = LOG2E * (MLA_NOPE + MLA_ROPE) ** -0.5
    t = t_ref[...]
    nq = _rms(cq_ref[...].astype(F32), gq_ref[...]).astype(BF16)
    q = jnp.dot(nq, wuq_ref[...], preferred_element_type=F32)
    for h in range(MLA_HEADS):
        c = 256 * h
        q_out[:, c:c + 128] = (q[:, c:c + 128] * scale).astype(BF16)
        q_out[:, c + 128:c + 256] = (q[:, c + 128:c + 256] * t * scale).astype(BF16)
    nkv = _rms(ckv_ref[...].astype(F32), gkv_ref[...]).astype(BF16)
    kv = jnp.dot(nkv, wukv_ref[...], preferred_element_type=F32)
    kr = kr_ref[...].astype(F32) * t
    kr2 = (kr + pltpu.roll(kr, MLA_ROPE, 1)).astype(BF16)
    for h in range(MLA_HEADS):
        c = 256 * h
        k_out[:, c:c + 128] = kv[:, c:c + 128].astype(BF16)
        k_out[:, c + 128:c + 256] = kr2
        v_out[:, 128 * h:128 * h + 128] = kv[:, c + 128:c + 256].astype(BF16)


def _mla_proj(z, rope_t, gq, gkv, wuq, wukv, l):
    S = z.shape[0]
    L = gq.shape[0]
    tm = 512
    HW = MLA_HEADS * 256
    return pl.pallas_call(
        _mla_proj_kernel,
        out_shape=(jax.ShapeDtypeStruct((S, HW), BF16),
                   jax.ShapeDtypeStruct((S, HW), BF16),
                   jax.ShapeDtypeStruct((S, MLA_HEADS * MLA_V), BF16)),
        grid=(S // tm,),
        in_specs=[pl.BlockSpec((tm, MLA_Q_LORA), lambda i: (i, Z_CQ // MLA_Q_LORA)),
                  pl.BlockSpec((tm, MLA_KV_LORA), lambda i: (i, Z_CKV // MLA_KV_LORA)),
                  pl.BlockSpec((tm, LANE), lambda i: (i, Z_KR // LANE)),
                  pl.BlockSpec((tm, LANE), lambda i: (i, 0)),
                  pl.BlockSpec((None, 1, MLA_Q_LORA), lambda i: (l, 0, 0)),
                  pl.BlockSpec((None, 1, MLA_KV_LORA), lambda i: (l, 0, 0)),
                  pl.BlockSpec((None, MLA_Q_LORA, HW), lambda i: (l, 0, 0)),
                  pl.BlockSpec((None, MLA_KV_LORA, HW), lambda i: (l, 0, 0))],
        out_specs=(pl.BlockSpec((tm, HW), lambda i: (i, 0)),
                   pl.BlockSpec((tm, HW), lambda i: (i, 0)),
                   pl.BlockSpec((tm, MLA_HEADS * MLA_V), lambda i: (i, 0))),
        compiler_params=_cparams(("arbitrary",)),
        name="mla_proj",
    )(z, z, z, rope_t, gq.reshape(L, 1, -1), gkv.reshape(L, 1, -1), wuq, wukv)


def _qk(q, k):
    return lax.dot_general(q, k, (((1,), (1,)), ((), ())), preferred_element_type=F32)


def _online(s, v, m, l, acc, valid=None):
    m_new = jnp.maximum(m, jnp.max(s, -1, keepdims=True))
    a = jnp.exp(m - m_new)
    p = jnp.exp(s - m_new)
    if valid is not None:
        p = jnp.where(valid, p, 0.0)
    l = a * l + jnp.sum(p, -1, keepdims=True)
    acc = a * acc + jnp.dot(p.astype(BF16), v, preferred_element_type=F32)
    return m_new, l, acc


def _init(rows, dv):
    return (jnp.full((rows, 1), NEG, F32), jnp.zeros((rows, 1), F32), jnp.zeros((rows, dv), F32))


def _kq(k, q):
    return lax.dot_general(k, q, (((1,), (1,)), ((), ())), preferred_element_type=F32)


def _online_t(s, vt, m, l, acc):
    m_new = jnp.maximum(m, jnp.max(s, 0, keepdims=True))
    a = jnp.exp2(m - m_new)
    p = jnp.exp2(s - m_new)
    l = a * l + jnp.sum(p, 0, keepdims=True)
    acc = a * acc + jnp.dot(vt, p.astype(BF16), preferred_element_type=F32)
    return m_new, l, acc


def _init_t(nq, dv):
    return (jnp.full((1, nq), NEG, F32), jnp.zeros((1, nq), F32), jnp.zeros((dv, nq), F32))


def _transpose_into(vt_ref, v_ref):
    for c in range(v_ref.shape[0] // T_CHUNK):
        rows = slice(c * T_CHUNK, (c + 1) * T_CHUNK)
        vt_ref[:, rows] = v_ref[rows, :].astype(F32).T.astype(vt_ref.dtype)


def _put_scores(buf, t, lanes=None):
    s_ref, m_ref = buf
    if lanes is None:
        s_ref[...] = t
        m_ref[...] = jnp.max(t, 0, keepdims=True)
    else:
        s_ref[:, lanes] = t
        m_ref[:, lanes] = jnp.max(t, 0, keepdims=True)


def _sweep(n_pairs, tk, score, vt_ref, carry, buf_a, buf_b):
    def update(j, buf, carry):
        s_ref, mt_ref = buf
        m, l, acc = carry
        m_new = jnp.maximum(m, mt_ref[...])
        a = jnp.exp2(m - m_new)
        p = jnp.exp2(s_ref[...] - m_new)
        l = a * l + jnp.sum(p, 0, keepdims=True)
        vt = vt_ref[:, pl.ds(pl.multiple_of(j * tk, tk), tk)]
        acc = a * acc + jnp.dot(vt, p.astype(BF16), preferred_element_type=F32)
        return m_new, l, acc

    jd = 2 * n_pairs
    score(jd, buf_a, True)
    score(jd + 1, buf_b, True)
    carry = update(jd, buf_a, carry)

    def pair(p, carry):
        j = 2 * p
        score(j, buf_a, False)
        carry = update(jnp.where(p == 0, jd + 1, j - 1), buf_b, carry)
        score(j + 1, buf_b, False)
        return update(j, buf_a, carry)

    carry = lax.fori_loop(0, n_pairs, pair, carry)
    return update(jd - 1 + 2 * (n_pairs == 0).astype(jnp.int32), buf_b, carry)


def _score_bufs(tk, nq):
    return [pltpu.VMEM((tk, nq), F32), pltpu.VMEM((1, nq), F32),
            pltpu.VMEM((tk, nq), F32), pltpu.VMEM((1, nq), F32)]


def _mla_attn_kernel(q_ref, k_ref, v_ref, o_ref, vt_ref, sa_ref, ma_ref, sb_ref, mb_ref, *, tq, tk):
    i = pl.program_id(1)

    @pl.when(i == 0)
    def _():
        _transpose_into(vt_ref, v_ref)

    q = q_ref[...]

    def score(j, buf, masked):
        off = pl.multiple_of(j * tk, tk)
        t = _kq(k_ref[pl.ds(off, tk), :], q)
        if masked:
            kidx = off + lax.broadcasted_iota(jnp.int32, (tk, tq), 0)
            qidx = i * tq + lax.broadcasted_iota(jnp.int32, (tk, tq), 1)
            t = jnp.where(kidx <= qidx, t, NEG)
        _put_scores(buf, t)

    _, l, acc = _sweep(i, tk, score, vt_ref, _init_t(tq, MLA_V), (sa_ref, ma_ref), (sb_ref, mb_ref))
    o_ref[...] = (acc / l).T.astype(o_ref.dtype)


def _mla_attn(qm, km, vm):
    S = qm.shape[0]
    tq, tk = 1024, 512
    return pl.pallas_call(
        functools.partial(_mla_attn_kernel, tq=tq, tk=tk),
        out_shape=jax.ShapeDtypeStruct((S, MLA_HEADS * MLA_V), BF16),
        grid=(MLA_HEADS, S // tq),
        in_specs=[pl.BlockSpec((tq, 256), lambda h, i: (i, h)),
                  pl.BlockSpec((S, 256), lambda h, i: (0, h)),
                  pl.BlockSpec((S, MLA_V), lambda h, i: (0, h))],
        out_specs=pl.BlockSpec((tq, MLA_V), lambda h, i: (i, h)),
        scratch_shapes=[pltpu.VMEM((MLA_V, S), BF16)] + _score_bufs(tk, tq),
        compiler_params=_cparams(("arbitrary", "arbitrary")),
        name="mla_attn",
    )(qm, km, vm)


def _alibi_features(pkb, coef):
    hi, mid, lo = _split3(coef * pkb)
    lane = lax.broadcasted_iota(jnp.int32, hi.shape, 1)
    zero = jnp.zeros_like(hi)
    return jnp.where(lane == 0, hi, jnp.where(lane == 1, mid, jnp.where(lane == 2, lo, zero)))


def _alibi_key_features(kf_ref, pkb_ref, coef):
    for c in range(pkb_ref.shape[0] // T_CHUNK):
        rows = slice(c * T_CHUNK, (c + 1) * T_CHUNK)
        kf_ref[rows, :] = _alibi_features(pkb_ref[rows, :], coef)


def _diff_attn_kernel(sc_ref, q_ref, k_ref, v_ref, pkb_ref, lam_ref, g_ref, o_ref,
                      vt_ref, kf_ref, sa_ref, ma_ref, sb_ref, mb_ref, *, tq, tk):
    h = pl.program_id(0)
    i = pl.program_id(1)

    @pl.when(i == 0)
    def _():
        _transpose_into(vt_ref, v_ref)
        _alibi_key_features(kf_ref, pkb_ref, sc_ref[h])

    lam_init = sc_ref[DIFF_HEADS]
    qs = (q_ref[...].astype(F32) * (LOG2E * DIFF_QK ** -0.5)).astype(BF16)
    lane = lax.broadcasted_iota(jnp.int32, qs.shape, 1)
    zero = jnp.zeros_like(qs)
    ones = jnp.where(lane < N_SPLIT, 1.0, 0.0).astype(BF16)
    q2 = jnp.concatenate([jnp.concatenate([jnp.where(lane < DIFF_QK, qs, zero), ones], 1),
                          jnp.concatenate([jnp.where(lane >= DIFF_QK, qs, zero), ones], 1)], 0)

    def score(j, buf, masked):
        off = pl.multiple_of(j * tk, tk)
        kx = jnp.concatenate([k_ref[pl.ds(off, tk), :], kf_ref[pl.ds(off, tk), :]], 1)
        t = _kq(kx, q2)
        if masked:
            kidx = off + lax.broadcasted_iota(jnp.int32, (tk, tq), 0)
            qidx = i * tq + lax.broadcasted_iota(jnp.int32, (tk, tq), 1)
            ok = kidx <= qidx
            for c in range(2):
                lanes = slice(c * tq, (c + 1) * tq)
                _put_scores(buf, jnp.where(ok, t[:, lanes], NEG), lanes)
        else:
            _put_scores(buf, t)

    _, l, acc = _sweep(i, tk, score, vt_ref, _init_t(2 * tq, DIFF_V), (sa_ref, ma_ref), (sb_ref, mb_ref))
    on = acc / l
    lp = lam_ref[...]
    lam = (jnp.exp(jnp.sum(lp[0:1] * lp[1:2], -1, keepdims=True))
           - jnp.exp(jnp.sum(lp[2:3] * lp[3:4], -1, keepdims=True)) + lam_init)
    o = on[:, :tq] - lam * on[:, tq:]
    o = o * lax.rsqrt(jnp.mean(o * o, 0, keepdims=True) + 1e-6)
    o_ref[...] = (o.T * g_ref[...] * (1.0 - lam_init)).astype(o_ref.dtype)


def _diff_attn(z, pos_b, scal, lam_p, g_sub, l):
    S = z.shape[0]
    L = g_sub.shape[0]
    tq, tk = 1024, 512
    return pl.pallas_call(
        functools.partial(_diff_attn_kernel, tq=tq, tk=tk),
        out_shape=jax.ShapeDtypeStruct((S, DIFF_HEADS * DIFF_V), BF16),
        grid=(DIFF_HEADS, S // tq),
        in_specs=[pl.BlockSpec(memory_space=pltpu.SMEM),
                  pl.BlockSpec((tq, LANE), lambda h, i: (i, Z_DQ // LANE + h)),
                  pl.BlockSpec((S, LANE), lambda h, i: (0, Z_DK // LANE + h)),
                  pl.BlockSpec((S, LANE), lambda h, i: (0, Z_DV // LANE + h)),
                  pl.BlockSpec((S, LANE), lambda h, i: (0, 0)),
                  pl.BlockSpec((None, 4, DIFF_QK), lambda h, i: (l, 0, 0)),
                  pl.BlockSpec((None, 1, DIFF_V), lambda h, i: (l, 0, 0))],
        out_specs=pl.BlockSpec((tq, DIFF_V), lambda h, i: (i, h)),
        scratch_shapes=[pltpu.VMEM((DIFF_V, S), BF16), pltpu.VMEM((S, LANE), BF16)]
        + _score_bufs(tk, 2 * tq),
        compiler_params=_cparams(("arbitrary", "arbitrary")),
        name="diff_attn",
    )(scal, z, z, z, pos_b, lam_p, g_sub.reshape(L, 1, -1))


def _compress_kernel(x_ref, w1_ref, b1_ref, w2_ref, o_ref):
    x = x_ref[...]
    half = CMP_STRIDE * NSA_DH
    a = jnp.dot(x, w1_ref[0:half, :].astype(BF16), preferred_element_type=F32)
    b = jnp.dot(x, w1_ref[half:2 * half, :].astype(BF16), preferred_element_type=F32)
    n = x.shape[0]
    pre = a + pltpu.roll(b, n - 1, 0) + b1_ref[...]
    hdn = 0.5 * pre * (1.0 + jnp.tanh(0.7978845608028654 * (pre + 0.044715 * pre * pre * pre)))
    o_ref[...] = jnp.dot(hdn.astype(BF16), w2_ref[...].astype(BF16),
                         preferred_element_type=F32).astype(o_ref.dtype)


def _compress(xc, w1, b1, w2, l):
    _, nb, kk = xc.shape
    L = w1.shape[0]
    return pl.pallas_call(
        _compress_kernel,
        out_shape=jax.ShapeDtypeStruct((4, nb, NSA_DH), BF16),
        grid=(4,),
        in_specs=[pl.BlockSpec((None, nb, kk), lambda c: (c, 0, 0)),
                  pl.BlockSpec((None, None, 2 * kk, NSA_DH), lambda c: (l, c // 2, 0, 0)),
                  pl.BlockSpec((None, None, 1, NSA_DH), lambda c: (l, c // 2, 0, 0)),
                  pl.BlockSpec((None, None, NSA_DH, NSA_DH), lambda c: (l, c // 2, 0, 0))],
        out_specs=pl.BlockSpec((None, nb, NSA_DH), lambda c: (c, 0, 0)),
        compiler_params=_cparams(("arbitrary",)),
        name="nsa_compress",
    )(xc, w1, b1.reshape(L, 2, 1, NSA_DH), w2)


def _split3(x):
    hi = x.astype(BF16)
    r = x - hi.astype(F32)
    mid = r.astype(BF16)
    lo = (r - mid.astype(F32)).astype(BF16)
    return hi, mid, lo


def _cmp_sel_kernel(sc_ref, q_ref, kc_ref, vc_ref, pcb_ref, o_ref, selt_ref, *, tq, n_slc):
    g = pl.program_id(0)
    i = pl.program_id(1)
    nc = kc_ref.shape[0]
    q = (q_ref[...].astype(F32) * (LOG2E * NSA_DH ** -0.5)).astype(BF16)
    lane = lax.broadcasted_iota(jnp.int32, (tq, NSA_DH), 1)
    q4 = jnp.concatenate(
        [jnp.concatenate([q[:, r * NSA_DH:(r + 1) * NSA_DH],
                          jnp.where(lane < N_SPLIT, sc_ref[g * NSA_REP + r], 0.0).astype(BF16)], 1)
         for r in range(NSA_REP)], 0)
    kx = jnp.concatenate([kc_ref[...], _alibi_features(pcb_ref[...], LOG2E)], 1)
    t = _kq(kx, q4)
    vct = vc_ref[...].astype(F32).T.astype(BF16)
    qidx = i * tq + lax.broadcasted_iota(jnp.int32, (nc, tq), 1)
    cmp_end = lax.broadcasted_iota(jnp.int32, (nc, tq), 0) * CMP_STRIDE + (CMP_BLOCK - 1)
    vis = cmp_end <= qidx
    p_sum = jnp.zeros((nc, tq), F32)
    for r in range(NSA_REP):
        s = jnp.where(vis, t[:, r * tq:(r + 1) * tq], NEG)
        e = jnp.where(vis, jnp.exp2(s - jnp.max(s, 0, keepdims=True)), 0.0)
        d = jnp.sum(e, 0, keepdims=True)
        p = e * (1.0 / jnp.where(d > 0.0, d, 1.0))
        p_sum = p_sum + p
        o = jnp.dot(vct, p.astype(BF16), preferred_element_type=F32)
        o_ref[:, r * NSA_DH:(r + 1) * NSA_DH] = o.T.astype(o_ref.dtype)
    jj = lax.broadcasted_iota(jnp.int32, (n_slc, nc), 0)
    nn = lax.broadcasted_iota(jnp.int32, (n_slc, nc), 1)
    lo_n = jj * (SLC_BLOCK // CMP_STRIDE) - 1
    ovl = jnp.where((nn >= lo_n) & (nn <= lo_n + 4), 1.0, 0.0).astype(BF16)
    imp = sum(jnp.dot(ovl, part, preferred_element_type=F32) for part in _split3(p_sum))
    blk = lax.broadcasted_iota(jnp.int32, (n_slc, tq), 0).astype(F32)
    cur = lax.shift_right_logical(i * tq + lax.broadcasted_iota(jnp.int32, (n_slc, tq), 1),
                                  int(math.log2(SLC_BLOCK))).astype(F32)
    forced = (blk == 0.0) | (blk == cur) | (blk == cur - 1.0)
    causal = blk <= cur
    score = jnp.where(forced, 3e38, jnp.where(causal, imp, -1.0))
    for _ in range(min(N_SELECT, n_slc)):
        mx = jnp.max(score, 0, keepdims=True)
        first = jnp.min(jnp.where(score == mx, blk, float(n_slc)), 0, keepdims=True)
        score = jnp.where(blk == first, PICKED, score)
    selt_ref[...] = jnp.where(causal & (score == PICKED), 1.0, 0.0).astype(selt_ref.dtype)


def _cmp_sel(z, cmp_kv, pos_cmp_b, scal):
    S = z.shape[0]
    nc = cmp_kv.shape[1]
    n_slc = S // SLC_BLOCK
    tq = 512
    gw = NSA_REP * NSA_DH
    return pl.pallas_call(
        functools.partial(_cmp_sel_kernel, tq=tq, n_slc=n_slc),
        out_shape=(jax.ShapeDtypeStruct((S, NSA_HEADS * NSA_DH), BF16),
                   jax.ShapeDtypeStruct((NSA_KV_HEADS, n_slc, S), BF16)),
        grid=(NSA_KV_HEADS, S // tq),
        in_specs=[pl.BlockSpec(memory_space=pltpu.SMEM),
                  pl.BlockSpec((tq, gw), lambda g, i: (i, Z_NQ // gw + g)),
                  pl.BlockSpec((None, nc, NSA_DH), lambda g, i: (g, 0, 0)),
                  pl.BlockSpec((None, nc, NSA_DH), lambda g, i: (2 + g, 0, 0)),
                  pl.BlockSpec((nc, LANE), lambda g, i: (0, 0))],
        out_specs=(pl.BlockSpec((tq, gw), lambda g, i: (i, g)),
                   pl.BlockSpec((None, n_slc, tq), lambda g, i: (g, 0, i))),
        compiler_params=_cparams(("arbitrary", "arbitrary")),
        name="nsa_cmp_select",
    )(scal, z, cmp_kv, cmp_kv, pos_cmp_b)


def _slc_attn_kernel(sc_ref, q_ref, k_ref, v_ref, sel_ref, pkb_ref, o_ref,
                     vt_ref, kf_ref, sa_ref, ma_ref, sb_ref, mb_ref, *, tq, tk):
    g = pl.program_id(0)
    i = pl.program_id(1)

    @pl.when(i == 0)
    def _():
        _transpose_into(vt_ref, v_ref)
        _alibi_key_features(kf_ref, pkb_ref, LOG2E)

    n_slc = sel_ref.shape[0]
    q = (q_ref[...].astype(F32) * (LOG2E * NSA_DH ** -0.5)).astype(BF16)
    lane = lax.broadcasted_iota(jnp.int32, (tq, NSA_DH), 1)
    q4 = jnp.concatenate(
        [jnp.concatenate([q[:, r * NSA_DH:(r + 1) * NSA_DH],
                          jnp.where(lane < N_SPLIT, sc_ref[g * NSA_REP + r], 0.0).astype(BF16)], 1)
         for r in range(NSA_REP)], 0)
    sel = sel_ref[...]
    blk = lax.broadcasted_iota(jnp.int32, (tk, n_slc), 1)
    kblk0 = lax.shift_right_logical(lax.broadcasted_iota(jnp.int32, (tk, n_slc), 0),
                                    int(math.log2(SLC_BLOCK)))

    def score(j, buf, masked):
        off = pl.multiple_of(j * tk, tk)
        kx = jnp.concatenate([k_ref[pl.ds(off, tk), :], kf_ref[pl.ds(off, tk), :]], 1)
        t = _kq(kx, q4)
        expand = jnp.where(blk == kblk0 + j * (tk // SLC_BLOCK), 1.0, 0.0).astype(BF16)
        ok = jnp.dot(expand, sel, preferred_element_type=F32) > 0.5
        if masked:
            kidx = off + lax.broadcasted_iota(jnp.int32, (tk, tq), 0)
            qidx = i * tq + lax.broadcasted_iota(jnp.int32, (tk, tq), 1)
            ok = ok & (kidx <= qidx)
        for r in range(NSA_REP):
            lanes = slice(r * tq, (r + 1) * tq)
            _put_scores(buf, jnp.where(ok, t[:, lanes], NEG), lanes)

    _, l, acc = _sweep(i, tk, score, vt_ref, _init_t(NSA_REP * tq, NSA_DH), (sa_ref, ma_ref), (sb_ref, mb_ref))
    on = acc / l
    for r in range(NSA_REP):
        o_ref[:, r * NSA_DH:(r + 1) * NSA_DH] = on[:, r * tq:(r + 1) * tq].T.astype(o_ref.dtype)


def _slc_attn(z, sel, pos_b, scal):
    S = z.shape[0]
    n_slc = S // SLC_BLOCK
    tq, tk = 512, 256
    gw = NSA_REP * NSA_DH
    kcol = Z_NKV // LANE + 4
    return pl.pallas_call(
        functools.partial(_slc_attn_kernel, tq=tq, tk=tk),
        out_shape=jax.ShapeDtypeStruct((S, NSA_HEADS * NSA_DH), BF16),
        grid=(NSA_KV_HEADS, S // tq),
        in_specs=[pl.BlockSpec(memory_space=pltpu.SMEM),
                  pl.BlockSpec((tq, gw), lambda g, i: (i, Z_NQ // gw + g)),
                  pl.BlockSpec((S, NSA_DH), lambda g, i: (0, kcol + g)),
                  pl.BlockSpec((S, NSA_DH), lambda g, i: (0, kcol + 2 + g)),
                  pl.BlockSpec((None, n_slc, tq), lambda g, i: (g, 0, i)),
                  pl.BlockSpec((S, LANE), lambda g, i: (0, 0))],
        out_specs=pl.BlockSpec((tq, gw), lambda g, i: (i, g)),
        scratch_shapes=[pltpu.VMEM((NSA_DH, S), BF16), pltpu.VMEM((S, LANE), BF16)]
        + _score_bufs(tk, NSA_REP * tq),
        compiler_params=_cparams(("arbitrary", "arbitrary")),
        name="nsa_slc_attn",
    )(scal, z, z, z, sel, pos_b)


def _win_attn_kernel(sc_ref, q_ref, k_ref, v_ref, pkb_ref, ocmp_ref, oslc_ref, gate_ref,
                     o_ref, vt_ref, kf_ref, *, t):
    g = pl.program_id(0)
    i = pl.program_id(1)

    @pl.when(i == 0)
    def _():
        _transpose_into(vt_ref, v_ref)
        _alibi_key_features(kf_ref, pkb_ref, LOG2E)

    span = t + WINDOW
    n_t = k_ref.shape[0] // t
    start = pl.multiple_of(jnp.clip(i - WINDOW // t, 0, n_t - span // t) * t, t)
    q = (q_ref[...].astype(F32) * (LOG2E * NSA_DH ** -0.5)).astype(BF16)
    lane = lax.broadcasted_iota(jnp.int32, (t, NSA_DH), 1)
    q4 = jnp.concatenate(
        [jnp.concatenate([q[:, r * NSA_DH:(r + 1) * NSA_DH],
                          jnp.where(lane < N_SPLIT, sc_ref[g * NSA_REP + r], 0.0).astype(BF16)], 1)
         for r in range(NSA_REP)], 0)
    kx = jnp.concatenate([k_ref[pl.ds(start, span), :], kf_ref[pl.ds(start, span), :]], 1)
    s = _kq(kx, q4)
    dif = (i * t + lax.broadcasted_iota(jnp.int32, (span, t), 1)
           - start - lax.broadcasted_iota(jnp.int32, (span, t), 0))
    ok = (dif >= 0) & (dif < WINDOW)
    vt = vt_ref[:, pl.ds(start, span)]
    gates = jax.nn.sigmoid(gate_ref[...].astype(F32))
    for r in range(NSA_REP):
        sr = jnp.where(ok, s[:, r * t:(r + 1) * t], NEG)
        e = jnp.exp2(sr - jnp.max(sr, 0, keepdims=True))
        d = jnp.sum(e, 0, keepdims=True)
        o_win = (jnp.dot(vt, e.astype(BF16), preferred_element_type=F32) * (1.0 / d)).T
        sl = slice(r * NSA_DH, (r + 1) * NSA_DH)

        def gcol(b):
            c0, c1 = 3 * r + b, 3 * (NSA_REP + r) + b
            return jnp.where(g == 0, gates[:, c0:c0 + 1], gates[:, c1:c1 + 1])

        o = (gcol(0) * ocmp_ref[:, sl].astype(F32) + gcol(1) * oslc_ref[:, sl].astype(F32)
             + gcol(2) * o_win)
        o_ref[:, sl] = o.astype(o_ref.dtype)


def _win_attn(z, o_cmp, o_slc, pos_b, scal):
    S = z.shape[0]
    t = 256
    gw = NSA_REP * NSA_DH
    kcol = Z_NKV // LANE + 8
    return pl.pallas_call(
        functools.partial(_win_attn_kernel, t=t),
        out_shape=jax.ShapeDtypeStruct((S, NSA_HEADS * NSA_DH), BF16),
        grid=(NSA_KV_HEADS, S // t),
        in_specs=[pl.BlockSpec(memory_space=pltpu.SMEM),
                  pl.BlockSpec((t, gw), lambda g, i: (i, Z_NQ // gw + g)),
                  pl.BlockSpec((S, NSA_DH), lambda g, i: (0, kcol + g)),
                  pl.BlockSpec((S, NSA_DH), lambda g, i: (0, kcol + 2 + g)),
                  pl.BlockSpec((S, LANE), lambda g, i: (0, 0)),
                  pl.BlockSpec((t, gw), lambda g, i: (i, g)),
                  pl.BlockSpec((t, gw), lambda g, i: (i, g)),
                  pl.BlockSpec((t, LANE), lambda g, i: (i, Z_NG // LANE))],
        out_specs=pl.BlockSpec((t, gw), lambda g, i: (i, g)),
        scratch_shapes=[pltpu.VMEM((NSA_DH, S), BF16), pltpu.VMEM((S, LANE), BF16)],
        compiler_params=_cparams(("arbitrary", "arbitrary")),
        name="nsa_win_attn",
    )(scal, z, z, z, pos_b, o_cmp, o_slc, z)


def kernel(x, c, positions, w_ada, b_ada, w_in, mla_q_norm, mla_kv_norm, mla_w_uq, mla_w_ukv, diff_lambda, diff_subln, nsa_cmp_w1, nsa_cmp_b1, nsa_cmp_w2, w_branch, w_out, ln1_g, ln1_b, w_ff1, w_ff2, ln2_g, ln2_b):
    B, S, D = x.shape
    L = w_ada.shape[0]
    assert B == 1 and D == D_MODEL and S % 512 == 0 and S // SLC_BLOCK >= 3

    w_in_x = _relayout_w_in(w_in)
    w_uq_x = _relayout_w_uq(mla_w_uq)
    w_ukv_b = mla_w_ukv.astype(BF16)
    w_out_b = w_out.astype(BF16)
    w_ff2_b = w_ff2.astype(BF16)

    slopes = 2.0 ** (-8.0 * np.arange(1, N_ALIBI_HEADS + 1, dtype=np.float64) / N_ALIBI_HEADS)
    s_diff, s_nsa = slopes[0::2], slopes[1::2]
    nsa_scal = jnp.asarray(s_nsa, F32)

    pos_f = positions.astype(F32)
    pos_col = pos_f.reshape(S, 1)
    pos_b = jnp.broadcast_to(pos_col, (S, LANE))
    pos_cmp_b = jnp.broadcast_to(jnp.pad(pos_f[0, CMP_BLOCK - 1::CMP_STRIDE], (0, 1))[:, None],
                                 (S // CMP_STRIDE, LANE))

    mod = _ada(c, w_ada, b_ada)
    rope_t = _rope_table(pos_col)
    xs = x.reshape(S, D)
    h = _lnmod(xs, mod, 0, 0, 1)

    for l in range(L):
        lam_init = 0.8 - 0.6 * math.exp(-0.3 * l)
        diff_scal = jnp.asarray(np.concatenate([s_diff * LOG2E, [lam_init]]), F32)

        z = _matmul(h, w_in_x, l, tm=1024, tn=1280, relu2=False, name="in_proj", w_transposed=True)

        qm, km, vm = _mla_proj(z, rope_t, mla_q_norm, mla_kv_norm, w_uq_x, w_ukv_b, l)
        o_mla = _mla_attn(qm, km, vm)

        o_diff = _diff_attn(z, pos_b, diff_scal, diff_lambda, diff_subln, l)

        xc = z[:, Z_NKV:Z_NKV + 4 * NSA_DH].reshape(S // CMP_STRIDE, CMP_STRIDE, 4, NSA_DH)
        xc = xc.transpose(2, 0, 1, 3).reshape(4, S // CMP_STRIDE, CMP_STRIDE * NSA_DH)
        cmp_kv = _compress(xc, nsa_cmp_w1, nsa_cmp_b1, nsa_cmp_w2, l)
        o_cmp, sel = _cmp_sel(z, cmp_kv, pos_cmp_b, nsa_scal)
        o_slc = _slc_attn(z, sel, pos_b, nsa_scal)
        o_nsa = _win_attn(z, o_cmp, o_slc, pos_b, nsa_scal)

        merged = _merge(o_mla, o_diff, o_nsa, w_branch, z, l)
        xs, h = _matmul_res_ln(merged, w_out_b, xs, mod, ln1_g, ln1_b, l, 2, ALPHA, "out_proj_ln",
                               next_mod=(l, 3, 4))
        u = _matmul(h, w_ff1, l, tm=1024, tn=1024, relu2=True, name="ff1")
        if l + 1 < L:
            xs, h = _matmul_res_ln(u, w_ff2_b, xs, mod, ln2_g, ln2_b, l, 5, ALPHA, "ff2_ln",
                                   next_mod=(l + 1, 0, 1))
        else:
            xs = _matmul_res_ln(u, w_ff2_b, xs, mod, ln2_g, ln2_b, l, 5, ALPHA, "ff2_ln")

    return xs.reshape(B, S, D)
```

```python
import functools
import math

import numpy as np
import jax
import jax.numpy as jnp
from jax import lax
from jax.experimental import pallas as pl
from jax.experimental.pallas import tpu as pltpu

F32 = jnp.float32
BF16 = jnp.bfloat16

D_MODEL = 2048
MLA_HEADS = 8
MLA_Q_LORA = 512
MLA_KV_LORA = 256
MLA_NOPE = 128
MLA_ROPE = 64
MLA_V = 128
ROPE_THETA = 10000.0
DIFF_HEADS = 8
DIFF_QK = 64
DIFF_V = 128
NSA_HEADS = 8
NSA_KV_HEADS = 2
NSA_REP = NSA_HEADS // NSA_KV_HEADS
NSA_DH = 128
CMP_BLOCK = 32
CMP_STRIDE = 16
SLC_BLOCK = 64
N_SELECT = 16
WINDOW = 512
N_ALIBI_HEADS = DIFF_HEADS + NSA_HEADS
N_BRANCH = 3
BRANCH_WIDTH = 1024
D_FF = 4 * D_MODEL
N_MOD = 6
DEPTH = 4
ALPHA = (2.0 * DEPTH) ** 0.25

LANE = 128
NEG = -1e30
LOG2E = math.log2(math.e)
T_CHUNK = 512
N_SPLIT = 3
PICKED = -2.0
SLC_TQ, SLC_TK = 512, 256
VMEM_LIMIT = 48 * 1024 * 1024

Z_CQ = 0
Z_DQ = 512
Z_DK = 1536
Z_DV = 2560
Z_NQ = 3584
Z_NKV = 4608
Z_MG = 6144
Z_CKV = 12288
Z_KR = 12544
Z_NG = 12672
Z_W = 12800


def _cparams(sem, vmem=VMEM_LIMIT):
    return pltpu.CompilerParams(dimension_semantics=sem, vmem_limit_bytes=vmem)


def _rot_half(w):
    half = MLA_ROPE // 2
    return jnp.concatenate([-w[..., half:], w[..., :half]], axis=-1)


def _relayout_w_in_kernel(wt_ref, o_ref):
    o = np.cumsum([0, 512, 256, 64, 1024, 1024, 1024, 1024, 1536, 24, N_BRANCH * D_MODEL])
    x = wt_ref[...]
    cq, ckv, kr, dq, dk, dv, nq, nkv, ng, mg = (x[o[k]:o[k + 1], :] for k in range(10))
    half = MLA_ROPE // 2
    kr_rot = jnp.concatenate([-kr[half:, :], kr[:half, :]], axis=0)
    pad = jnp.zeros((Z_W - Z_NG - ng.shape[0], x.shape[1]), x.dtype)
    o_ref[...] = jnp.concatenate([cq, dq, dk, dv, nq, nkv, mg, ckv, kr, kr_rot, ng, pad],
                                 axis=0).astype(o_ref.dtype)


def _relayout_w_in(w_in):
    L, D, N = w_in.shape
    tc = 256
    return pl.pallas_call(
        _relayout_w_in_kernel,
        out_shape=jax.ShapeDtypeStruct((L, Z_W, D), BF16),
        grid=(L, D // tc),
        in_specs=[pl.BlockSpec((None, N, tc), lambda l, i: (l, 0, i))],
        out_specs=pl.BlockSpec((None, Z_W, tc), lambda l, i: (l, 0, i)),
        compiler_params=_cparams(("arbitrary", "arbitrary")),
        name="w_in_relayout",
    )(jnp.swapaxes(w_in, 1, 2))


def _relayout_w_uq(w_uq):
    L, K, _ = w_uq.shape
    w = w_uq.reshape(L, K, MLA_HEADS, MLA_NOPE + MLA_ROPE)
    rope = w[..., MLA_NOPE:]
    return jnp.concatenate([w, _rot_half(rope)], axis=-1).reshape(L, K, MLA_HEADS * 256).astype(BF16)


def _ada_kernel(c_ref, w_ref, b_ref, o_ref):
    c = c_ref[...]
    ca = c * jax.nn.sigmoid(c)
    ca8 = jnp.broadcast_to(ca, (8, c.shape[1])).astype(BF16)
    r = jnp.dot(ca8, w_ref[...].astype(BF16), preferred_element_type=F32)
    o_ref[...] = r[0:1, :] + b_ref[...]


def _ada(c, w_ada, b_ada):
    L, D, N = w_ada.shape
    tn = 1536
    return pl.pallas_call(
        _ada_kernel,
        out_shape=jax.ShapeDtypeStruct((L, 1, N), F32),
        grid=(L, N // tn),
        in_specs=[pl.BlockSpec((1, D), lambda l, j: (0, 0)),
                  pl.BlockSpec((None, D, tn), lambda l, j: (l, 0, j)),
                  pl.BlockSpec((None, 1, tn), lambda l, j: (l, 0, j))],
        out_specs=pl.BlockSpec((None, 1, tn), lambda l, j: (l, 0, j)),
        compiler_params=_cparams(("arbitrary", "arbitrary")),
        name="ada_mod",
    )(c, w_ada, b_ada.reshape(L, 1, N))


def _lnmod_kernel(x_ref, sh_ref, sc_ref, o_ref):
    x = x_ref[...]
    mu = jnp.mean(x, -1, keepdims=True)
    xc = x - mu
    var = jnp.mean(xc * xc, -1, keepdims=True)
    y = xc * lax.rsqrt(var + 1e-5)
    o_ref[...] = (y * (1.0 + sc_ref[...]) + sh_ref[...]).astype(o_ref.dtype)


def _lnmod(x, mod, l, k_shift, k_scale):
    S, D = x.shape
    tm = 512
    return pl.pallas_call(
        _lnmod_kernel,
        out_shape=jax.ShapeDtypeStruct((S, D), BF16),
        grid=(S // tm,),
        in_specs=[pl.BlockSpec((tm, D), lambda i: (i, 0)),
                  pl.BlockSpec((None, 1, D), lambda i: (l, 0, k_shift)),
                  pl.BlockSpec((None, 1, D), lambda i: (l, 0, k_scale))],
        out_specs=pl.BlockSpec((tm, D), lambda i: (i, 0)),
        compiler_params=_cparams(("arbitrary",)),
        name="ln_modulate",
    )(x, mod, mod)


def _mm_kernel(x_ref, w_ref, o_ref, *, relu2, w_transposed):
    w = w_ref[...].astype(BF16)
    contract = (((1,), (1,)), ((), ())) if w_transposed else (((1,), (0,)), ((), ()))
    r = lax.dot_general(x_ref[...], w, contract, preferred_element_type=F32)
    if relu2:
        r = jnp.square(jnp.maximum(r, 0.0))
    o_ref[...] = r.astype(o_ref.dtype)


def _matmul(x, w, l, *, tm, tn, relu2, name, w_transposed=False):
    S, K = x.shape
    if w_transposed:
        N = w.shape[1]
        w_spec = pl.BlockSpec((None, tn, K), lambda i, j: (l, j, 0))
    else:
        N = w.shape[2]
        w_spec = pl.BlockSpec((None, K, tn), lambda i, j: (l, 0, j))
    return pl.pallas_call(
        functools.partial(_mm_kernel, relu2=relu2, w_transposed=w_transposed),
        out_shape=jax.ShapeDtypeStruct((S, N), BF16),
        grid=(S // tm, N // tn),
        in_specs=[pl.BlockSpec((tm, K), lambda i, j: (i, 0)), w_spec],
        out_specs=pl.BlockSpec((tm, tn), lambda i, j: (i, j)),
        compiler_params=_cparams(("arbitrary", "arbitrary")),
        name=name,
    )(x, w)


def _ln_plain(y):
    mu = jnp.mean(y, -1, keepdims=True)
    yc = y - mu
    return yc * lax.rsqrt(jnp.mean(yc * yc, -1, keepdims=True) + 1e-5)


def _mm_res_ln_kernel(a_ref, w_ref, x_ref, gate_ref, lng_ref, lnb_ref, *rest, nk, alpha, with_next):
    o_ref = rest[2] if with_next else rest[0]
    k = pl.program_id(1)
    part = jnp.dot(a_ref[...], w_ref[...], preferred_element_type=F32)

    @pl.when(k == 0)
    def _():
        o_ref[...] = part

    @pl.when(k > 0)
    def _():
        o_ref[...] += part

    @pl.when(k == nk - 1)
    def _():
        x_new = _ln_plain(alpha * x_ref[...] + gate_ref[...] * o_ref[...]) * lng_ref[...] + lnb_ref[...]
        o_ref[...] = x_new
        if with_next:
            sh_ref, sc_ref, _, h_ref = rest
            h_ref[...] = (_ln_plain(x_new) * (1.0 + sc_ref[...]) + sh_ref[...]).astype(h_ref.dtype)


def _matmul_res_ln(a, w, x, mod, lng, lnb, l, k_gate, alpha, name, next_mod=None):
    S, K = a.shape
    D = x.shape[1]
    L = lng.shape[0]
    tm, tk = 512, 2048
    nk = K // tk
    vmem = VMEM_LIMIT + (2 * tm * D * 2 if next_mod is not None else 0)
    vec =lambda ll, kk: pl.BlockSpec((None, 1, D), lambda i, k: (ll, 0, kk))
    row = pl.BlockSpec((tm, D), lambda i, k: (i, 0))
    in_specs = [pl.BlockSpec((tm, tk), lambda i, k: (i, k)),
                pl.BlockSpec((None, tk, D), lambda i, k: (l, k, 0)),
                row, vec(l, k_gate), vec(l, 0), vec(l, 0)]
    args = [a, w, x, mod, lng.reshape(L, 1, D), lnb.reshape(L, 1, D)]
    out_shape, out_specs = jax.ShapeDtypeStruct((S, D), F32), row
    if next_mod is not None:
        in_specs += [vec(next_mod[0], next_mod[1]), vec(next_mod[0], next_mod[2])]
        args += [mod, mod]
        out_shape, out_specs = (out_shape, jax.ShapeDtypeStruct((S, D), BF16)), (row, row)
    return pl.pallas_call(
        functools.partial(_mm_res_ln_kernel, nk=nk, alpha=alpha, with_next=next_mod is not None),
        out_shape=out_shape,
        grid=(S // tm, nk),
        in_specs=in_specs,
        out_specs=out_specs,
        compiler_params=_cparams(("arbitrary", "arbitrary"), vmem),
        name=name,
    )(*args)


def _merge_kernel(o0_ref, o1_ref, o2_ref, w_ref, g0_ref, g1_ref, g2_ref, out_ref):
    acc = None
    for n, (o_ref, g_ref) in enumerate(((o0_ref, g0_ref), (o1_ref, g1_ref), (o2_ref, g2_ref))):
        t = jax.nn.sigmoid(g_ref[...].astype(F32)) * jnp.dot(o_ref[...], w_ref[n].astype(BF16),
                                                               preferred_element_type=F32)
        acc = t if acc is None else acc + t
    out_ref[...] = acc.astype(out_ref.dtype)


def _merge(o_mla, o_diff, o_nsa, w_branch, z, l):
    S = o_mla.shape[0]
    tm, tn = 1024, 512
    bspec = pl.BlockSpec((tm, BRANCH_WIDTH), lambda i, j: (i, 0))

    def gspec(n):
        return pl.BlockSpec((tm, tn), lambda i, j: (i, (Z_MG + D_MODEL * n) // tn + j))

    return pl.pallas_call(
        _merge_kernel,
        out_shape=jax.ShapeDtypeStruct((S, D_MODEL), BF16),
        grid=(S // tm, D_MODEL // tn),
        in_specs=[bspec, bspec, bspec,
                  pl.BlockSpec((None, N_BRANCH, BRANCH_WIDTH, tn), lambda i, j: (l, 0, 0, j)),
                  gspec(0), gspec(1), gspec(2)],
        out_specs=pl.BlockSpec((tm, tn), lambda i, j: (i, j)),
        compiler_params=_cparams(("arbitrary", "arbitrary")),
        name="branch_merge",
    )(o_mla, o_diff, o_nsa, w_branch, z, z, z)


def _rope_kernel(pos_ref, inv_ref, o_ref):
    ang = pos_ref[...] * inv_ref[...]
    lane = lax.broadcasted_iota(jnp.int32, ang.shape, 1)
    o_ref[...] = jnp.where(lane < MLA_ROPE, jnp.cos(ang), jnp.sin(ang))


def _rope_table(pos_col):
    S = pos_col.shape[0]
    inv = ROPE_THETA ** (-np.arange(0, MLA_ROPE, 2, dtype=np.float64) / MLA_ROPE)
    inv4 = jnp.asarray(np.tile(inv, 4)[None, :], F32)
    tm = 512
    return pl.pallas_call(
        _rope_kernel,
        out_shape=jax.ShapeDtypeStruct((S, LANE), F32),
        grid=(S // tm,),
        in_specs=[pl.BlockSpec((tm, 1), lambda i: (i, 0)),
                  pl.BlockSpec((1, LANE), lambda i: (0, 0))],
        out_specs=pl.BlockSpec((tm, LANE), lambda i: (i, 0)),
        compiler_params=_cparams(("arbitrary",)),
        name="rope_table",
    )(pos_col, inv4)


def _rms(x, g):
    return x * lax.rsqrt(jnp.mean(x * x, -1, keepdims=True) + 1e-6) * g


def _mla_proj_kernel(cq_ref, ckv_ref, kr_ref, t_ref, gq_ref, gkv_ref, wuq_ref, wukv_ref,
                     q_out, k_out, v_out):
    scale = LOG2E * (MLA_NOPE + MLA_ROPE) ** -0.5
    t = t_ref[...]
    nq = _rms(cq_ref[...].astype(F32), gq_ref[...]).astype(BF16)
    q = jnp.dot(nq, wuq_ref[...], preferred_element_type=F32)
    for h in range(MLA_HEADS):
        c = 256 * h
        q_out[:, c:c + 128] = (q[:, c:c + 128] * scale).astype(BF16)
        q_out[:, c + 128:c + 256] = (q[:, c + 128:c + 256] * t * scale).astype(BF16)
    nkv = _rms(ckv_ref[...].astype(F32), gkv_ref[...]).astype(BF16)
    kv = jnp.dot(nkv, wukv_ref[...], preferred_element_type=F32)
    kr = kr_ref[...].astype(F32) * t
    kr2 = (kr + pltpu.roll(kr, MLA_ROPE, 1)).astype(BF16)
    for h in range(MLA_HEADS):
        c = 256 * h
        k_out[:, c:c + 128] = kv[:, c:c + 128].astype(BF16)
        k_out[:, c + 128:c + 256] = kr2
        v_out[:, 128 * h:128 * h + 128] = kv[:, c + 128:c + 256].astype(BF16)


def _mla_proj(z, rope_t, gq, gkv, wuq, wukv, l):
    S = z.shape[0]
    L = gq.shape[0]
    tm = 512
    HW = MLA_HEADS * 256
    return pl.pallas_call(
        _mla_proj_kernel,
        out_shape=(jax.ShapeDtypeStruct((S, HW), BF16),
                   jax.ShapeDtypeStruct((S, HW), BF16),
                   jax.ShapeDtypeStruct((S, MLA_HEADS * MLA_V), BF16)),
        grid=(S // tm,),
        in_specs=[pl.BlockSpec((tm, MLA_Q_LORA), lambda i: (i, Z_CQ // MLA_Q_LORA)),
                  pl.BlockSpec((tm, MLA_KV_LORA), lambda i: (i, Z_CKV // MLA_KV_LORA)),
                  pl.BlockSpec((tm, LANE), lambda i: (i, Z_KR // LANE)),
                  pl.BlockSpec((tm, LANE), lambda i: (i, 0)),
                  pl.BlockSpec((None, 1, MLA_Q_LORA), lambda i: (l, 0, 0)),
                  pl.BlockSpec((None, 1, MLA_KV_LORA), lambda i: (l, 0, 0)),
                  pl.BlockSpec((None, MLA_Q_LORA, HW), lambda i: (l, 0, 0)),
                  pl.BlockSpec((None, MLA_KV_LORA, HW), lambda i: (l, 0, 0))],
        out_specs=(pl.BlockSpec((tm, HW), lambda i: (i, 0)),
                   pl.BlockSpec((tm, HW), lambda i: (i, 0)),
                   pl.BlockSpec((tm, MLA_HEADS * MLA_V), lambda i: (i, 0))),
        compiler_params=_cparams(("arbitrary",)),
        name="mla_proj",
    )(z, z, z, rope_t, gq.reshape(L, 1, -1), gkv.reshape(L, 1, -1), wuq, wukv)


def _qk(q, k):
    return lax.dot_general(q, k, (((1,), (1,)), ((), ())), preferred_element_type=F32)


def _online(s, v, m, l, acc, valid=None):
    m_new = jnp.maximum(m, jnp.max(s, -1, keepdims=True))
    a = jnp.exp(m - m_new)
    p = jnp.exp(s - m_new)
    if valid is not None:
        p = jnp.where(valid, p, 0.0)
    l = a * l + jnp.sum(p, -1, keepdims=True)
    acc = a * acc + jnp.dot(p.astype(BF16), v, preferred_element_type=F32)
    return m_new, l, acc


def _init(rows, dv):
    return (jnp.full((rows, 1), NEG, F32), jnp.zeros((rows, 1), F32), jnp.zeros((rows, dv), F32))


def _kq(k, q):
    return lax.dot_general(k, q, (((1,), (1,)), ((), ())), preferred_element_type=F32)


def _online_t(s, vt, m, l, acc):
    m_new = jnp.maximum(m, jnp.max(s, 0, keepdims=True))
    a = jnp.exp2(m - m_new)
    p = jnp.exp2(s - m_new)
    l = a * l + jnp.sum(p, 0, keepdims=True)
    acc = a * acc + jnp.dot(vt, p.astype(BF16), preferred_element_type=F32)
    return m_new, l, acc


def _init_t(nq, dv):
    return (jnp.full((1, nq), NEG, F32), jnp.zeros((1, nq), F32), jnp.zeros((dv, nq), F32))


def _transpose_into(vt_ref, v_ref):
    for c in range(v_ref.shape[0] // T_CHUNK):
        rows = slice(c * T_CHUNK, (c + 1) * T_CHUNK)
        vt_ref[:, rows] = v_ref[rows, :].astype(F32).T.astype(vt_ref.dtype)


def _put_scores(buf, t, lanes=None):
    s_ref, m_ref = buf
    if lanes is None:
        s_ref[...] = t
        m_ref[...] = jnp.max(t, 0, keepdims=True)
    else:
        s_ref[:, lanes] = t
        m_ref[:, lanes] = jnp.max(t, 0, keepdims=True)


def _sweep(n_pairs, tk, score, vt_ref, carry, buf_a, buf_b, jd=None, tile_of=lambda n: n):
    def update(j, buf, carry):
        s_ref, mt_ref = buf
        m, l, acc = carry
        m_new = jnp.maximum(m, mt_ref[...])
        a = jnp.exp2(m - m_new)
        p = jnp.exp2(s_ref[...] - m_new)
        l = a * l + jnp.sum(p, 0, keepdims=True)
        vt = vt_ref[:, pl.ds(pl.multiple_of(j * tk, tk), tk)]
        acc = a * acc + jnp.dot(vt, p.astype(BF16), preferred_element_type=F32)
        return m_new, l, acc

    if jd is None:
        jd = 2 * n_pairs
    score(jd, buf_a, True)
    score(jd + 1, buf_b, True)
    carry = update(jd, buf_a, carry)

    def pair(p, carry):
        ja, jb = tile_of(2 * p), tile_of(2 * p + 1)
        score(ja, buf_a, False)
        carry = update(jnp.where(p == 0, jd + 1, tile_of(jnp.maximum(2 * p - 1, 0))), buf_b, carry)
        score(jb, buf_b, False)
        return update(ja, buf_a, carry)

    carry = lax.fori_loop(0, n_pairs, pair, carry)
    last = jnp.where(n_pairs == 0, jd + 1, tile_of(jnp.maximum(2 * n_pairs - 1, 0)))
    return update(last, buf_b, carry)


def _score_bufs(tk, nq):
    return [pltpu.VMEM((tk, nq), F32), pltpu.VMEM((1, nq), F32),
            pltpu.VMEM((tk, nq), F32), pltpu.VMEM((1, nq), F32)]


def _mla_attn_kernel(q_ref, k_ref, v_ref, o_ref, vt_ref, sa_ref, ma_ref, sb_ref, mb_ref, *, tq, tk):
    i = pl.program_id(1)

    @pl.when(i == 0)
    def _():
        _transpose_into(vt_ref, v_ref)

    q = q_ref[...]

    def score(j, buf, masked):
        off = pl.multiple_of(j * tk, tk)
        t = _kq(k_ref[pl.ds(off, tk), :], q)
        if masked:
            kidx = off + lax.broadcasted_iota(jnp.int32, (tk, tq), 0)
            qidx = i * tq + lax.broadcasted_iota(jnp.int32, (tk, tq), 1)
            t = jnp.where(kidx <= qidx, t, NEG)
        _put_scores(buf, t)

    _, l, acc = _sweep(i, tk, score, vt_ref, _init_t(tq, MLA_V), (sa_ref, ma_ref), (sb_ref, mb_ref))
    o_ref[...] = (acc / l).T.astype(o_ref.dtype)


def _mla_attn(qm, km, vm):
    S = qm.shape[0]
    tq, tk = 1024, 512
    return pl.pallas_call(
        functools.partial(_mla_attn_kernel, tq=tq, tk=tk),
        out_shape=jax.ShapeDtypeStruct((S, MLA_HEADS * MLA_V), BF16),
        grid=(MLA_HEADS, S // tq),
        in_specs=[pl.BlockSpec((tq, 256), lambda h, i: (i, h)),
                  pl.BlockSpec((S, 256), lambda h, i: (0, h)),
                  pl.BlockSpec((S, MLA_V), lambda h, i: (0, h))],
        out_specs=pl.BlockSpec((tq, MLA_V), lambda h, i: (i, h)),
        scratch_shapes=[pltpu.VMEM((MLA_V, S), BF16)] + _score_bufs(tk, tq),
        compiler_params=_cparams(("arbitrary", "arbitrary")),
        name="mla_attn",
    )(qm, km, vm)


def _alibi_features(pkb, coef):
    hi, mid, lo = _split3(coef * pkb)
    lane = lax.broadcasted_iota(jnp.int32, hi.shape, 1)
    zero = jnp.zeros_like(hi)
    return jnp.where(lane == 0, hi, jnp.where(lane == 1, mid, jnp.where(lane == 2, lo, zero)))


def _alibi_key_features(kf_ref, pkb_ref, coef):
    for c in range(pkb_ref.shape[0] // T_CHUNK):
        rows = slice(c * T_CHUNK, (c + 1) * T_CHUNK)
        kf_ref[rows, :] = _alibi_features(pkb_ref[rows, :], coef)


def _diff_attn_kernel(sc_ref, q_ref, k_ref, v_ref, pkb_ref, lam_ref, g_ref, o_ref,
                      vt_ref, kf_ref, sa_ref, ma_ref, sb_ref, mb_ref, *, tq, tk):
    h = pl.program_id(0)
    i = pl.program_id(1)

    @pl.when(i == 0)
    def _():
        _transpose_into(vt_ref, v_ref)
        _alibi_key_features(kf_ref, pkb_ref, sc_ref[h])

    lam_init = sc_ref[DIFF_HEADS]
    qs = (q_ref[...].astype(F32) * (LOG2E * DIFF_QK ** -0.5)).astype(BF16)
    lane = lax.broadcasted_iota(jnp.int32, qs.shape, 1)
    zero = jnp.zeros_like(qs)
    ones = jnp.where(lane < N_SPLIT, 1.0, 0.0).astype(BF16)
    q2 = jnp.concatenate([jnp.concatenate([jnp.where(lane < DIFF_QK, qs, zero), ones], 1),
                          jnp.concatenate([jnp.where(lane >= DIFF_QK, qs, zero), ones], 1)], 0)

    def score(j, buf, masked):
        off = pl.multiple_of(j * tk, tk)
        kx = jnp.concatenate([k_ref[pl.ds(off, tk), :], kf_ref[pl.ds(off, tk), :]], 1)
        t = _kq(kx, q2)
        if masked:
            kidx = off + lax.broadcasted_iota(jnp.int32, (tk, tq), 0)
            qidx = i * tq + lax.broadcasted_iota(jnp.int32, (tk, tq), 1)
            ok = kidx <= qidx
            for c in range(2):
                lanes = slice(c * tq, (c + 1) * tq)
                _put_scores(buf, jnp.where(ok, t[:, lanes], NEG), lanes)
        else:
            _put_scores(buf, t)

    _, l, acc = _sweep(i, tk, score, vt_ref, _init_t(2 * tq, DIFF_V), (sa_ref, ma_ref), (sb_ref, mb_ref))
    on = acc / l
    lp = lam_ref[...]
    lam = (jnp.exp(jnp.sum(lp[0:1] * lp[1:2], -1, keepdims=True))
           - jnp.exp(jnp.sum(lp[2:3] * lp[3:4], -1, keepdims=True)) + lam_init)
    o = on[:, :tq] - lam * on[:, tq:]
    o = o * lax.rsqrt(jnp.mean(o * o, 0, keepdims=True) + 1e-6)
    o_ref[...] = (o.T * g_ref[...] * (1.0 - lam_init)).astype(o_ref.dtype)


def _diff_attn(z, pos_b, scal, lam_p, g_sub, l):
    S = z.shape[0]
    L = g_sub.shape[0]
    tq, tk = 1024, 512
    return pl.pallas_call(
        functools.partial(_diff_attn_kernel, tq=tq, tk=tk),
        out_shape=jax.ShapeDtypeStruct((S, DIFF_HEADS * DIFF_V), BF16),
        grid=(DIFF_HEADS, S // tq),
        in_specs=[pl.BlockSpec(memory_space=pltpu.SMEM),
                  pl.BlockSpec((tq, LANE), lambda h, i: (i, Z_DQ // LANE + h)),
                  pl.BlockSpec((S, LANE), lambda h, i: (0, Z_DK // LANE + h)),
                  pl.BlockSpec((S, LANE), lambda h, i: (0, Z_DV // LANE + h)),
                  pl.BlockSpec((S, LANE), lambda h, i: (0, 0)),
                  pl.BlockSpec((None, 4, DIFF_QK), lambda h, i: (l, 0, 0)),
                  pl.BlockSpec((None, 1, DIFF_V), lambda h, i: (l, 0, 0))],
        out_specs=pl.BlockSpec((tq, DIFF_V), lambda h, i: (i, h)),
        scratch_shapes=[pltpu.VMEM((DIFF_V, S), BF16), pltpu.VMEM((S, LANE), BF16)]
        + _score_bufs(tk, 2 * tq),
        compiler_params=_cparams(("arbitrary", "arbitrary")),
        name="diff_attn",
    )(scal, z, z, z, pos_b, lam_p, g_sub.reshape(L, 1, -1))


def _compress_kernel(x_ref, w1_ref, b1_ref, w2_ref, o_ref):
    x = x_ref[...]
    half = CMP_STRIDE * NSA_DH
    a = jnp.dot(x, w1_ref[0:half, :].astype(BF16), preferred_element_type=F32)
    b = jnp.dot(x, w1_ref[half:2 * half, :].astype(BF16), preferred_element_type=F32)
    n = x.shape[0]
    pre = a + pltpu.roll(b, n - 1, 0) + b1_ref[...]
    hdn = 0.5 * pre * (1.0 + jnp.tanh(0.7978845608028654 * (pre + 0.044715 * pre * pre * pre)))
    o_ref[...] = jnp.dot(hdn.astype(BF16), w2_ref[...].astype(BF16),
                         preferred_element_type=F32).astype(o_ref.dtype)


def _compress(xc, w1, b1, w2, l):
    _, nb, kk = xc.shape
    L = w1.shape[0]
    return pl.pallas_call(
        _compress_kernel,
        out_shape=jax.ShapeDtypeStruct((4, nb, NSA_DH), BF16),
        grid=(4,),
        in_specs=[pl.BlockSpec((None, nb, kk), lambda c: (c, 0, 0)),
                  pl.BlockSpec((None, None, 2 * kk, NSA_DH), lambda c: (l, c // 2, 0, 0)),
                  pl.BlockSpec((None, None, 1, NSA_DH), lambda c: (l, c // 2, 0, 0)),
                  pl.BlockSpec((None, None, NSA_DH, NSA_DH), lambda c: (l, c // 2, 0, 0))],
        out_specs=pl.BlockSpec((None, nb, NSA_DH), lambda c: (c, 0, 0)),
        compiler_params=_cparams(("arbitrary",)),
        name="nsa_compress",
    )(xc, w1, b1.reshape(L, 2, 1, NSA_DH), w2)


def _split3(x):
    hi = x.astype(BF16)
    r = x - hi.astype(F32)
    mid = r.astype(BF16)
    lo = (r - mid.astype(F32)).astype(BF16)
    return hi, mid, lo


def _cmp_sel_kernel(sc_ref, q_ref, kc_ref, vc_ref, pcb_ref, o_ref, selt_ref, act_ref, *, tq, n_slc):
    g = pl.program_id(0)
    i = pl.program_id(1)
    nc = kc_ref.shape[0]
    q = (q_ref[...].astype(F32) * (LOG2E * NSA_DH ** -0.5)).astype(BF16)
    lane = lax.broadcasted_iota(jnp.int32, (tq, NSA_DH), 1)
    q4 = jnp.concatenate(
        [jnp.concatenate([q[:, r * NSA_DH:(r + 1) * NSA_DH],
                          jnp.where(lane < N_SPLIT, sc_ref[g * NSA_REP + r], 0.0).astype(BF16)], 1)
         for r in range(NSA_REP)], 0)
    kx = jnp.concatenate([kc_ref[...], _alibi_features(pcb_ref[...], LOG2E)], 1)
    t = _kq(kx, q4)
    vct = vc_ref[...].astype(F32).T.astype(BF16)
    qidx = i * tq + lax.broadcasted_iota(jnp.int32, (nc, tq), 1)
    cmp_end = lax.broadcasted_iota(jnp.int32, (nc, tq), 0) * CMP_STRIDE + (CMP_BLOCK - 1)
    vis = cmp_end <= qidx
    p_sum = jnp.zeros((nc, tq), F32)
    for r in range(NSA_REP):
        s = jnp.where(vis, t[:, r * tq:(r + 1) * tq], NEG)
        e = jnp.where(vis, jnp.exp2(s - jnp.max(s, 0, keepdims=True)), 0.0)
        d = jnp.sum(e, 0, keepdims=True)
        p = e * (1.0 / jnp.where(d > 0.0, d, 1.0))
        p_sum = p_sum + p
        o = jnp.dot(vct, p.astype(BF16), preferred_element_type=F32)
        o_ref[:, r * NSA_DH:(r + 1) * NSA_DH] = o.T.astype(o_ref.dtype)
    jj = lax.broadcasted_iota(jnp.int32, (n_slc, nc), 0)
    nn = lax.broadcasted_iota(jnp.int32, (n_slc, nc), 1)
    lo_n = jj * (SLC_BLOCK // CMP_STRIDE) - 1
    ovl = jnp.where((nn >= lo_n) & (nn <= lo_n + 4), 1.0, 0.0).astype(BF16)
    imp = sum(jnp.dot(ovl, part, preferred_element_type=F32) for part in _split3(p_sum))
    blk = lax.broadcasted_iota(jnp.int32, (n_slc, tq), 0).astype(F32)
    cur = lax.shift_right_logical(i * tq + lax.broadcasted_iota(jnp.int32, (n_slc, tq), 1),
                                  int(math.log2(SLC_BLOCK))).astype(F32)
    forced = (blk == 0.0) | (blk == cur) | (blk == cur - 1.0)
    causal = blk <= cur
    score = jnp.where(forced, 3e38, jnp.where(causal, imp, -1.0))
    for _ in range(min(N_SELECT, n_slc)):
        mx = jnp.max(score, 0, keepdims=True)
        first = jnp.min(jnp.where(score == mx, blk, float(n_slc)), 0, keepdims=True)
        score = jnp.where(blk == first, PICKED, score)
    sel = jnp.where(causal & (score == PICKED), 1.0, 0.0).astype(selt_ref.dtype)
    selt_ref[...] = sel
    n_kt = act_ref.shape[0]
    tile_of_blk = lax.shift_right_logical(lax.broadcasted_iota(jnp.int32, (n_kt, n_slc), 1),
                                          int(math.log2(n_slc // n_kt)))
    grp = jnp.where(tile_of_blk == lax.broadcasted_iota(jnp.int32, (n_kt, n_slc), 0), 1.0, 0.0).astype(BF16)
    hits = jnp.sum(jnp.dot(grp, sel, preferred_element_type=F32), 1, keepdims=True)
    act_ref[...] = (hits > 0.0).astype(jnp.int32)


def _cmp_sel(z, cmp_kv, pos_cmp_b, scal):
    S = z.shape[0]
    nc = cmp_kv.shape[1]
    n_slc = S // SLC_BLOCK
    tq = SLC_TQ
    n_kt = S // SLC_TK
    gw = NSA_REP * NSA_DH
    o_cmp, sel_t, act = pl.pallas_call(
        functools.partial(_cmp_sel_kernel, tq=tq, n_slc=n_slc),
        out_shape=(jax.ShapeDtypeStruct((S, NSA_HEADS * NSA_DH), BF16),
                   jax.ShapeDtypeStruct((NSA_KV_HEADS, n_slc, S), BF16),
                   jax.ShapeDtypeStruct((NSA_KV_HEADS, S // tq, n_kt, 1), jnp.int32)),
        grid=(NSA_KV_HEADS, S // tq),
        in_specs=[pl.BlockSpec(memory_space=pltpu.SMEM),
                  pl.BlockSpec((tq, gw), lambda g, i: (i, Z_NQ // gw + g)),
                  pl.BlockSpec((None, nc, NSA_DH), lambda g, i: (g, 0, 0)),
                  pl.BlockSpec((None, nc, NSA_DH), lambda g, i: (2 + g, 0, 0)),
                  pl.BlockSpec((nc, LANE), lambda g, i: (0, 0))],
        out_specs=(pl.BlockSpec((tq, gw), lambda g, i: (i, g)),
                   pl.BlockSpec((None, n_slc, tq), lambda g, i: (g, 0, i)),
                   pl.BlockSpec((None, None, n_kt, 1), lambda g, i: (g, i, 0, 0))),
        compiler_params=_cparams(("arbitrary", "arbitrary")),
        name="nsa_cmp_select",
    )(scal, z, cmp_kv, cmp_kv, pos_cmp_b)
    return o_cmp, sel_t, act.reshape(NSA_KV_HEADS, S // tq, n_kt)


def _slc_attn_kernel(act_ref, sc_ref, q_ref, k_ref, v_ref, sel_ref, pkb_ref, o_ref,
                     vt_ref, kf_ref, sa_ref, ma_ref, sb_ref, mb_ref, lst_ref, *, tq, tk):
    g = pl.program_id(0)
    i = pl.program_id(1)

    @pl.when(i == 0)
    def _():
        _transpose_into(vt_ref, v_ref)
        _alibi_key_features(kf_ref, pkb_ref, LOG2E)

    n_slc = sel_ref.shape[0]
    q = (q_ref[...].astype(F32) * (LOG2E * NSA_DH ** -0.5)).astype(BF16)
    lane = lax.broadcasted_iota(jnp.int32, (tq, NSA_DH), 1)
    q4 = jnp.concatenate(
        [jnp.concatenate([q[:, r * NSA_DH:(r + 1) * NSA_DH],
                          jnp.where(lane < N_SPLIT, sc_ref[g * NSA_REP + r], 0.0).astype(BF16)], 1)
         for r in range(NSA_REP)], 0)
    sel = sel_ref[...]
    blk = lax.broadcasted_iota(jnp.int32, (tk, n_slc), 1)
    kblk0 = lax.shift_right_logical(lax.broadcasted_iota(jnp.int32, (tk, n_slc), 0),
                                    int(math.log2(SLC_BLOCK)))

    def score(j, buf, masked):
        off = pl.multiple_of(j * tk, tk)
        kx = jnp.concatenate([k_ref[pl.ds(off, tk), :], kf_ref[pl.ds(off, tk), :]], 1)
        t = _kq(kx, q4)
        expand = jnp.where(blk == kblk0 + j * (tk // SLC_BLOCK), 1.0, 0.0).astype(BF16)
        ok = jnp.dot(expand, sel, preferred_element_type=F32) > 0.5
        if masked:
            kidx = off + lax.broadcasted_iota(jnp.int32, (tk, tq), 0)
            qidx = i * tq + lax.broadcasted_iota(jnp.int32, (tk, tq), 1)
            ok = ok & (kidx <= qidx)
        for r in range(NSA_REP):
            lanes = slice(r * tq, (r + 1) * tq)
            _put_scores(buf, jnp.where(ok, t[:, lanes], NEG), lanes)

    def compact(t, carry):
        n, spare = carry
        hit = act_ref[g, i, t] > 0
        lst_ref[n] = t
        return n + hit.astype(jnp.int32), jnp.where(hit, spare, t)

    n, spare = lax.fori_loop(0, 2 * i, compact, (jnp.int32(0), jnp.int32(0)))
    lst_ref[n] = spare
    n_pairs = lax.shift_right_logical(n + 1, 1)

    _, l, acc = _sweep(n_pairs, tk, score, vt_ref, _init_t(NSA_REP * tq, NSA_DH),
                       (sa_ref, ma_ref), (sb_ref, mb_ref), jd=2 * i, tile_of=lambda idx: lst_ref[idx])
    on = acc / l
    for r in range(NSA_REP):
        o_ref[:, r * NSA_DH:(r + 1) * NSA_DH] = on[:, r * tq:(r + 1) * tq].T.astype(o_ref.dtype)


def _slc_attn(z, sel, act, pos_b, scal):
    S = z.shape[0]
    n_slc = S // SLC_BLOCK
    tq, tk = SLC_TQ, SLC_TK
    gw = NSA_REP * NSA_DH
    kcol = Z_NKV // LANE + 4
    grid_spec = pltpu.PrefetchScalarGridSpec(
        num_scalar_prefetch=1,
        grid=(NSA_KV_HEADS, S // tq),
        in_specs=[pl.BlockSpec(memory_space=pltpu.SMEM),
                  pl.BlockSpec((tq, gw), lambda g, i, a: (i, Z_NQ // gw + g)),
                  pl.BlockSpec((S, NSA_DH), lambda g, i, a: (0, kcol + g)),
                  pl.BlockSpec((S, NSA_DH), lambda g, i, a: (0, kcol + 2 + g)),
                  pl.BlockSpec((None, n_slc, tq), lambda g, i, a: (g, 0, i)),
                  pl.BlockSpec((S, LANE), lambda g, i, a: (0, 0))],
        out_specs=pl.BlockSpec((tq, gw), lambda g, i, a: (i, g)),
        scratch_shapes=[pltpu.VMEM((NSA_DH, S), BF16), pltpu.VMEM((S, LANE), BF16)]
        + _score_bufs(tk, NSA_REP * tq) + [pltpu.SMEM((S // tk + 1,), jnp.int32)])
    return pl.pallas_call(
        functools.partial(_slc_attn_kernel, tq=tq, tk=tk),
        out_shape=jax.ShapeDtypeStruct((S, NSA_HEADS * NSA_DH), BF16),
        grid_spec=grid_spec,
        compiler_params=_cparams(("arbitrary", "arbitrary")),
        name="nsa_slc_attn",
    )(act, scal, z, z, z, sel, pos_b)


def _win_attn_kernel(sc_ref, q_ref, k_ref, v_ref, pkb_ref, ocmp_ref, oslc_ref, gate_ref,
                     o_ref, vt_ref, kf_ref, *, t):
    g = pl.program_id(0)
    i = pl.program_id(1)

    @pl.when(i == 0)
    def _():
        _transpose_into(vt_ref, v_ref)
        _alibi_key_features(kf_ref, pkb_ref, LOG2E)

    span = t + WINDOW
    n_t = k_ref.shape[0] // t
    start = pl.multiple_of(jnp.clip(i - WINDOW // t, 0, n_t - span // t) * t, t)
    q = (q_ref[...].astype(F32) * (LOG2E * NSA_DH ** -0.5)).astype(BF16)
    lane = lax.broadcasted_iota(jnp.int32, (t, NSA_DH), 1)
    q4 = jnp.concatenate(
        [jnp.concatenate([q[:, r * NSA_DH:(r + 1) * NSA_DH],
                          jnp.where(lane < N_SPLIT, sc_ref[g * NSA_REP + r], 0.0).astype(BF16)], 1)
         for r in range(NSA_REP)], 0)
    kx = jnp.concatenate([k_ref[pl.ds(start, span), :], kf_ref[pl.ds(start, span), :]], 1)
    s = _kq(kx, q4)
    dif = (i * t + lax.broadcasted_iota(jnp.int32, (span, t), 1)
           - start - lax.broadcasted_iota(jnp.int32, (span, t), 0))
    ok = (dif >= 0) & (dif < WINDOW)
    vt = vt_ref[:, pl.ds(start, span)]
    gates = jax.nn.sigmoid(gate_ref[...].astype(F32))
    for r in range(NSA_REP):
        sr = jnp.where(ok, s[:, r * t:(r + 1) * t], NEG)
        e = jnp.exp2(sr - jnp.max(sr, 0, keepdims=True))
        d = jnp.sum(e, 0, keepdims=True)
        o_win = (jnp.dot(vt, e.astype(BF16), preferred_element_type=F32) * (1.0 / d)).T
        sl = slice(r * NSA_DH, (r + 1) * NSA_DH)

        def gcol(b):
            c0, c1 = 3 * r + b, 3 * (NSA_REP + r) + b
            return jnp.where(g == 0, gates[:, c0:c0 + 1], gates[:, c1:c1 + 1])

        o = (gcol(0) * ocmp_ref[:, sl].astype(F32) + gcol(1) * oslc_ref[:, sl].astype(F32)
             + gcol(2) * o_win)
        o_ref[:, sl] = o.astype(o_ref.dtype)


def _win_attn(z, o_cmp, o_slc, pos_b, scal):
    S = z.shape[0]
    t = 256
    gw = NSA_REP * NSA_DH
    kcol = Z_NKV // LANE + 8
    return pl.pallas_call(
        functools.partial(_win_attn_kernel, t=t),
        out_shape=jax.ShapeDtypeStruct((S, NSA_HEADS * NSA_DH), BF16),
        grid=(NSA_KV_HEADS, S // t),
        in_specs=[pl.BlockSpec(memory_space=pltpu.SMEM),
                  pl.BlockSpec((t, gw), lambda g, i: (i, Z_NQ // gw + g)),
                  pl.BlockSpec((S, NSA_DH), lambda g, i: (0, kcol + g)),
                  pl.BlockSpec((S, NSA_DH), lambda g, i: (0, kcol + 2 + g)),
                  pl.BlockSpec((S, LANE), lambda g, i: (0, 0)),
                  pl.BlockSpec((t, gw), lambda g, i: (i, g)),
                  pl.BlockSpec((t, gw), lambda g, i: (i, g)),
                  pl.BlockSpec((t, LANE), lambda g, i: (i, Z_NG // LANE))],
        out_specs=pl.BlockSpec((t, gw), lambda g, i: (i, g)),
        scratch_shapes=[pltpu.VMEM((NSA_DH, S), BF16), pltpu.VMEM((S, LANE), BF16)],
        compiler_params=_cparams(("arbitrary", "arbitrary")),
        name="nsa_win_attn",
    )(scal, z, z, z, pos_b, o_cmp, o_slc, z)


def kernel(x, c, positions, w_ada, b_ada, w_in, mla_q_norm, mla_kv_norm, mla_w_uq, mla_w_ukv, diff_lambda, diff_subln, nsa_cmp_w1, nsa_cmp_b1, nsa_cmp_w2, w_branch, w_out, ln1_g, ln1_b, w_ff1, w_ff2, ln2_g, ln2_b):
    B, S, D = x.shape
    L = w_ada.shape[0]
    assert B == 1 and D == D_MODEL and S % 512 == 0 and S // SLC_BLOCK >= 3

    w_in_x = _relayout_w_in(w_in)
    w_uq_x = _relayout_w_uq(mla_w_uq)
    w_ukv_b = mla_w_ukv.astype(BF16)
    w_out_b = w_out.astype(BF16)
    w_ff2_b = w_ff2.astype(BF16)

    slopes = 2.0 ** (-8.0 * np.arange(1, N_ALIBI_HEADS + 1, dtype=np.float64) / N_ALIBI_HEADS)
    s_diff, s_nsa = slopes[0::2], slopes[1::2]
    nsa_scal = jnp.asarray(s_nsa, F32)

    pos_f = positions.astype(F32)
    pos_col = pos_f.reshape(S, 1)
    pos_b = jnp.broadcast_to(pos_col, (S, LANE))
    pos_cmp_b = jnp.broadcast_to(jnp.pad(pos_f[0, CMP_BLOCK - 1::CMP_STRIDE], (0, 1))[:, None],
                                 (S // CMP_STRIDE, LANE))

    mod = _ada(c, w_ada, b_ada)
    rope_t = _rope_table(pos_col)
    xs = x.reshape(S, D)
    h = _lnmod(xs, mod, 0, 0, 1)

    for l in range(L):
        lam_init = 0.8 - 0.6 * math.exp(-0.3 * l)
        diff_scal = jnp.asarray(np.concatenate([s_diff * LOG2E, [lam_init]]), F32)

        z = _matmul(h, w_in_x, l, tm=1024, tn=1280, relu2=False, name="in_proj", w_transposed=True)

        qm, km, vm = _mla_proj(z, rope_t, mla_q_norm, mla_kv_norm, w_uq_x, w_ukv_b, l)
        o_mla = _mla_attn(qm, km, vm)

        o_diff = _diff_attn(z, pos_b, diff_scal, diff_lambda, diff_subln, l)

        xc = z[:, Z_NKV:Z_NKV + 4 * NSA_DH].reshape(S // CMP_STRIDE, CMP_STRIDE, 4, NSA_DH)
        xc = xc.transpose(2, 0, 1, 3).reshape(4, S // CMP_STRIDE, CMP_STRIDE * NSA_DH)
        cmp_kv = _compress(xc, nsa_cmp_w1, nsa_cmp_b1, nsa_cmp_w2, l)
        o_cmp, sel, act = _cmp_sel(z, cmp_kv, pos_cmp_b, nsa_scal)
        o_slc = _slc_attn(z, sel, act, pos_b, nsa_scal)
        o_nsa = _win_attn(z, o_cmp, o_slc, pos_b, nsa_scal)

        merged = _merge(o_mla, o_diff, o_nsa, w_branch, z, l)
        xs, h = _matmul_res_ln(merged, w_out_b, xs, mod, ln1_g, ln1_b, l, 2, ALPHA, "out_proj_ln",
                               next_mod=(l, 3, 4))
        u = _matmul(h, w_ff1, l, tm=1024, tn=1024, relu2=True, name="ff1")
        if l + 1 < L:
            xs, h = _matmul_res_ln(u, w_ff2_b, xs, mod, ln2_g, ln2_b, l, 5, ALPHA, "ff2_ln",
                                   next_mod=(l + 1, 0, 1))
        else:
            xs = _matmul_res_ln(u, w_ff2_b, xs, mod, ln2_g, ln2_b, l, 5, ALPHA, "ff2_ln")

    return xs.reshape(B, S, D)
```

```python
import functools
import math

import numpy as np
import jax
import jax.numpy as jnp
from jax import lax
from jax.experimental import pallas as pl
from jax.experimental.pallas import tpu as pltpu

F32 = jnp.float32
BF16 = jnp.bfloat16

D_MODEL = 2048
MLA_HEADS = 8
MLA_Q_LORA = 512
MLA_KV_LORA = 256
MLA_NOPE = 128
MLA_ROPE = 64
MLA_V = 128
ROPE_THETA = 10000.0
DIFF_HEADS = 8
DIFF_QK = 64
DIFF_V = 128
NSA_HEADS = 8
NSA_KV_HEADS = 2
NSA_REP = NSA_HEADS // NSA_KV_HEADS
NSA_DH = 128
CMP_BLOCK = 32
CMP_STRIDE = 16
SLC_BLOCK = 64
N_SELECT = 16
WINDOW = 512
N_ALIBI_HEADS = DIFF_HEADS + NSA_HEADS
N_BRANCH = 3
BRANCH_WIDTH = 1024
D_FF = 4 * D_MODEL
N_MOD = 6
DEPTH = 4
ALPHA = (2.0 * DEPTH) ** 0.25

LANE = 128
NEG = -1e30
LOG2E = math.log2(math.e)
T_CHUNK = 512
N_SPLIT = 3
PICKED = -2.0
SLC_TQ, SLC_TK = 512, 256
VMEM_LIMIT = 48 * 1024 * 1024

Z_CQ = 0
Z_DQ = 512
Z_DK = 1536
Z_DV = 2560
Z_NQ = 3584
Z_NKV = 4608
Z_MG = 6144
Z_CKV = 12288
Z_KR = 12544
Z_NG = 12672
Z_W = 12800


def _cparams(sem, vmem=VMEM_LIMIT):
    return pltpu.CompilerParams(dimension_semantics=sem, vmem_limit_bytes=vmem)


def _rot_half(w):
    half = MLA_ROPE // 2
    return jnp.concatenate([-w[..., half:], w[..., :half]], axis=-1)


def _relayout_w_in_kernel(wt_ref, o_ref):
    o = np.cumsum([0, 512, 256, 64, 1024, 1024, 1024, 1024, 1536, 24, N_BRANCH * D_MODEL])
    x = wt_ref[...]
    cq, ckv, kr, dq, dk, dv, nq, nkv, ng, mg = (x[o[k]:o[k + 1], :] for k in range(10))
    half = MLA_ROPE // 2
    kr_rot = jnp.concatenate([-kr[half:, :], kr[:half, :]], axis=0)
    pad = jnp.zeros((Z_W - Z_NG - ng.shape[0], x.shape[1]), x.dtype)
    o_ref[...] = jnp.concatenate([cq, dq, dk, dv, nq, nkv, mg, ckv, kr, kr_rot, ng, pad],
                                 axis=0).astype(o_ref.dtype)


def _relayout_w_in(w_in):
    L, D, N = w_in.shape
    tc = 256
    return pl.pallas_call(
        _relayout_w_in_kernel,
        out_shape=jax.ShapeDtypeStruct((L, Z_W, D), BF16),
        grid=(L, D // tc),
        in_specs=[pl.BlockSpec((None, N, tc), lambda l, i: (l, 0, i))],
        out_specs=pl.BlockSpec((None, Z_W, tc), lambda l, i: (l, 0, i)),
        compiler_params=_cparams(("arbitrary", "arbitrary")),
        name="w_in_relayout",
    )(jnp.swapaxes(w_in, 1, 2))


def _relayout_w_uq(w_uq):
    L, K, _ = w_uq.shape
    w = w_uq.reshape(L, K, MLA_HEADS, MLA_NOPE + MLA_ROPE)
    rope = w[..., MLA_NOPE:]
    return jnp.concatenate([w, _rot_half(rope)], axis=-1).reshape(L, K, MLA_HEADS * 256).astype(BF16)


def _ada_kernel(c_ref, w_ref, b_ref, o_ref):
    c = c_ref[...]
    ca = c * jax.nn.sigmoid(c)
    ca8 = jnp.broadcast_to(ca, (8, c.shape[1])).astype(BF16)
    r = jnp.dot(ca8, w_ref[...].astype(BF16), preferred_element_type=F32)
    o_ref[...] = r[0:1, :] + b_ref[...]


def _ada(c, w_ada, b_ada):
    L, D, N = w_ada.shape
    tn = 1536
    return pl.pallas_call(
        _ada_kernel,
        out_shape=jax.ShapeDtypeStruct((L, 1, N), F32),
        grid=(L, N // tn),
        in_specs=[pl.BlockSpec((1, D), lambda l, j: (0, 0)),
                  pl.BlockSpec((None, D, tn), lambda l, j: (l, 0, j)),
                  pl.BlockSpec((None, 1, tn), lambda l, j: (l, 0, j))],
        out_specs=pl.BlockSpec((None, 1, tn), lambda l, j: (l, 0, j)),
        compiler_params=_cparams(("arbitrary", "arbitrary")),
        name="ada_mod",
    )(c, w_ada, b_ada.reshape(L, 1, N))


def _lnmod_kernel(x_ref, sh_ref, sc_ref, o_ref):
    x = x_ref[...]
    mu = jnp.mean(x, -1, keepdims=True)
    xc = x - mu
    var = jnp.mean(xc * xc, -1, keepdims=True)
    y = xc * lax.rsqrt(var + 1e-5)
    o_ref[...] = (y * (1.0 + sc_ref[...]) + sh_ref[...]).astype(o_ref.dtype)


def _lnmod(x, mod, l, k_shift, k_scale):
    S, D = x.shape
    tm = 512
    return pl.pallas_call(
        _lnmod_kernel,
        out_shape=jax.ShapeDtypeStruct((S, D), BF16),
        grid=(S // tm,),
        in_specs=[pl.BlockSpec((tm, D), lambda i: (i, 0)),
                  pl.BlockSpec((None, 1, D), lambda i: (l, 0, k_shift)),
                  pl.BlockSpec((None, 1, D), lambda i: (l, 0, k_scale))],
        out_specs=pl.BlockSpec((tm, D), lambda i: (i, 0)),
        compiler_params=_cparams(("arbitrary",)),
        name="ln_modulate",
    )(x, mod, mod)


def _mm_kernel(x_ref, w_ref, o_ref, *, relu2, w_transposed):
    w = w_ref[...].astype(BF16)
    contract = (((1,), (1,)), ((), ())) if w_transposed else (((1,), (0,)), ((), ()))
    r = lax.dot_general(x_ref[...], w, contract, preferred_element_type=F32)
    if relu2:
        r = jnp.square(jnp.maximum(r, 0.0))
    o_ref[...] = r.astype(o_ref.dtype)


def _matmul(x, w, l, *, tm, tn, relu2, name, w_transposed=False):
    S, K = x.shape
    if w_transposed:
        N = w.shape[1]
        w_spec = pl.BlockSpec((None, tn, K), lambda i, j: (l, j, 0))
    else:
        N = w.shape[2]
        w_spec = pl.BlockSpec((None, K, tn), lambda i, j: (l, 0, j))
    return pl.pallas_call(
        functools.partial(_mm_kernel, relu2=relu2, w_transposed=w_transposed),
        out_shape=jax.ShapeDtypeStruct((S, N), BF16),
        grid=(S // tm, N // tn),
        in_specs=[pl.BlockSpec((tm, K), lambda i, j: (i, 0)), w_spec],
        out_specs=pl.BlockSpec((tm, tn), lambda i, j: (i, j)),
        compiler_params=_cparams(("arbitrary", "arbitrary")),
        name=name,
    )(x, w)


def _ln_plain(y):
    mu = jnp.mean(y, -1, keepdims=True)
    yc = y - mu
    return yc * lax.rsqrt(jnp.mean(yc * yc, -1, keepdims=True) + 1e-5)


def _mm_res_ln_kernel(a_ref, w_ref, x_ref, gate_ref, lng_ref, lnb_ref, *rest, nk, alpha, with_next):
    o_ref = rest[2] if with_next else rest[0]
    k = pl.program_id(1)
    part = jnp.dot(a_ref[...], w_ref[...], preferred_element_type=F32)

    @pl.when(k == 0)
    def _():
        o_ref[...] = part

    @pl.when(k > 0)
    def _():
        o_ref[...] += part

    @pl.when(k == nk - 1)
    def _():
        x_new = _ln_plain(alpha * x_ref[...] + gate_ref[...] * o_ref[...]) * lng_ref[...] + lnb_ref[...]
        o_ref[...] = x_new
        if with_next:
            sh_ref, sc_ref, _, h_ref = rest
            h_ref[...] = (_ln_plain(x_new) * (1.0 + sc_ref[...]) + sh_ref[...]).astype(h_ref.dtype)


def _matmul_res_ln(a, w, x, mod, lng, lnb, l, k_gate, alpha, name, next_mod=None):
    S, K = a.shape
    D = x.shape[1]
    L = lng.shape[0]
    tm, tk = 512, 2048
    nk = K // tk
    vmem = VMEM_LIMIT + (2 * tm * D * 2 if next_mod is not None else 0)
    vec =lambda ll, kk: pl.BlockSpec((None, 1, D), lambda i, k: (ll, 0, kk))
    row = pl.BlockSpec((tm, D), lambda i, k: (i, 0))
    in_specs = [pl.BlockSpec((tm, tk), lambda i, k: (i, k)),
                pl.BlockSpec((None, tk, D), lambda i, k: (l, k, 0)),
                row, vec(l, k_gate), vec(l, 0), vec(l, 0)]
    args = [a, w, x, mod, lng.reshape(L, 1, D), lnb.reshape(L, 1, D)]
    out_shape, out_specs = jax.ShapeDtypeStruct((S, D), F32), row
    if next_mod is not None:
        in_specs += [vec(next_mod[0], next_mod[1]), vec(next_mod[0], next_mod[2])]
        args += [mod, mod]
        out_shape, out_specs = (out_shape, jax.ShapeDtypeStruct((S, D), BF16)), (row, row)
    return pl.pallas_call(
        functools.partial(_mm_res_ln_kernel, nk=nk, alpha=alpha, with_next=next_mod is not None),
        out_shape=out_shape,
        grid=(S // tm, nk),
        in_specs=in_specs,
        out_specs=out_specs,
        compiler_params=_cparams(("arbitrary", "arbitrary"), vmem),
        name=name,
    )(*args)


def _merge_kernel(o0_ref, o1_ref, o2_ref, w_ref, g0_ref, g1_ref, g2_ref, out_ref):
    acc = None
    for n, (o_ref, g_ref) in enumerate(((o0_ref, g0_ref), (o1_ref, g1_ref), (o2_ref, g2_ref))):
        t = jax.nn.sigmoid(g_ref[...].astype(F32)) * jnp.dot(o_ref[...], w_ref[n].astype(BF16),
                                                               preferred_element_type=F32)
        acc = t if acc is None else acc + t
    out_ref[...] = acc.astype(out_ref.dtype)


def _merge(o_mla, o_diff, o_nsa, w_branch, z, l):
    S = o_mla.shape[0]
    tm, tn = 1024, 512
    bspec = pl.BlockSpec((tm, BRANCH_WIDTH), lambda i, j: (i, 0))

    def gspec(n):
        return pl.BlockSpec((tm, tn), lambda i, j: (i, (Z_MG + D_MODEL * n) // tn + j))

    return pl.pallas_call(
        _merge_kernel,
        out_shape=jax.ShapeDtypeStruct((S, D_MODEL), BF16),
        grid=(S // tm, D_MODEL // tn),
        in_specs=[bspec, bspec, bspec,
                  pl.BlockSpec((None, N_BRANCH, BRANCH_WIDTH, tn), lambda i, j: (l, 0, 0, j)),
                  gspec(0), gspec(1), gspec(2)],
        out_specs=pl.BlockSpec((tm, tn), lambda i, j: (i, j)),
        compiler_params=_cparams(("arbitrary", "arbitrary")),
        name="branch_merge",
    )(o_mla, o_diff, o_nsa, w_branch, z, z, z)


def _rope_kernel(pos_ref, inv_ref, o_ref):
    ang = pos_ref[...] * inv_ref[...]
    lane = lax.broadcasted_iota(jnp.int32, ang.shape, 1)
    o_ref[...] = jnp.where(lane < MLA_ROPE, jnp.cos(ang), jnp.sin(ang))


def _rope_table(pos_col):
    S = pos_col.shape[0]
    inv = ROPE_THETA ** (-np.arange(0, MLA_ROPE, 2, dtype=np.float64) / MLA_ROPE)
    inv4 = jnp.asarray(np.tile(inv, 4)[None, :], F32)
    tm = 512
    return pl.pallas_call(
        _rope_kernel,
        out_shape=jax.ShapeDtypeStruct((S, LANE), F32),
        grid=(S // tm,),
        in_specs=[pl.BlockSpec((tm, 1), lambda i: (i, 0)),
                  pl.BlockSpec((1, LANE), lambda i: (0, 0))],
        out_specs=pl.BlockSpec((tm, LANE), lambda i: (i, 0)),
        compiler_params=_cparams(("arbitrary",)),
        name="rope_table",
    )(pos_col, inv4)


def _rms(x, g):
    return x * lax.rsqrt(jnp.mean(x * x, -1, keepdims=True) + 1e-6) * g


def _mla_proj_kernel(cq_ref, ckv_ref, kr_ref, t_ref, gq_ref, gkv_ref, wuq_ref, wukv_ref,
                     q_out, k_out, v_out):
    scale = LOG2E * (MLA_NOPE + MLA_ROPE) ** -0.5
    t = t_ref[...]
    nq = _rms(cq_ref[...].astype(F32), gq_ref[...]).astype(BF16)
    q = jnp.dot(nq, wuq_ref[...], preferred_element_type=F32)
    for h in range(MLA_HEADS):
        c = 256 * h
        q_out[:, c:c + 128] = (q[:, c:c + 128] * scale).astype(BF16)
        q_out[:, c + 128:c + 256] = (q[:, c + 128:c + 256] * t * scale).astype(BF16)
    nkv = _rms(ckv_ref[...].astype(F32), gkv_ref[...]).astype(BF16)
    kv = jnp.dot(nkv, wukv_ref[...], preferred_element_type=F32)
    kr = kr_ref[...].astype(F32) * t
    kr2 = (kr + pltpu.roll(kr, MLA_ROPE, 1)).astype(BF16)
    for h in range(MLA_HEADS):
        c = 256 * h
        k_out[:, c:c + 128] = kv[:, c:c + 128].astype(BF16)
        k_out[:, c + 128:c + 256] = kr2
        v_out[:, 128 * h:128 * h + 128] = kv[:, c + 128:c + 256].astype(BF16)


def _mla_proj(z, rope_t, gq, gkv, wuq, wukv, l):
    S = z.shape[0]
    L = gq.shape[0]
    tm = 512
    HW = MLA_HEADS * 256
    return pl.pallas_call(
        _mla_proj_kernel,
        out_shape=(jax.ShapeDtypeStruct((S, HW), BF16),
                   jax.ShapeDtypeStruct((S, HW), BF16),
                   jax.ShapeDtypeStruct((S, MLA_HEADS * MLA_V), BF16)),
        grid=(S // tm,),
        in_specs=[pl.BlockSpec((tm, MLA_Q_LORA), lambda i: (i, Z_CQ // MLA_Q_LORA)),
                  pl.BlockSpec((tm, MLA_KV_LORA), lambda i: (i, Z_CKV // MLA_KV_LORA)),
                  pl.BlockSpec((tm, LANE), lambda i: (i, Z_KR // LANE)),
                  pl.BlockSpec((tm, LANE), lambda i: (i, 0)),
                  pl.BlockSpec((None, 1, MLA_Q_LORA), lambda i: (l, 0, 0)),
                  pl.BlockSpec((None, 1, MLA_KV_LORA), lambda i: (l, 0, 0)),
                  pl.BlockSpec((None, MLA_Q_LORA, HW), lambda i: (l, 0, 0)),
                  pl.BlockSpec((None, MLA_KV_LORA, HW), lambda i: (l, 0, 0))],
        out_specs=(pl.BlockSpec((tm, HW), lambda i: (i, 0)),
                   pl.BlockSpec((tm, HW), lambda i: (i, 0)),
                   pl.BlockSpec((tm, MLA_HEADS * MLA_V), lambda i: (i, 0))),
        compiler_params=_cparams(("arbitrary",)),
        name="mla_proj",
    )(z, z, z, rope_t, gq.reshape(L, 1, -1), gkv.reshape(L, 1, -1), wuq, wukv)


def _qk(q, k):
    return lax.dot_general(q, k, (((1,), (1,)), ((), ())), preferred_element_type=F32)


def _online(s, v, m, l, acc, valid=None):
    m_new = jnp.maximum(m, jnp.max(s, -1, keepdims=True))
    a = jnp.exp(m - m_new)
    p = jnp.exp(s - m_new)
    if valid is not None:
        p = jnp.where(valid, p, 0.0)
    l = a * l + jnp.sum(p, -1, keepdims=True)
    acc = a * acc + jnp.dot(p.astype(BF16), v, preferred_element_type=F32)
    return m_new, l, acc


def _init(rows, dv):
    return (jnp.full((rows, 1), NEG, F32), jnp.zeros((rows, 1), F32), jnp.zeros((rows, dv), F32))


def _kq(k, q):
    return lax.dot_general(k, q, (((1,), (1,)), ((), ())), preferred_element_type=F32)


def _online_t(s, vt, m, l, acc):
    m_new = jnp.maximum(m, jnp.max(s, 0, keepdims=True))
    a = jnp.exp2(m - m_new)
    p = jnp.exp2(s - m_new)
    l = a * l + jnp.sum(p, 0, keepdims=True)
    acc = a * acc + jnp.dot(vt, p.astype(BF16), preferred_element_type=F32)
    return m_new, l, acc


def _init_t(nq, dv):
    return (jnp.full((1, nq), NEG, F32), jnp.zeros((1, nq), F32), jnp.zeros((dv, nq), F32))


def _transpose_into(vt_ref, v_ref):
    for c in range(v_ref.shape[0] // T_CHUNK):
        rows = slice(c * T_CHUNK, (c + 1) * T_CHUNK)
        vt_ref[:, rows] = v_ref[rows, :].astype(F32).T.astype(vt_ref.dtype)


def _put_scores(buf, t, lanes=None):
    s_ref, m_ref = buf
    if lanes is None:
        s_ref[...] = t
        m_ref[...] = jnp.max(t, 0, keepdims=True)
    else:
        s_ref[:, lanes] = t
        m_ref[:, lanes] = jnp.max(t, 0, keepdims=True)


def _sweep(n_pairs, tk, score, vt_ref, carry, buf_a, buf_b, jd=None, tile_of=lambda n: n):
    def update(j, buf, carry):
        s_ref, mt_ref = buf
        m, l, acc = carry
        m_new = jnp.maximum(m, mt_ref[...])
        a = jnp.exp2(m - m_new)
        p = jnp.exp2(s_ref[...] - m_new)
        l = a * l + jnp.sum(p, 0, keepdims=True)
        vt = vt_ref[:, pl.ds(pl.multiple_of(j * tk, tk), tk)]
        acc = a * acc + jnp.dot(vt, p.astype(BF16), preferred_element_type=F32)
        return m_new, l, acc

    if jd is None:
        jd = 2 * n_pairs
    score(jd, buf_a, True)
    score(jd + 1, buf_b, True)
    carry = update(jd, buf_a, carry)

    def pair(p, carry):
        ja, jb = tile_of(2 * p), tile_of(2 * p + 1)
        score(ja, buf_a, False)
        carry = update(jnp.where(p == 0, jd + 1, tile_of(jnp.maximum(2 * p - 1, 0))), buf_b, carry)
        score(jb, buf_b, False)
        return update(ja, buf_a, carry)

    n_quads = lax.shift_right_logical(n_pairs, 1)
    carry = lax.fori_loop(0, n_quads, lambda d, c: pair(2 * d + 1, pair(2 * d, c)), carry)
    carry = lax.cond(n_pairs > 2 * n_quads, lambda c: pair(n_pairs - 1, c), lambda c: c, carry)
    last = jnp.where(n_pairs == 0, jd + 1, tile_of(jnp.maximum(2 * n_pairs - 1, 0)))
    return update(last, buf_b, carry)


def _score_bufs(tk, nq):
    return [pltpu.VMEM((tk, nq), F32), pltpu.VMEM((1, nq), F32),
            pltpu.VMEM((tk, nq), F32), pltpu.VMEM((1, nq), F32)]


def _mla_attn_kernel(q_ref, k_ref, v_ref, o_ref, vt_ref, sa_ref, ma_ref, sb_ref, mb_ref, *, tq, tk):
    i = pl.program_id(1)

    @pl.when(i == 0)
    def _():
        _transpose_into(vt_ref, v_ref)

    q = q_ref[...]

    def score(j, buf, masked):
        off = pl.multiple_of(j * tk, tk)
        t = _kq(k_ref[pl.ds(off, tk), :], q)
        if masked:
            kidx = off + lax.broadcasted_iota(jnp.int32, (tk, tq), 0)
            qidx = i * tq + lax.broadcasted_iota(jnp.int32, (tk, tq), 1)
            t = jnp.where(kidx <= qidx, t, NEG)
        _put_scores(buf, t)

    _, l, acc = _sweep(i, tk, score, vt_ref, _init_t(tq, MLA_V), (sa_ref, ma_ref), (sb_ref, mb_ref))
    o_ref[...] = (acc / l).T.astype(o_ref.dtype)


def _mla_attn(qm, km, vm):
    S = qm.shape[0]
    tq, tk = 1024, 512
    return pl.pallas_call(
        functools.partial(_mla_attn_kernel, tq=tq, tk=tk),
        out_shape=jax.ShapeDtypeStruct((S, MLA_HEADS * MLA_V), BF16),
        grid=(MLA_HEADS, S // tq),
        in_specs=[pl.BlockSpec((tq, 256), lambda h, i: (i, h)),
                  pl.BlockSpec((S, 256), lambda h, i: (0, h)),
                  pl.BlockSpec((S, MLA_V), lambda h, i: (0, h))],
        out_specs=pl.BlockSpec((tq, MLA_V), lambda h, i: (i, h)),
        scratch_shapes=[pltpu.VMEM((MLA_V, S), BF16)] + _score_bufs(tk, tq),
        compiler_params=_cparams(("arbitrary", "arbitrary")),
        name="mla_attn",
    )(qm, km, vm)


def _alibi_features(pkb, coef):
    hi, mid, lo = _split3(coef * pkb)
    lane = lax.broadcasted_iota(jnp.int32, hi.shape, 1)
    zero = jnp.zeros_like(hi)
    return jnp.where(lane == 0, hi, jnp.where(lane == 1, mid, jnp.where(lane == 2, lo, zero)))


def _alibi_key_features(kf_ref, pkb_ref, coef):
    for c in range(pkb_ref.shape[0] // T_CHUNK):
        rows = slice(c * T_CHUNK, (c + 1) * T_CHUNK)
        kf_ref[rows, :] = _alibi_features(pkb_ref[rows, :], coef)


def _diff_attn_kernel(sc_ref, q_ref, k_ref, v_ref, pkb_ref, lam_ref, g_ref, o_ref,
                      vt_ref, kf_ref, sa_ref, ma_ref, sb_ref, mb_ref, *, tq, tk):
    h = pl.program_id(0)
    i = pl.program_id(1)

    @pl.when(i == 0)
    def _():
        _transpose_into(vt_ref, v_ref)
        _alibi_key_features(kf_ref, pkb_ref, sc_ref[h])

    lam_init = sc_ref[DIFF_HEADS]
    qs = (q_ref[...].astype(F32) * (LOG2E * DIFF_QK ** -0.5)).astype(BF16)
    lane = lax.broadcasted_iota(jnp.int32, qs.shape, 1)
    zero = jnp.zeros_like(qs)
    ones = jnp.where(lane < N_SPLIT, 1.0, 0.0).astype(BF16)
    q2 = jnp.concatenate([jnp.concatenate([jnp.where(lane < DIFF_QK, qs, zero), ones], 1),
                          jnp.concatenate([jnp.where(lane >= DIFF_QK, qs, zero), ones], 1)], 0)

    def score(j, buf, masked):
        off = pl.multiple_of(j * tk, tk)
        kx = jnp.concatenate([k_ref[pl.ds(off, tk), :], kf_ref[pl.ds(off, tk), :]], 1)
        t = _kq(kx, q2)
        if masked:
            kidx = off + lax.broadcasted_iota(jnp.int32, (tk, tq), 0)
            qidx = i * tq + lax.broadcasted_iota(jnp.int32, (tk, tq), 1)
            ok = kidx <= qidx
            for c in range(2):
                lanes = slice(c * tq, (c + 1) * tq)
                _put_scores(buf, jnp.where(ok, t[:, lanes], NEG), lanes)
        else:
            _put_scores(buf, t)

    _, l, acc = _sweep(i, tk, score, vt_ref, _init_t(2 * tq, DIFF_V), (sa_ref, ma_ref), (sb_ref, mb_ref))
    on = acc / l
    lp = lam_ref[...]
    lam = (jnp.exp(jnp.sum(lp[0:1] * lp[1:2], -1, keepdims=True))
           - jnp.exp(jnp.sum(lp[2:3] * lp[3:4], -1, keepdims=True)) + lam_init)
    o = on[:, :tq] - lam * on[:, tq:]
    o = o * lax.rsqrt(jnp.mean(o * o, 0, keepdims=True) + 1e-6)
    o_ref[...] = (o.T * g_ref[...] * (1.0 - lam_init)).astype(o_ref.dtype)


def _diff_attn(z, pos_b, scal, lam_p, g_sub, l):
    S = z.shape[0]
    L = g_sub.shape[0]
    tq, tk = 1024, 512
    return pl.pallas_call(
        functools.partial(_diff_attn_kernel, tq=tq, tk=tk),
        out_shape=jax.ShapeDtypeStruct((S, DIFF_HEADS * DIFF_V), BF16),
        grid=(DIFF_HEADS, S // tq),
        in_specs=[pl.BlockSpec(memory_space=pltpu.SMEM),
                  pl.BlockSpec((tq, LANE), lambda h, i: (i, Z_DQ // LANE + h)),
                  pl.BlockSpec((S, LANE), lambda h, i: (0, Z_DK // LANE + h)),
                  pl.BlockSpec((S, LANE), lambda h, i: (0, Z_DV // LANE + h)),
                  pl.BlockSpec((S, LANE), lambda h, i: (0, 0)),
                  pl.BlockSpec((None, 4, DIFF_QK), lambda h, i: (l, 0, 0)),
                  pl.BlockSpec((None, 1, DIFF_V), lambda h, i: (l, 0, 0))],
        out_specs=pl.BlockSpec((tq, DIFF_V), lambda h, i: (i, h)),
        scratch_shapes=[pltpu.VMEM((DIFF_V, S), BF16), pltpu.VMEM((S, LANE), BF16)]
        + _score_bufs(tk, 2 * tq),
        compiler_params=_cparams(("arbitrary", "arbitrary")),
        name="diff_attn",
    )(scal, z, z, z, pos_b, lam_p, g_sub.reshape(L, 1, -1))


def _compress_kernel(x_ref, w1_ref, b1_ref, w2_ref, o_ref):
    x = x_ref[...]
    half = CMP_STRIDE * NSA_DH
    a = jnp.dot(x, w1_ref[0:half, :].astype(BF16), preferred_element_type=F32)
    b = jnp.dot(x, w1_ref[half:2 * half, :].astype(BF16), preferred_element_type=F32)
    n = x.shape[0]
    pre = a + pltpu.roll(b, n - 1, 0) + b1_ref[...]
    hdn = 0.5 * pre * (1.0 + jnp.tanh(0.7978845608028654 * (pre + 0.044715 * pre * pre * pre)))
    o_ref[...] = jnp.dot(hdn.astype(BF16), w2_ref[...].astype(BF16),
                         preferred_element_type=F32).astype(o_ref.dtype)


def _compress(xc, w1, b1, w2, l):
    _, nb, kk = xc.shape
    L = w1.shape[0]
    return pl.pallas_call(
        _compress_kernel,
        out_shape=jax.ShapeDtypeStruct((4, nb, NSA_DH), BF16),
        grid=(4,),
        in_specs=[pl.BlockSpec((None, nb, kk), lambda c: (c, 0, 0)),
                  pl.BlockSpec((None, None, 2 * kk, NSA_DH), lambda c: (l, c // 2, 0, 0)),
                  pl.BlockSpec((None, None, 1, NSA_DH), lambda c: (l, c // 2, 0, 0)),
                  pl.BlockSpec((None, None, NSA_DH, NSA_DH), lambda c: (l, c // 2, 0, 0))],
        out_specs=pl.BlockSpec((None, nb, NSA_DH), lambda c: (c, 0, 0)),
        compiler_params=_cparams(("arbitrary",)),
        name="nsa_compress",
    )(xc, w1, b1.reshape(L, 2, 1, NSA_DH), w2)


def _split3(x):
    hi = x.astype(BF16)
    r = x - hi.astype(F32)
    mid = r.astype(BF16)
    lo = (r - mid.astype(F32)).astype(BF16)
    return hi, mid, lo


def _cmp_sel_kernel(sc_ref, q_ref, kc_ref, vc_ref, pcb_ref, o_ref, selt_ref, act_ref, *, tq, n_slc):
    g = pl.program_id(0)
    i = pl.program_id(1)
    nc = kc_ref.shape[0]
    q = (q_ref[...].astype(F32) * (LOG2E * NSA_DH ** -0.5)).astype(BF16)
    lane = lax.broadcasted_iota(jnp.int32, (tq, NSA_DH), 1)
    q4 = jnp.concatenate(
        [jnp.concatenate([q[:, r * NSA_DH:(r + 1) * NSA_DH],
                          jnp.where(lane < N_SPLIT, sc_ref[g * NSA_REP + r], 0.0).astype(BF16)], 1)
         for r in range(NSA_REP)], 0)
    kx = jnp.concatenate([kc_ref[...], _alibi_features(pcb_ref[...], LOG2E)], 1)
    t = _kq(kx, q4)
    vct = vc_ref[...].astype(F32).T.astype(BF16)
    qidx = i * tq + lax.broadcasted_iota(jnp.int32, (nc, tq), 1)
    cmp_end = lax.broadcasted_iota(jnp.int32, (nc, tq), 0) * CMP_STRIDE + (CMP_BLOCK - 1)
    vis = cmp_end <= qidx
    p_sum = jnp.zeros((nc, tq), F32)
    for r in range(NSA_REP):
        s = jnp.where(vis, t[:, r * tq:(r + 1) * tq], NEG)
        e = jnp.where(vis, jnp.exp2(s - jnp.max(s, 0, keepdims=True)), 0.0)
        d = jnp.sum(e, 0, keepdims=True)
        p = e * (1.0 / jnp.where(d > 0.0, d, 1.0))
        p_sum = p_sum + p
        o = jnp.dot(vct, p.astype(BF16), preferred_element_type=F32)
        o_ref[:, r * NSA_DH:(r + 1) * NSA_DH] = o.T.astype(o_ref.dtype)
    jj = lax.broadcasted_iota(jnp.int32, (n_slc, nc), 0)
    nn = lax.broadcasted_iota(jnp.int32, (n_slc, nc), 1)
    lo_n = jj * (SLC_BLOCK // CMP_STRIDE) - 1
    ovl = jnp.where((nn >= lo_n) & (nn <= lo_n + 4), 1.0, 0.0).astype(BF16)
    imp = sum(jnp.dot(ovl, part, preferred_element_type=F32) for part in _split3(p_sum))
    blk = lax.broadcasted_iota(jnp.int32, (n_slc, tq), 0).astype(F32)
    cur = lax.shift_right_logical(i * tq + lax.broadcasted_iota(jnp.int32, (n_slc, tq), 1),
                                  int(math.log2(SLC_BLOCK))).astype(F32)
    forced = (blk == 0.0) | (blk == cur) | (blk == cur - 1.0)
    causal = blk <= cur
    score = jnp.where(forced, 3e38, jnp.where(causal, imp, -1.0))
    for _ in range(min(N_SELECT, n_slc)):
        mx = jnp.max(score, 0, keepdims=True)
        first = jnp.min(jnp.where(score == mx, blk, float(n_slc)), 0, keepdims=True)
        score = jnp.where(blk == first, PICKED, score)
    sel = jnp.where(causal & (score == PICKED), 1.0, 0.0).astype(selt_ref.dtype)
    selt_ref[...] = sel
    n_kt = act_ref.shape[0]
    tile_of_blk = lax.shift_right_logical(lax.broadcasted_iota(jnp.int32, (n_kt, n_slc), 1),
                                          int(math.log2(n_slc // n_kt)))
    grp = jnp.where(tile_of_blk == lax.broadcasted_iota(jnp.int32, (n_kt, n_slc), 0), 1.0, 0.0).astype(BF16)
    hits = jnp.sum(jnp.dot(grp, sel, preferred_element_type=F32), 1, keepdims=True)
    act_ref[...] = (hits > 0.0).astype(jnp.int32)


def _cmp_sel(z, cmp_kv, pos_cmp_b, scal):
    S = z.shape[0]
    nc = cmp_kv.shape[1]
    n_slc = S // SLC_BLOCK
    tq = SLC_TQ
    n_kt = S // SLC_TK
    gw = NSA_REP * NSA_DH
    o_cmp, sel_t, act = pl.pallas_call(
        functools.partial(_cmp_sel_kernel, tq=tq, n_slc=n_slc),
        out_shape=(jax.ShapeDtypeStruct((S, NSA_HEADS * NSA_DH), BF16),
                   jax.ShapeDtypeStruct((NSA_KV_HEADS, n_slc, S), BF16),
                   jax.ShapeDtypeStruct((NSA_KV_HEADS, S // tq, n_kt, 1), jnp.int32)),
        grid=(NSA_KV_HEADS, S // tq),
        in_specs=[pl.BlockSpec(memory_space=pltpu.SMEM),
                  pl.BlockSpec((tq, gw), lambda g, i: (i, Z_NQ // gw + g)),
                  pl.BlockSpec((None, nc, NSA_DH), lambda g, i: (g, 0, 0)),
                  pl.BlockSpec((None, nc, NSA_DH), lambda g, i: (2 + g, 0, 0)),
                  pl.BlockSpec((nc, LANE), lambda g, i: (0, 0))],
        out_specs=(pl.BlockSpec((tq, gw), lambda g, i: (i, g)),
                   pl.BlockSpec((None, n_slc, tq), lambda g, i: (g, 0, i)),
                   pl.BlockSpec((None, None, n_kt, 1), lambda g, i: (g, i, 0, 0))),
        compiler_params=_cparams(("arbitrary", "arbitrary")),
        name="nsa_cmp_select",
    )(scal, z, cmp_kv, cmp_kv, pos_cmp_b)
    return o_cmp, sel_t, act.reshape(NSA_KV_HEADS, S // tq, n_kt)


def _slc_attn_kernel(act_ref, sc_ref, q_ref, k_ref, v_ref, sel_ref, pkb_ref, o_ref,
                     vt_ref, kf_ref, sa_ref, ma_ref, sb_ref, mb_ref, lst_ref, *, tq, tk):
    g = pl.program_id(0)
    i = pl.program_id(1)

    @pl.when(i == 0)
    def _():
        _transpose_into(vt_ref, v_ref)
        _alibi_key_features(kf_ref, pkb_ref, LOG2E)

    n_slc = sel_ref.shape[0]
    q = (q_ref[...].astype(F32) * (LOG2E * NSA_DH ** -0.5)).astype(BF16)
    lane = lax.broadcasted_iota(jnp.int32, (tq, NSA_DH), 1)
    q4 = jnp.concatenate(
        [jnp.concatenate([q[:, r * NSA_DH:(r + 1) * NSA_DH],
                          jnp.where(lane < N_SPLIT, sc_ref[g * NSA_REP + r], 0.0).astype(BF16)], 1)
         for r in range(NSA_REP)], 0)
    sel = sel_ref[...]
    blk = lax.broadcasted_iota(jnp.int32, (tk, n_slc), 1)
    kblk0 = lax.shift_right_logical(lax.broadcasted_iota(jnp.int32, (tk, n_slc), 0),
                                    int(math.log2(SLC_BLOCK)))

    def score(j, buf, masked):
        off = pl.multiple_of(j * tk, tk)
        kx = jnp.concatenate([k_ref[pl.ds(off, tk), :], kf_ref[pl.ds(off, tk), :]], 1)
        t = _kq(kx, q4)
        expand = jnp.where(blk == kblk0 + j * (tk // SLC_BLOCK), 1.0, 0.0).astype(BF16)
        ok = jnp.dot(expand, sel, preferred_element_type=F32) > 0.5
        if masked:
            kidx = off + lax.broadcasted_iota(jnp.int32, (tk, tq), 0)
            qidx = i * tq + lax.broadcasted_iota(jnp.int32, (tk, tq), 1)
            ok = ok & (kidx <= qidx)
        for r in range(NSA_REP):
            lanes = slice(r * tq, (r + 1) * tq)
            _put_scores(buf, jnp.where(ok, t[:, lanes], NEG), lanes)

    def compact(t, carry):
        n, spare = carry
        hit = act_ref[g, i, t] > 0
        lst_ref[n] = t
        return n + hit.astype(jnp.int32), jnp.where(hit, spare, t)

    n, spare = lax.fori_loop(0, 2 * i, compact, (jnp.int32(0), jnp.int32(0)))
    lst_ref[n] = spare
    n_pairs = lax.shift_right_logical(n + 1, 1)

    _, l, acc = _sweep(n_pairs, tk, score, vt_ref, _init_t(NSA_REP * tq, NSA_DH),
                       (sa_ref, ma_ref), (sb_ref, mb_ref), jd=2 * i, tile_of=lambda idx: lst_ref[idx])
    on = acc / l
    for r in range(NSA_REP):
        o_ref[:, r * NSA_DH:(r + 1) * NSA_DH] = on[:, r * tq:(r + 1) * tq].T.astype(o_ref.dtype)


def _slc_attn(z, sel, act, pos_b, scal):
    S = z.shape[0]
    n_slc = S // SLC_BLOCK
    tq, tk = SLC_TQ, SLC_TK
    gw = NSA_REP * NSA_DH
    kcol = Z_NKV // LANE + 4
    grid_spec = pltpu.PrefetchScalarGridSpec(
        num_scalar_prefetch=1,
        grid=(NSA_KV_HEADS, S // tq),
        in_specs=[pl.BlockSpec(memory_space=pltpu.SMEM),
                  pl.BlockSpec((tq, gw), lambda g, i, a: (i, Z_NQ // gw + g)),
                  pl.BlockSpec((S, NSA_DH), lambda g, i, a: (0, kcol + g)),
                  pl.BlockSpec((S, NSA_DH), lambda g, i, a: (0, kcol + 2 + g)),
                  pl.BlockSpec((None, n_slc, tq), lambda g, i, a: (g, 0, i)),
                  pl.BlockSpec((S, LANE), lambda g, i, a: (0, 0))],
        out_specs=pl.BlockSpec((tq, gw), lambda g, i, a: (i, g)),
        scratch_shapes=[pltpu.VMEM((NSA_DH, S), BF16), pltpu.VMEM((S, LANE), BF16)]
        + _score_bufs(tk, NSA_REP * tq) + [pltpu.SMEM((S // tk + 1,), jnp.int32)])
    return pl.pallas_call(
        functools.partial(_slc_attn_kernel, tq=tq, tk=tk),
        out_shape=jax.ShapeDtypeStruct((S, NSA_HEADS * NSA_DH), BF16),
        grid_spec=grid_spec,
        compiler_params=_cparams(("arbitrary", "arbitrary")),
        name="nsa_slc_attn",
    )(act, scal, z, z, z, sel, pos_b)


def _win_attn_kernel(sc_ref, q_ref, k_ref, v_ref, pkb_ref, ocmp_ref, oslc_ref, gate_ref,
                     o_ref, vt_ref, kf_ref, *, t):
    g = pl.program_id(0)
    i = pl.program_id(1)

    @pl.when(i == 0)
    def _():
        _transpose_into(vt_ref, v_ref)
        _alibi_key_features(kf_ref, pkb_ref, LOG2E)

    span = t + WINDOW
    n_t = k_ref.shape[0] // t
    start = pl.multiple_of(jnp.clip(i - WINDOW // t, 0, n_t - span // t) * t, t)
    q = (q_ref[...].astype(F32) * (LOG2E * NSA_DH ** -0.5)).astype(BF16)
    lane = lax.broadcasted_iota(jnp.int32, (t, NSA_DH), 1)
    q4 = jnp.concatenate(
        [jnp.concatenate([q[:, r * NSA_DH:(r + 1) * NSA_DH],
                          jnp.where(lane < N_SPLIT, sc_ref[g * NSA_REP + r], 0.0).astype(BF16)], 1)
         for r in range(NSA_REP)], 0)
    kx = jnp.concatenate([k_ref[pl.ds(start, span), :], kf_ref[pl.ds(start, span), :]], 1)
    s = _kq(kx, q4)
    dif = (i * t + lax.broadcasted_iota(jnp.int32, (span, t), 1)
           - start - lax.broadcasted_iota(jnp.int32, (span, t), 0))
    ok = (dif >= 0) & (dif < WINDOW)
    vt = vt_ref[:, pl.ds(start, span)]
    gates = jax.nn.sigmoid(gate_ref[...].astype(F32))
    for r in range(NSA_REP):
        sr = jnp.where(ok, s[:, r * t:(r + 1) * t], NEG)
        e = jnp.exp2(sr - jnp.max(sr, 0, keepdims=True))
        d = jnp.sum(e, 0, keepdims=True)
        o_win = (jnp.dot(vt, e.astype(BF16), preferred_element_type=F32) * (1.0 / d)).T
        sl = slice(r * NSA_DH, (r + 1) * NSA_DH)

        def gcol(b):
            c0, c1 = 3 * r + b, 3 * (NSA_REP + r) + b
            return jnp.where(g == 0, gates[:, c0:c0 + 1], gates[:, c1:c1 + 1])

        o = (gcol(0) * ocmp_ref[:, sl].astype(F32) + gcol(1) * oslc_ref[:, sl].astype(F32)
             + gcol(2) * o_win)
        o_ref[:, sl] = o.astype(o_ref.dtype)


def _win_attn(z, o_cmp, o_slc, pos_b, scal):
    S = z.shape[0]
    t = 256
    gw = NSA_REP * NSA_DH
    kcol = Z_NKV // LANE + 8
    return pl.pallas_call(
        functools.partial(_win_attn_kernel, t=t),
        out_shape=jax.ShapeDtypeStruct((S, NSA_HEADS * NSA_DH), BF16),
        grid=(NSA_KV_HEADS, S // t),
        in_specs=[pl.BlockSpec(memory_space=pltpu.SMEM),
                  pl.BlockSpec((t, gw), lambda g, i: (i, Z_NQ // gw + g)),
                  pl.BlockSpec((S, NSA_DH), lambda g, i: (0, kcol + g)),
                  pl.BlockSpec((S, NSA_DH), lambda g, i: (0, kcol + 2 + g)),
                  pl.BlockSpec((S, LANE), lambda g, i: (0, 0)),
                  pl.BlockSpec((t, gw), lambda g, i: (i, g)),
                  pl.BlockSpec((t, gw), lambda g, i: (i, g)),
                  pl.BlockSpec((t, LANE), lambda g, i: (i, Z_NG // LANE))],
        out_specs=pl.BlockSpec((t, gw), lambda g, i: (i, g)),
        scratch_shapes=[pltpu.VMEM((NSA_DH, S), BF16), pltpu.VMEM((S, LANE), BF16)],
        compiler_params=_cparams(("arbitrary", "arbitrary")),
        name="nsa_win_attn",
    )(scal, z, z, z, pos_b, o_cmp, o_slc, z)


def kernel(x, c, positions, w_ada, b_ada, w_in, mla_q_norm, mla_kv_norm, mla_w_uq, mla_w_ukv, diff_lambda, diff_subln, nsa_cmp_w1, nsa_cmp_b1, nsa_cmp_w2, w_branch, w_out, ln1_g, ln1_b, w_ff1, w_ff2, ln2_g, ln2_b):
    B, S, D = x.shape
    L = w_ada.shape[0]
    assert B == 1 and D == D_MODEL and S % 512 == 0 and S // SLC_BLOCK >= 3

    w_in_x = _relayout_w_in(w_in)
    w_uq_x = _relayout_w_uq(mla_w_uq)
    w_ukv_b = mla_w_ukv.astype(BF16)
    w_out_b = w_out.astype(BF16)
    w_ff2_b = w_ff2.astype(BF16)

    slopes = 2.0 ** (-8.0 * np.arange(1, N_ALIBI_HEADS + 1, dtype=np.float64) / N_ALIBI_HEADS)
    s_diff, s_nsa = slopes[0::2], slopes[1::2]
    nsa_scal = jnp.asarray(s_nsa, F32)

    pos_f = positions.astype(F32)
    pos_col = pos_f.reshape(S, 1)
    pos_b = jnp.broadcast_to(pos_col, (S, LANE))
    pos_cmp_b = jnp.broadcast_to(jnp.pad(pos_f[0, CMP_BLOCK - 1::CMP_STRIDE], (0, 1))[:, None],
                                 (S // CMP_STRIDE, LANE))

    mod = _ada(c, w_ada, b_ada)
    rope_t = _rope_table(pos_col)
    xs = x.reshape(S, D)
    h = _lnmod(xs, mod, 0, 0, 1)

    for l in range(L):
        lam_init = 0.8 - 0.6 * math.exp(-0.3 * l)
        diff_scal = jnp.asarray(np.concatenate([s_diff * LOG2E, [lam_init]]), F32)

        z = _matmul(h, w_in_x, l, tm=1024, tn=1280, relu2=False, name="in_proj", w_transposed=True)

        qm, km, vm = _mla_proj(z, rope_t, mla_q_norm, mla_kv_norm, w_uq_x, w_ukv_b, l)
        o_mla = _mla_attn(qm, km, vm)

        o_diff = _diff_attn(z, pos_b, diff_scal, diff_lambda, diff_subln, l)

        xc = z[:, Z_NKV:Z_NKV + 4 * NSA_DH].reshape(S // CMP_STRIDE, CMP_STRIDE, 4, NSA_DH)
        xc = xc.transpose(2, 0, 1, 3).reshape(4, S // CMP_STRIDE, CMP_STRIDE * NSA_DH)
        cmp_kv = _compress(xc, nsa_cmp_w1, nsa_cmp_b1, nsa_cmp_w2, l)
        o_cmp, sel, act = _cmp_sel(z, cmp_kv, pos_cmp_b, nsa_scal)
        o_slc = _slc_attn(z, sel, act, pos_b, nsa_scal)
        o_nsa = _win_attn(z, o_cmp, o_slc, pos_b, nsa_scal)

        merged = _merge(o_mla, o_diff, o_nsa, w_branch, z, l)
        xs, h = _matmul_res_ln(merged, w_out_b, xs, mod, ln1_g, ln1_b, l, 2, ALPHA, "out_proj_ln",
                               next_mod=(l, 3, 4))
        u = _matmul(h, w_ff1, l, tm=1024, tn=1024, relu2=True, name="ff1")
        if l + 1 < L:
            xs, h = _matmul_res_ln(u, w_ff2_b, xs, mod, ln2_g, ln2_b, l, 5, ALPHA, "ff2_ln",
                                   next_mod=(l + 1, 0, 1))
        else:
            xs = _matmul_res_ln(u, w_ff2_b, xs, mod, ln2_g, ln2_b, l, 5, ALPHA, "ff2_ln")

    return xs.reshape(B, S, D)
```

```python
import functools
import math

import numpy as np
import jax
import jax.numpy as jnp
from jax import lax
from jax.experimental import pallas as pl
from jax.experimental.pallas import tpu as pltpu

F32 = jnp.float32
BF16 = jnp.bfloat16

D_MODEL = 2048
MLA_HEADS = 8
MLA_Q_LORA = 512
MLA_KV_LORA = 256
MLA_NOPE = 128
MLA_ROPE = 64
MLA_V = 128
ROPE_THETA = 10000.0
DIFF_HEADS = 8
DIFF_QK = 64
DIFF_V = 128
NSA_HEADS = 8
NSA_KV_HEADS = 2
NSA_REP = NSA_HEADS // NSA_KV_HEADS
NSA_DH = 128
CMP_BLOCK = 32
CMP_STRIDE = 16
SLC_BLOCK = 64
N_SELECT = 16
WINDOW = 512
N_ALIBI_HEADS = DIFF_HEADS + NSA_HEADS
N_BRANCH = 3
BRANCH_WIDTH = 1024
D_FF = 4 * D_MODEL
N_MOD = 6
DEPTH = 4
ALPHA = (2.0 * DEPTH) ** 0.25

LANE = 128
NEG = -1e30
LOG2E = math.log2(math.e)
T_CHUNK = 512
N_SPLIT = 3
PICKED = -2.0
SLC_TQ, SLC_TK = 512, 256
VMEM_LIMIT = 48 * 1024 * 1024

Z_CQ = 0
Z_DQ = 512
Z_DK = 1536
Z_DV = 2560
Z_NQ = 3584
Z_NKV = 4608
Z_MG = 6144
Z_CKV = 12288
Z_KR = 12544
Z_NG = 12672
Z_W = 12800


def _cparams(sem, vmem=VMEM_LIMIT):
    return pltpu.CompilerParams(dimension_semantics=sem, vmem_limit_bytes=vmem)


def _rot_half(w):
    half = MLA_ROPE // 2
    return jnp.concatenate([-w[..., half:], w[..., :half]], axis=-1)


def _relayout_w_in_kernel(wt_ref, o_ref):
    o = np.cumsum([0, 512, 256, 64, 1024, 1024, 1024, 1024, 1536, 24, N_BRANCH * D_MODEL])
    x = wt_ref[...]
    cq, ckv, kr, dq, dk, dv, nq, nkv, ng, mg = (x[o[k]:o[k + 1], :] for k in range(10))
    half = MLA_ROPE // 2
    kr_rot = jnp.concatenate([-kr[half:, :], kr[:half, :]], axis=0)
    pad = jnp.zeros((Z_W - Z_NG - ng.shape[0], x.shape[1]), x.dtype)
    o_ref[...] = jnp.concatenate([cq, dq, dk, dv, nq, nkv, mg, ckv, kr, kr_rot, ng, pad],
                                 axis=0).astype(o_ref.dtype)


def _relayout_w_in(w_in):
    L, D, N = w_in.shape
    tc = 256
    return pl.pallas_call(
        _relayout_w_in_kernel,
        out_shape=jax.ShapeDtypeStruct((L, Z_W, D), BF16),
        grid=(L, D // tc),
        in_specs=[pl.BlockSpec((None, N, tc), lambda l, i: (l, 0, i))],
        out_specs=pl.BlockSpec((None, Z_W, tc), lambda l, i: (l, 0, i)),
        compiler_params=_cparams(("arbitrary", "arbitrary")),
        name="w_in_relayout",
    )(jnp.swapaxes(w_in, 1, 2))


def _relayout_w_uq(w_uq):
    L, K, _ = w_uq.shape
    w = w_uq.reshape(L, K, MLA_HEADS, MLA_NOPE + MLA_ROPE)
    rope = w[..., MLA_NOPE:]
    return jnp.concatenate([w, _rot_half(rope)], axis=-1).reshape(L, K, MLA_HEADS * 256).astype(BF16)


def _ada_kernel(c_ref, w_ref, b_ref, o_ref):
    c = c_ref[...]
    ca = c * jax.nn.sigmoid(c)
    ca8 = jnp.broadcast_to(ca, (8, c.shape[1])).astype(BF16)
    r = jnp.dot(ca8, w_ref[...].astype(BF16), preferred_element_type=F32)
    o_ref[...] = r[0:1, :] + b_ref[...]


def _ada(c, w_ada, b_ada):
    L, D, N = w_ada.shape
    tn = 1536
    return pl.pallas_call(
        _ada_kernel,
        out_shape=jax.ShapeDtypeStruct((L, 1, N), F32),
        grid=(L, N // tn),
        in_specs=[pl.BlockSpec((1, D), lambda l, j: (0, 0)),
                  pl.BlockSpec((None, D, tn), lambda l, j: (l, 0, j)),
                  pl.BlockSpec((None, 1, tn), lambda l, j: (l, 0, j))],
        out_specs=pl.BlockSpec((None, 1, tn), lambda l, j: (l, 0, j)),
        compiler_params=_cparams(("arbitrary", "arbitrary")),
        name="ada_mod",
    )(c, w_ada, b_ada.reshape(L, 1, N))


def _lnmod_kernel(x_ref, sh_ref, sc_ref, o_ref):
    x = x_ref[...]
    mu = jnp.mean(x, -1, keepdims=True)
    xc = x - mu
    var = jnp.mean(xc * xc, -1, keepdims=True)
    y = xc * lax.rsqrt(var + 1e-5)
    o_ref[...] = (y * (1.0 + sc_ref[...]) + sh_ref[...]).astype(o_ref.dtype)


def _lnmod(x, mod, l, k_shift, k_scale):
    S, D = x.shape
    tm = 512
    return pl.pallas_call(
        _lnmod_kernel,
        out_shape=jax.ShapeDtypeStruct((S, D), BF16),
        grid=(S // tm,),
        in_specs=[pl.BlockSpec((tm, D), lambda i: (i, 0)),
                  pl.BlockSpec((None, 1, D), lambda i: (l, 0, k_shift)),
                  pl.BlockSpec((None, 1, D), lambda i: (l, 0, k_scale))],
        out_specs=pl.BlockSpec((tm, D), lambda i: (i, 0)),
        compiler_params=_cparams(("arbitrary",)),
        name="ln_modulate",
    )(x, mod, mod)


def _mm_kernel(x_ref, w_ref, o_ref, *, relu2, w_transposed):
    w = w_ref[...].astype(BF16)
    contract = (((1,), (1,)), ((), ())) if w_transposed else (((1,), (0,)), ((), ()))
    r = lax.dot_general(x_ref[...], w, contract, preferred_element_type=F32)
    if relu2:
        r = jnp.square(jnp.maximum(r, 0.0))
    o_ref[...] = r.astype(o_ref.dtype)


def _matmul(x, w, l, *, tm, tn, relu2, name, w_transposed=False, out_dtype=BF16):
    S, K = x.shape
    if w_transposed:
        N = w.shape[1]
        w_spec = pl.BlockSpec((None, tn, K), lambda i, j: (l, j, 0))
    else:
        N = w.shape[2]
        w_spec = pl.BlockSpec((None, K, tn), lambda i, j: (l, 0, j))
    return pl.pallas_call(
        functools.partial(_mm_kernel, relu2=relu2, w_transposed=w_transposed),
        out_shape=jax.ShapeDtypeStruct((S, N), out_dtype),
        grid=(S // tm, N // tn),
        in_specs=[pl.BlockSpec((tm, K), lambda i, j: (i, 0)), w_spec],
        out_specs=pl.BlockSpec((tm, tn), lambda i, j: (i, j)),
        compiler_params=_cparams(("arbitrary", "arbitrary")),
        name=name,
    )(x, w)


def _ln_plain(y):
    mu = jnp.mean(y, -1, keepdims=True)
    yc = y - mu
    return yc * lax.rsqrt(jnp.mean(yc * yc, -1, keepdims=True) + 1e-5)


def _mm_res_ln_kernel(a_ref, w_ref, x_ref, gate_ref, lng_ref, lnb_ref, *rest, nk, alpha, with_next):
    o_ref = rest[2] if with_next else rest[0]
    k = pl.program_id(1)
    part = jnp.dot(a_ref[...], w_ref[...], preferred_element_type=F32)

    @pl.when(k == 0)
    def _():
        o_ref[...] = part

    @pl.when(k > 0)
    def _():
        o_ref[...] += part

    @pl.when(k == nk - 1)
    def _():
        x_new = _ln_plain(alpha * x_ref[...] + gate_ref[...] * o_ref[...]) * lng_ref[...] + lnb_ref[...]
        o_ref[...] = x_new
        if with_next:
            sh_ref, sc_ref, _, h_ref = rest
            h_ref[...] = (_ln_plain(x_new) * (1.0 + sc_ref[...]) + sh_ref[...]).astype(h_ref.dtype)


def _res_ln_kernel(f_ref, x_ref, gate_ref, lng_ref, lnb_ref, *rest, alpha, with_next):
    x_new = _ln_plain(alpha * x_ref[...] + gate_ref[...] * f_ref[...].astype(F32)) * lng_ref[...] + lnb_ref[...]
    if with_next:
        sh_ref, sc_ref, o_ref, h_ref = rest
        h_ref[...] = (_ln_plain(x_new) * (1.0 + sc_ref[...]) + sh_ref[...]).astype(h_ref.dtype)
    else:
        o_ref, = rest
    o_ref[...] = x_new


def _res_ln(f, x, mod, lng, lnb, l, k_gate, alpha, name, next_mod=None):
    S, D = x.shape
    L = lng.shape[0]
    tm = 512
    vec = lambda ll, kk: pl.BlockSpec((None, 1, D), lambda i: (ll, 0, kk))
    row = pl.BlockSpec((tm, D), lambda i: (i, 0))
    in_specs = [row, row, vec(l, k_gate), vec(l, 0), vec(l, 0)]
    args = [f, x, mod, lng.reshape(L, 1, D), lnb.reshape(L, 1, D)]
    out_shape, out_specs = jax.ShapeDtypeStruct((S, D), F32), row
    if next_mod is not None:
        in_specs += [vec(next_mod[0], next_mod[1]), vec(next_mod[0], next_mod[2])]
        args += [mod, mod]
        out_shape, out_specs = (out_shape, jax.ShapeDtypeStruct((S, D), BF16)), (row, row)
    return pl.pallas_call(
        functools.partial(_res_ln_kernel, alpha=alpha, with_next=next_mod is not None),
        out_shape=out_shape, grid=(S // tm,), in_specs=in_specs, out_specs=out_specs,
        compiler_params=_cparams(("arbitrary",)), name=name,
    )(*args)


def _matmul_res_ln(a, w, x, mod, lng, lnb, l, k_gate, alpha, name, next_mod=None):
    S, K = a.shape
    D = x.shape[1]
    L = lng.shape[0]
    tm, tk = 512, 2048
    nk = K // tk
    vmem = VMEM_LIMIT + (2 * tm * D * 2 if next_mod is not None else 0)
    vec =lambda ll, kk: pl.BlockSpec((None, 1, D), lambda i, k: (ll, 0, kk))
    row = pl.BlockSpec((tm, D), lambda i, k: (i, 0))
    in_specs = [pl.BlockSpec((tm, tk), lambda i, k: (i, k)),
                pl.BlockSpec((None, tk, D), lambda i, k: (l, k, 0)),
                row, vec(l, k_gate), vec(l, 0), vec(l, 0)]
    args = [a, w, x, mod, lng.reshape(L, 1, D), lnb.reshape(L, 1, D)]
    out_shape, out_specs = jax.ShapeDtypeStruct((S, D), F32), row
    if next_mod is not None:
        in_specs += [vec(next_mod[0], next_mod[1]), vec(next_mod[0], next_mod[2])]
        args += [mod, mod]
        out_shape, out_specs = (out_shape, jax.ShapeDtypeStruct((S, D), BF16)), (row, row)
    return pl.pallas_call(
        functools.partial(_mm_res_ln_kernel, nk=nk, alpha=alpha, with_next=next_mod is not None),
        out_shape=out_shape,
        grid=(S // tm, nk),
        in_specs=in_specs,
        out_specs=out_specs,
        compiler_params=_cparams(("arbitrary", "arbitrary"), vmem),
        name=name,
    )(*args)


def _merge_kernel(o0_ref, o1_ref, o2_ref, w_ref, g0_ref, g1_ref, g2_ref, out_ref):
    acc = None
    for n, (o_ref, g_ref) in enumerate(((o0_ref, g0_ref), (o1_ref, g1_ref), (o2_ref, g2_ref))):
        t = jax.nn.sigmoid(g_ref[...].astype(F32)) * jnp.dot(o_ref[...], w_ref[n].astype(BF16),
                                                               preferred_element_type=F32)
        acc = t if acc is None else acc + t
    out_ref[...] = acc.astype(out_ref.dtype)


def _merge(o_mla, o_diff, o_nsa, w_branch, z, l):
    S = o_mla.shape[0]
    tm, tn = 1024, 512
    bspec = pl.BlockSpec((tm, BRANCH_WIDTH), lambda i, j: (i, 0))

    def gspec(n):
        return pl.BlockSpec((tm, tn), lambda i, j: (i, (Z_MG + D_MODEL * n) // tn + j))

    return pl.pallas_call(
        _merge_kernel,
        out_shape=jax.ShapeDtypeStruct((S, D_MODEL), BF16),
        grid=(S // tm, D_MODEL // tn),
        in_specs=[bspec, bspec, bspec,
                  pl.BlockSpec((None, N_BRANCH, BRANCH_WIDTH, tn), lambda i, j: (l, 0, 0, j)),
                  gspec(0), gspec(1), gspec(2)],
        out_specs=pl.BlockSpec((tm, tn), lambda i, j: (i, j)),
        compiler_params=_cparams(("arbitrary", "arbitrary")),
        name="branch_merge",
    )(o_mla, o_diff, o_nsa, w_branch, z, z, z)


def _rope_kernel(pos_ref, inv_ref, o_ref):
    ang = pos_ref[...] * inv_ref[...]
    lane = lax.broadcasted_iota(jnp.int32, ang.shape, 1)
    o_ref[...] = jnp.where(lane < MLA_ROPE, jnp.cos(ang), jnp.sin(ang))


def _rope_table(pos_col):
    S = pos_col.shape[0]
    inv = ROPE_THETA ** (-np.arange(0, MLA_ROPE, 2, dtype=np.float64) / MLA_ROPE)
    inv4 = jnp.asarray(np.tile(inv, 4)[None, :], F32)
    tm = 512
    return pl.pallas_call(
        _rope_kernel,
        out_shape=jax.ShapeDtypeStruct((S, LANE), F32),
        grid=(S // tm,),
        in_specs=[pl.BlockSpec((tm, 1), lambda i: (i, 0)),
                  pl.BlockSpec((1, LANE), lambda i: (0, 0))],
        out_specs=pl.BlockSpec((tm, LANE), lambda i: (i, 0)),
        compiler_params=_cparams(("arbitrary",)),
        name="rope_table",
    )(pos_col, inv4)


def _rms(x, g):
    return x * lax.rsqrt(jnp.mean(x * x, -1, keepdims=True) + 1e-6) * g


def _mla_proj_kernel(cq_ref, ckv_ref, kr_ref, t_ref, gq_ref, gkv_ref, wuq_ref, wukv_ref,
                     q_out, k_out, v_out):
    scale = LOG2E * (MLA_NOPE + MLA_ROPE) ** -0.5
    t = t_ref[...]
    nq = _rms(cq_ref[...].astype(F32), gq_ref[...]).astype(BF16)
    q = jnp.dot(nq, wuq_ref[...], preferred_element_type=F32)
    for h in range(MLA_HEADS):
        c = 256 * h
        q_out[:, c:c + 128] = (q[:, c:c + 128] * scale).astype(BF16)
        q_out[:, c + 128:c + 256] = (q[:, c + 128:c + 256] * t * scale).astype(BF16)
    nkv = _rms(ckv_ref[...].astype(F32), gkv_ref[...]).astype(BF16)
    kv = jnp.dot(nkv, wukv_ref[...], preferred_element_type=F32)
    kr = kr_ref[...].astype(F32) * t
    kr2 = (kr + pltpu.roll(kr, MLA_ROPE, 1)).astype(BF16)
    for h in range(MLA_HEADS):
        c = 256 * h
        k_out[:, c:c + 128] = kv[:, c:c + 128].astype(BF16)
        k_out[:, c + 128:c + 256] = kr2
        v_out[:, 128 * h:128 * h + 128] = kv[:, c + 128:c + 256].astype(BF16)


def _mla_proj(z, rope_t, gq, gkv, wuq, wukv, l):
    S = z.shape[0]
    L = gq.shape[0]
    tm = 512
    HW = MLA_HEADS * 256
    return pl.pallas_call(
        _mla_proj_kernel,
        out_shape=(jax.ShapeDtypeStruct((S, HW), BF16),
                   jax.ShapeDtypeStruct((S, HW), BF16),
                   jax.ShapeDtypeStruct((S, MLA_HEADS * MLA_V), BF16)),
        grid=(S // tm,),
        in_specs=[pl.BlockSpec((tm, MLA_Q_LORA), lambda i: (i, Z_CQ // MLA_Q_LORA)),
                  pl.BlockSpec((tm, MLA_KV_LORA), lambda i: (i, Z_CKV // MLA_KV_LORA)),
                  pl.BlockSpec((tm, LANE), lambda i: (i, Z_KR // LANE)),
                  pl.BlockSpec((tm, LANE), lambda i: (i, 0)),
                  pl.BlockSpec((None, 1, MLA_Q_LORA), lambda i: (l, 0, 0)),
                  pl.BlockSpec((None, 1, MLA_KV_LORA), lambda i: (l, 0, 0)),
                  pl.BlockSpec((None, MLA_Q_LORA, HW), lambda i: (l, 0, 0)),
                  pl.BlockSpec((None, MLA_KV_LORA, HW), lambda i: (l, 0, 0))],
        out_specs=(pl.BlockSpec((tm, HW), lambda i: (i, 0)),
                   pl.BlockSpec((tm, HW), lambda i: (i, 0)),
                   pl.BlockSpec((tm, MLA_HEADS * MLA_V), lambda i: (i, 0))),
        compiler_params=_cparams(("arbitrary",)),
        name="mla_proj",
    )(z, z, z, rope_t, gq.reshape(L, 1, -1), gkv.reshape(L, 1, -1), wuq, wukv)


def _qk(q, k):
    return lax.dot_general(q, k, (((1,), (1,)), ((), ())), preferred_element_type=F32)


def _online(s, v, m, l, acc, valid=None):
    m_new = jnp.maximum(m, jnp.max(s, -1, keepdims=True))
    a = jnp.exp(m - m_new)
    p = jnp.exp(s - m_new)
    if valid is not None:
        p = jnp.where(valid, p, 0.0)
    l = a * l + jnp.sum(p, -1, keepdims=True)
    acc = a * acc + jnp.dot(p.astype(BF16), v, preferred_element_type=F32)
    return m_new, l, acc


def _init(rows, dv):
    return (jnp.full((rows, 1), NEG, F32), jnp.zeros((rows, 1), F32), jnp.zeros((rows, dv), F32))


def _kq(k, q):
    return lax.dot_general(k, q, (((1,), (1,)), ((), ())), preferred_element_type=F32)


def _online_t(s, vt, m, l, acc):
    m_new = jnp.maximum(m, jnp.max(s, 0, keepdims=True))
    a = jnp.exp2(m - m_new)
    p = jnp.exp2(s - m_new)
    l = a * l + jnp.sum(p, 0, keepdims=True)
    acc = a * acc + jnp.dot(vt, p.astype(BF16), preferred_element_type=F32)
    return m_new, l, acc


def _init_t(nq, dv):
    return (jnp.full((1, nq), NEG, F32), jnp.zeros((1, nq), F32), jnp.zeros((dv, nq), F32))


def _transpose_into(vt_ref, v_ref):
    for c in range(v_ref.shape[0] // T_CHUNK):
        rows = slice(c * T_CHUNK, (c + 1) * T_CHUNK)
        vt_ref[:, rows] = v_ref[rows, :].astype(F32).T.astype(vt_ref.dtype)


def _put_scores(buf, t, lanes=None):
    s_ref, m_ref = buf
    if lanes is None:
        s_ref[...] = t
        m_ref[...] = jnp.max(t, 0, keepdims=True)
    else:
        s_ref[:, lanes] = t
        m_ref[:, lanes] = jnp.max(t, 0, keepdims=True)


def _sweep(n_pairs, tk, score, vt_ref, carry, buf_a, buf_b, jd=None, tile_of=lambda n: n):
    def update(j, buf, carry):
        s_ref, mt_ref = buf
        m, l, acc = carry
        m_new = jnp.maximum(m, mt_ref[...])
        a = jnp.exp2(m - m_new)
        p = jnp.exp2(s_ref[...] - m_new)
        l = a * l + jnp.sum(p, 0, keepdims=True)
        vt = vt_ref[:, pl.ds(pl.multiple_of(j * tk, tk), tk)]
        acc = a * acc + jnp.dot(vt, p.astype(BF16), preferred_element_type=F32)
        return m_new, l, acc

    if jd is None:
        jd = 2 * n_pairs
    score(jd, buf_a, True)
    score(jd + 1, buf_b, True)
    carry = update(jd, buf_a, carry)

    def pair(p, carry):
        ja, jb = tile_of(2 * p), tile_of(2 * p + 1)
        score(ja, buf_a, False)
        carry = update(jnp.where(p == 0, jd + 1, tile_of(jnp.maximum(2 * p - 1, 0))), buf_b, carry)
        score(jb, buf_b, False)
        return update(ja, buf_a, carry)

    carry = lax.fori_loop(0, n_pairs, pair, carry)
    last = jnp.where(n_pairs == 0, jd + 1, tile_of(jnp.maximum(2 * n_pairs - 1, 0)))
    return update(last, buf_b, carry)


def _score_bufs(tk, nq):
    return [pltpu.VMEM((tk, nq), F32), pltpu.VMEM((1, nq), F32),
            pltpu.VMEM((tk, nq), F32), pltpu.VMEM((1, nq), F32)]


def _mla_attn_kernel(q_ref, k_ref, v_ref, o_ref, vt_ref, sa_ref, ma_ref, sb_ref, mb_ref, *, tq, tk):
    i = pl.program_id(1)

    @pl.when(i == 0)
    def _():
        _transpose_into(vt_ref, v_ref)

    q = q_ref[...]

    def score(j, buf, masked):
        off = pl.multiple_of(j * tk, tk)
        t = _kq(k_ref[pl.ds(off, tk), :], q)
        if masked:
            kidx = off + lax.broadcasted_iota(jnp.int32, (tk, tq), 0)
            qidx = i * tq + lax.broadcasted_iota(jnp.int32, (tk, tq), 1)
            t = jnp.where(kidx <= qidx, t, NEG)
        _put_scores(buf, t)

    _, l, acc = _sweep(i, tk, score, vt_ref, _init_t(tq, MLA_V), (sa_ref, ma_ref), (sb_ref, mb_ref))
    o_ref[...] = (acc / l).T.astype(o_ref.dtype)


def _mla_attn(qm, km, vm):
    S = qm.shape[0]
    tq, tk = 1024, 512
    return pl.pallas_call(
        functools.partial(_mla_attn_kernel, tq=tq, tk=tk),
        out_shape=jax.ShapeDtypeStruct((S, MLA_HEADS * MLA_V), BF16),
        grid=(MLA_HEADS, S // tq),
        in_specs=[pl.BlockSpec((tq, 256), lambda h, i: (i, h)),
                  pl.BlockSpec((S, 256), lambda h, i: (0, h)),
                  pl.BlockSpec((S, MLA_V), lambda h, i: (0, h))],
        out_specs=pl.BlockSpec((tq, MLA_V), lambda h, i: (i, h)),
        scratch_shapes=[pltpu.VMEM((MLA_V, S), BF16)] + _score_bufs(tk, tq),
        compiler_params=_cparams(("arbitrary", "arbitrary")),
        name="mla_attn",
    )(qm, km, vm)


def _alibi_features(pkb, coef):
    hi, mid, lo = _split3(coef * pkb)
    lane = lax.broadcasted_iota(jnp.int32, hi.shape, 1)
    zero = jnp.zeros_like(hi)
    return jnp.where(lane == 0, hi, jnp.where(lane == 1, mid, jnp.where(lane == 2, lo, zero)))


def _alibi_key_features(kf_ref, pkb_ref, coef):
    for c in range(pkb_ref.shape[0] // T_CHUNK):
        rows = slice(c * T_CHUNK, (c + 1) * T_CHUNK)
        kf_ref[rows, :] = _alibi_features(pkb_ref[rows, :], coef)


def _diff_attn_kernel(sc_ref, q_ref, k_ref, v_ref, pkb_ref, lam_ref, g_ref, o_ref,
                      vt_ref, kf_ref, sa_ref, ma_ref, sb_ref, mb_ref, *, tq, tk):
    h = pl.program_id(0)
    i = pl.program_id(1)

    @pl.when(i == 0)
    def _():
        _transpose_into(vt_ref, v_ref)
        _alibi_key_features(kf_ref, pkb_ref, sc_ref[h])

    lam_init = sc_ref[DIFF_HEADS]
    qs = (q_ref[...].astype(F32) * (LOG2E * DIFF_QK ** -0.5)).astype(BF16)
    lane = lax.broadcasted_iota(jnp.int32, qs.shape, 1)
    zero = jnp.zeros_like(qs)
    ones = jnp.where(lane < N_SPLIT, 1.0, 0.0).astype(BF16)
    q2 = jnp.concatenate([jnp.concatenate([jnp.where(lane < DIFF_QK, qs, zero), ones], 1),
                          jnp.concatenate([jnp.where(lane >= DIFF_QK, qs, zero), ones], 1)], 0)

    def score(j, buf, masked):
        off = pl.multiple_of(j * tk, tk)
        kx = jnp.concatenate([k_ref[pl.ds(off, tk), :], kf_ref[pl.ds(off, tk), :]], 1)
        t = _kq(kx, q2)
        if masked:
            kidx = off + lax.broadcasted_iota(jnp.int32, (tk, tq), 0)
            qidx = i * tq + lax.broadcasted_iota(jnp.int32, (tk, tq), 1)
            ok = kidx <= qidx
            for c in range(2):
                lanes = slice(c * tq, (c + 1) * tq)
                _put_scores(buf, jnp.where(ok, t[:, lanes], NEG), lanes)
        else:
            _put_scores(buf, t)

    _, l, acc = _sweep(i, tk, score, vt_ref, _init_t(2 * tq, DIFF_V), (sa_ref, ma_ref), (sb_ref, mb_ref))
    on = acc / l
    lp = lam_ref[...]
    lam = (jnp.exp(jnp.sum(lp[0:1] * lp[1:2], -1, keepdims=True))
           - jnp.exp(jnp.sum(lp[2:3] * lp[3:4], -1, keepdims=True)) + lam_init)
    o = on[:, :tq] - lam * on[:, tq:]
    o = o * lax.rsqrt(jnp.mean(o * o, 0, keepdims=True) + 1e-6)
    o_ref[...] = (o.T * g_ref[...] * (1.0 - lam_init)).astype(o_ref.dtype)


def _diff_attn(z, pos_b, scal, lam_p, g_sub, l):
    S = z.shape[0]
    L = g_sub.shape[0]
    tq, tk = 1024, 512
    return pl.pallas_call(
        functools.partial(_diff_attn_kernel, tq=tq, tk=tk),
        out_shape=jax.ShapeDtypeStruct((S, DIFF_HEADS * DIFF_V), BF16),
        grid=(DIFF_HEADS, S // tq),
        in_specs=[pl.BlockSpec(memory_space=pltpu.SMEM),
                  pl.BlockSpec((tq, LANE), lambda h, i: (i, Z_DQ // LANE + h)),
                  pl.BlockSpec((S, LANE), lambda h, i: (0, Z_DK // LANE + h)),
                  pl.BlockSpec((S, LANE), lambda h, i: (0, Z_DV // LANE + h)),
                  pl.BlockSpec((S, LANE), lambda h, i: (0, 0)),
                  pl.BlockSpec((None, 4, DIFF_QK), lambda h, i: (l, 0, 0)),
                  pl.BlockSpec((None, 1, DIFF_V), lambda h, i: (l, 0, 0))],
        out_specs=pl.BlockSpec((tq, DIFF_V), lambda h, i: (i, h)),
        scratch_shapes=[pltpu.VMEM((DIFF_V, S), BF16), pltpu.VMEM((S, LANE), BF16)]
        + _score_bufs(tk, 2 * tq),
        compiler_params=_cparams(("arbitrary", "arbitrary")),
        name="diff_attn",
    )(scal, z, z, z, pos_b, lam_p, g_sub.reshape(L, 1, -1))


def _compress_kernel(x_ref, w1_ref, b1_ref, w2_ref, o_ref):
    x = x_ref[...]
    half = CMP_STRIDE * NSA_DH
    a = jnp.dot(x, w1_ref[0:half, :].astype(BF16), preferred_element_type=F32)
    b = jnp.dot(x, w1_ref[half:2 * half, :].astype(BF16), preferred_element_type=F32)
    n = x.shape[0]
    pre = a + pltpu.roll(b, n - 1, 0) + b1_ref[...]
    hdn = 0.5 * pre * (1.0 + jnp.tanh(0.7978845608028654 * (pre + 0.044715 * pre * pre * pre)))
    o_ref[...] = jnp.dot(hdn.astype(BF16), w2_ref[...].astype(BF16),
                         preferred_element_type=F32).astype(o_ref.dtype)


def _compress(xc, w1, b1, w2, l):
    _, nb, kk = xc.shape
    L = w1.shape[0]
    return pl.pallas_call(
        _compress_kernel,
        out_shape=jax.ShapeDtypeStruct((4, nb, NSA_DH), BF16),
        grid=(4,),
        in_specs=[pl.BlockSpec((None, nb, kk), lambda c: (c, 0, 0)),
                  pl.BlockSpec((None, None, 2 * kk, NSA_DH), lambda c: (l, c // 2, 0, 0)),
                  pl.BlockSpec((None, None, 1, NSA_DH), lambda c: (l, c // 2, 0, 0)),
                  pl.BlockSpec((None, None, NSA_DH, NSA_DH), lambda c: (l, c // 2, 0, 0))],
        out_specs=pl.BlockSpec((None, nb, NSA_DH), lambda c: (c, 0, 0)),
        compiler_params=_cparams(("arbitrary",)),
        name="nsa_compress",
    )(xc, w1, b1.reshape(L, 2, 1, NSA_DH), w2)


def _split3(x):
    hi = x.astype(BF16)
    r = x - hi.astype(F32)
    mid = r.astype(BF16)
    lo = (r - mid.astype(F32)).astype(BF16)
    return hi, mid, lo


def _cmp_sel_kernel(sc_ref, q_ref, kc_ref, vc_ref, pcb_ref, o_ref, selt_ref, act_ref, *, tq, n_slc):
    g = pl.program_id(0)
    i = pl.program_id(1)
    nc = kc_ref.shape[0]
    q = (q_ref[...].astype(F32) * (LOG2E * NSA_DH ** -0.5)).astype(BF16)
    lane = lax.broadcasted_iota(jnp.int32, (tq, NSA_DH), 1)
    q4 = jnp.concatenate(
        [jnp.concatenate([q[:, r * NSA_DH:(r + 1) * NSA_DH],
                          jnp.where(lane < N_SPLIT, sc_ref[g * NSA_REP + r], 0.0).astype(BF16)], 1)
         for r in range(NSA_REP)], 0)
    kx = jnp.concatenate([kc_ref[...], _alibi_features(pcb_ref[...], LOG2E)], 1)
    t = _kq(kx, q4)
    vct = vc_ref[...].astype(F32).T.astype(BF16)
    qidx = i * tq + lax.broadcasted_iota(jnp.int32, (nc, tq), 1)
    cmp_end = lax.broadcasted_iota(jnp.int32, (nc, tq), 0) * CMP_STRIDE + (CMP_BLOCK - 1)
    vis = cmp_end <= qidx
    p_sum = jnp.zeros((nc, tq), F32)
    for r in range(NSA_REP):
        s = jnp.where(vis, t[:, r * tq:(r + 1) * tq], NEG)
        e = jnp.where(vis, jnp.exp2(s - jnp.max(s, 0, keepdims=True)), 0.0)
        d = jnp.sum(e, 0, keepdims=True)
        p = e * (1.0 / jnp.where(d > 0.0, d, 1.0))
        p_sum = p_sum + p
        o = jnp.dot(vct, p.astype(BF16), preferred_element_type=F32)
        o_ref[:, r * NSA_DH:(r + 1) * NSA_DH] = o.T.astype(o_ref.dtype)
    jj = lax.broadcasted_iota(jnp.int32, (n_slc, nc), 0)
    nn = lax.broadcasted_iota(jnp.int32, (n_slc, nc), 1)
    lo_n = jj * (SLC_BLOCK // CMP_STRIDE) - 1
    ovl = jnp.where((nn >= lo_n) & (nn <= lo_n + 4), 1.0, 0.0).astype(BF16)
    imp = sum(jnp.dot(ovl, part, preferred_element_type=F32) for part in _split3(p_sum))
    blk = lax.broadcasted_iota(jnp.int32, (n_slc, tq), 0).astype(F32)
    cur = lax.shift_right_logical(i * tq + lax.broadcasted_iota(jnp.int32, (n_slc, tq), 1),
                                  int(math.log2(SLC_BLOCK))).astype(F32)
    forced = (blk == 0.0) | (blk == cur) | (blk == cur - 1.0)
    causal = blk <= cur
    score = jnp.where(forced, 3e38, jnp.where(causal, imp, -1.0))
    for _ in range(min(N_SELECT, n_slc)):
        mx = jnp.max(score, 0, keepdims=True)
        first = jnp.min(jnp.where(score == mx, blk, float(n_slc)), 0, keepdims=True)
        score = jnp.where(blk == first, PICKED, score)
    sel = jnp.where(causal & (score == PICKED), 1.0, 0.0).astype(selt_ref.dtype)
    selt_ref[...] = sel
    n_kt = act_ref.shape[0]
    tile_of_blk = lax.shift_right_logical(lax.broadcasted_iota(jnp.int32, (n_kt, n_slc), 1),
                                          int(math.log2(n_slc // n_kt)))
    grp = jnp.where(tile_of_blk == lax.broadcasted_iota(jnp.int32, (n_kt, n_slc), 0), 1.0, 0.0).astype(BF16)
    hits = jnp.sum(jnp.dot(grp, sel, preferred_element_type=F32), 1, keepdims=True)
    act_ref[...] = (hits > 0.0).astype(jnp.int32)


def _cmp_sel(z, cmp_kv, pos_cmp_b, scal):
    S = z.shape[0]
    nc = cmp_kv.shape[1]
    n_slc = S // SLC_BLOCK
    tq = SLC_TQ
    n_kt = S // SLC_TK
    gw = NSA_REP * NSA_DH
    o_cmp, sel_t, act = pl.pallas_call(
        functools.partial(_cmp_sel_kernel, tq=tq, n_slc=n_slc),
        out_shape=(jax.ShapeDtypeStruct((S, NSA_HEADS * NSA_DH), BF16),
                   jax.ShapeDtypeStruct((NSA_KV_HEADS, n_slc, S), BF16),
                   jax.ShapeDtypeStruct((NSA_KV_HEADS, S // tq, n_kt, 1), jnp.int32)),
        grid=(NSA_KV_HEADS, S // tq),
        in_specs=[pl.BlockSpec(memory_space=pltpu.SMEM),
                  pl.BlockSpec((tq, gw), lambda g, i: (i, Z_NQ // gw + g)),
                  pl.BlockSpec((None, nc, NSA_DH), lambda g, i: (g, 0, 0)),
                  pl.BlockSpec((None, nc, NSA_DH), lambda g, i: (2 + g, 0, 0)),
                  pl.BlockSpec((nc, LANE), lambda g, i: (0, 0))],
        out_specs=(pl.BlockSpec((tq, gw), lambda g, i: (i, g)),
                   pl.BlockSpec((None, n_slc, tq), lambda g, i: (g, 0, i)),
                   pl.BlockSpec((None, None, n_kt, 1), lambda g, i: (g, i, 0, 0))),
        compiler_params=_cparams(("arbitrary", "arbitrary")),
        name="nsa_cmp_select",
    )(scal, z, cmp_kv, cmp_kv, pos_cmp_b)
    return o_cmp, sel_t, act.reshape(NSA_KV_HEADS, S // tq, n_kt)


def _slc_attn_kernel(act_ref, sc_ref, q_ref, k_ref, v_ref, sel_ref, pkb_ref, o_ref,
                     vt_ref, kf_ref, sa_ref, ma_ref, sb_ref, mb_ref, lst_ref, *, tq, tk):
    g = pl.program_id(0)
    i = pl.program_id(1)

    @pl.when(i == 0)
    def _():
        _transpose_into(vt_ref, v_ref)
        _alibi_key_features(kf_ref, pkb_ref, LOG2E)

    n_slc = sel_ref.shape[0]
    q = (q_ref[...].astype(F32) * (LOG2E * NSA_DH ** -0.5)).astype(BF16)
    lane = lax.broadcasted_iota(jnp.int32, (tq, NSA_DH), 1)
    q4 = jnp.concatenate(
        [jnp.concatenate([q[:, r * NSA_DH:(r + 1) * NSA_DH],
                          jnp.where(lane < N_SPLIT, sc_ref[g * NSA_REP + r], 0.0).astype(BF16)], 1)
         for r in range(NSA_REP)], 0)
    sel = sel_ref[...]
    blk = lax.broadcasted_iota(jnp.int32, (tk, n_slc), 1)
    kblk0 = lax.shift_right_logical(lax.broadcasted_iota(jnp.int32, (tk, n_slc), 0),
                                    int(math.log2(SLC_BLOCK)))

    def score(j, buf, masked):
        off = pl.multiple_of(j * tk, tk)
        kx = jnp.concatenate([k_ref[pl.ds(off, tk), :], kf_ref[pl.ds(off, tk), :]], 1)
        t = _kq(kx, q4)
        expand = jnp.where(blk == kblk0 + j * (tk // SLC_BLOCK), 1.0, 0.0).astype(BF16)
        ok = jnp.dot(expand, sel, preferred_element_type=F32) > 0.5
        if masked:
            kidx = off + lax.broadcasted_iota(jnp.int32, (tk, tq), 0)
            qidx = i * tq + lax.broadcasted_iota(jnp.int32, (tk, tq), 1)
            ok = ok & (kidx <= qidx)
        for r in range(NSA_REP):
            lanes = slice(r * tq, (r + 1) * tq)
            _put_scores(buf, jnp.where(ok, t[:, lanes], NEG), lanes)

    def compact(t, carry):
        n, spare = carry
        hit = act_ref[g, i, t] > 0
        lst_ref[n] = t
        return n + hit.astype(jnp.int32), jnp.where(hit, spare, t)

    n, spare = lax.fori_loop(0, 2 * i, compact, (jnp.int32(0), jnp.int32(0)))
    lst_ref[n] = spare
    n_pairs = lax.shift_right_logical(n + 1, 1)

    _, l, acc = _sweep(n_pairs, tk, score, vt_ref, _init_t(NSA_REP * tq, NSA_DH),
                       (sa_ref, ma_ref), (sb_ref, mb_ref), jd=2 * i, tile_of=lambda idx: lst_ref[idx])
    on = acc / l
    for r in range(NSA_REP):
        o_ref[:, r * NSA_DH:(r + 1) * NSA_DH] = on[:, r * tq:(r + 1) * tq].T.astype(o_ref.dtype)


def _slc_attn(z, sel, act, pos_b, scal):
    S = z.shape[0]
    n_slc = S // SLC_BLOCK
    tq, tk = SLC_TQ, SLC_TK
    gw = NSA_REP * NSA_DH
    kcol = Z_NKV // LANE + 4
    grid_spec = pltpu.PrefetchScalarGridSpec(
        num_scalar_prefetch=1,
        grid=(NSA_KV_HEADS, S // tq),
        in_specs=[pl.BlockSpec(memory_space=pltpu.SMEM),
                  pl.BlockSpec((tq, gw), lambda g, i, a: (i, Z_NQ // gw + g)),
                  pl.BlockSpec((S, NSA_DH), lambda g, i, a: (0, kcol + g)),
                  pl.BlockSpec((S, NSA_DH), lambda g, i, a: (0, kcol + 2 + g)),
                  pl.BlockSpec((None, n_slc, tq), lambda g, i, a: (g, 0, i)),
                  pl.BlockSpec((S, LANE), lambda g, i, a: (0, 0))],
        out_specs=pl.BlockSpec((tq, gw), lambda g, i, a: (i, g)),
        scratch_shapes=[pltpu.VMEM((NSA_DH, S), BF16), pltpu.VMEM((S, LANE), BF16)]
        + _score_bufs(tk, NSA_REP * tq) + [pltpu.SMEM((S // tk + 1,), jnp.int32)])
    return pl.pallas_call(
        functools.partial(_slc_attn_kernel, tq=tq, tk=tk),
        out_shape=jax.ShapeDtypeStruct((S, NSA_HEADS * NSA_DH), BF16),
        grid_spec=grid_spec,
        compiler_params=_cparams(("arbitrary", "arbitrary")),
        name="nsa_slc_attn",
    )(act, scal, z, z, z, sel, pos_b)


def _win_attn_kernel(sc_ref, q_ref, k_ref, v_ref, pkb_ref, ocmp_ref, oslc_ref, gate_ref,
                     o_ref, vt_ref, kf_ref, *, t):
    g = pl.program_id(0)
    i = pl.program_id(1)

    @pl.when(i == 0)
    def _():
        _transpose_into(vt_ref, v_ref)
        _alibi_key_features(kf_ref, pkb_ref, LOG2E)

    span = t + WINDOW
    n_t = k_ref.shape[0] // t
    start = pl.multiple_of(jnp.clip(i - WINDOW // t, 0, n_t - span // t) * t, t)
    q = (q_ref[...].astype(F32) * (LOG2E * NSA_DH ** -0.5)).astype(BF16)
    lane = lax.broadcasted_iota(jnp.int32, (t, NSA_DH), 1)
    q4 = jnp.concatenate(
        [jnp.concatenate([q[:, r * NSA_DH:(r + 1) * NSA_DH],
                          jnp.where(lane < N_SPLIT, sc_ref[g * NSA_REP + r], 0.0).astype(BF16)], 1)
         for r in range(NSA_REP)], 0)
    kx = jnp.concatenate([k_ref[pl.ds(start, span), :], kf_ref[pl.ds(start, span), :]], 1)
    s = _kq(kx, q4)
    dif = (i * t + lax.broadcasted_iota(jnp.int32, (span, t), 1)
           - start - lax.broadcasted_iota(jnp.int32, (span, t), 0))
    ok = (dif >= 0) & (dif < WINDOW)
    vt = vt_ref[:, pl.ds(start, span)]
    gates = jax.nn.sigmoid(gate_ref[...].astype(F32))
    for r in range(NSA_REP):
        sr = jnp.where(ok, s[:, r * t:(r + 1) * t], NEG)
        e = jnp.exp2(sr - jnp.max(sr, 0, keepdims=True))
        d = jnp.sum(e, 0, keepdims=True)
        o_win = (jnp.dot(vt, e.astype(BF16), preferred_element_type=F32) * (1.0 / d)).T
        sl = slice(r * NSA_DH, (r + 1) * NSA_DH)

        def gcol(b):
            c0, c1 = 3 * r + b, 3 * (NSA_REP + r) + b
            return jnp.where(g == 0, gates[:, c0:c0 + 1], gates[:, c1:c1 + 1])

        o = (gcol(0) * ocmp_ref[:, sl].astype(F32) + gcol(1) * oslc_ref[:, sl].astype(F32)
             + gcol(2) * o_win)
        o_ref[:, sl] = o.astype(o_ref.dtype)


def _win_attn(z, o_cmp, o_slc, pos_b, scal):
    S = z.shape[0]
    t = 256
    gw = NSA_REP * NSA_DH
    kcol = Z_NKV // LANE + 8
    return pl.pallas_call(
        functools.partial(_win_attn_kernel, t=t),
        out_shape=jax.ShapeDtypeStruct((S, NSA_HEADS * NSA_DH), BF16),
        grid=(NSA_KV_HEADS, S // t),
        in_specs=[pl.BlockSpec(memory_space=pltpu.SMEM),
                  pl.BlockSpec((t, gw), lambda g, i: (i, Z_NQ // gw + g)),
                  pl.BlockSpec((S, NSA_DH), lambda g, i: (0, kcol + g)),
                  pl.BlockSpec((S, NSA_DH), lambda g, i: (0, kcol + 2 + g)),
                  pl.BlockSpec((S, LANE), lambda g, i: (0, 0)),
                  pl.BlockSpec((t, gw), lambda g, i: (i, g)),
                  pl.BlockSpec((t, gw), lambda g, i: (i, g)),
                  pl.BlockSpec((t, LANE), lambda g, i: (i, Z_NG // LANE))],
        out_specs=pl.BlockSpec((t, gw), lambda g, i: (i, g)),
        scratch_shapes=[pltpu.VMEM((NSA_DH, S), BF16), pltpu.VMEM((S, LANE), BF16)],
        compiler_params=_cparams(("arbitrary", "arbitrary")),
        name="nsa_win_attn",
    )(scal, z, z, z, pos_b, o_cmp, o_slc, z)


def kernel(x, c, positions, w_ada, b_ada, w_in, mla_q_norm, mla_kv_norm, mla_w_uq, mla_w_ukv, diff_lambda, diff_subln, nsa_cmp_w1, nsa_cmp_b1, nsa_cmp_w2, w_branch, w_out, ln1_g, ln1_b, w_ff1, w_ff2, ln2_g, ln2_b):
    B, S, D = x.shape
    L = w_ada.shape[0]
    assert B == 1 and D == D_MODEL and S % 512 == 0 and S // SLC_BLOCK >= 3

    w_in_x = _relayout_w_in(w_in)
    w_uq_x = _relayout_w_uq(mla_w_uq)
    w_ukv_b = mla_w_ukv.astype(BF16)
    w_out_b = w_out.astype(BF16)
    w_ff2_b = w_ff2.astype(BF16)

    slopes = 2.0 ** (-8.0 * np.arange(1, N_ALIBI_HEADS + 1, dtype=np.float64) / N_ALIBI_HEADS)
    s_diff, s_nsa = slopes[0::2], slopes[1::2]
    nsa_scal = jnp.asarray(s_nsa, F32)

    pos_f = positions.astype(F32)
    pos_col = pos_f.reshape(S, 1)
    pos_b = jnp.broadcast_to(pos_col, (S, LANE))
    pos_cmp_b = jnp.broadcast_to(jnp.pad(pos_f[0, CMP_BLOCK - 1::CMP_STRIDE], (0, 1))[:, None],
                                 (S // CMP_STRIDE, LANE))

    mod = _ada(c, w_ada, b_ada)
    rope_t = _rope_table(pos_col)
    xs = x.reshape(S, D)
    h = _lnmod(xs, mod, 0, 0, 1)

    for l in range(L):
        lam_init = 0.8 - 0.6 * math.exp(-0.3 * l)
        diff_scal = jnp.asarray(np.concatenate([s_diff * LOG2E, [lam_init]]), F32)

        z = _matmul(h, w_in_x, l, tm=1024, tn=1280, relu2=False, name="in_proj", w_transposed=True)

        qm, km, vm = _mla_proj(z, rope_t, mla_q_norm, mla_kv_norm, w_uq_x, w_ukv_b, l)
        o_mla = _mla_attn(qm, km, vm)

        o_diff = _diff_attn(z, pos_b, diff_scal, diff_lambda, diff_subln, l)

        xc = z[:, Z_NKV:Z_NKV + 4 * NSA_DH].reshape(S // CMP_STRIDE, CMP_STRIDE, 4, NSA_DH)
        xc = xc.transpose(2, 0, 1, 3).reshape(4, S // CMP_STRIDE, CMP_STRIDE * NSA_DH)
        cmp_kv = _compress(xc, nsa_cmp_w1, nsa_cmp_b1, nsa_cmp_w2, l)
        o_cmp, sel, act = _cmp_sel(z, cmp_kv, pos_cmp_b, nsa_scal)
        o_slc = _slc_attn(z, sel, act, pos_b, nsa_scal)
        o_nsa = _win_attn(z, o_cmp, o_slc, pos_b, nsa_scal)

        merged = _merge(o_mla, o_diff, o_nsa, w_branch, z, l)
        xs, h = _matmul_res_ln(merged, w_out_b, xs, mod, ln1_g, ln1_b, l, 2, ALPHA, "out_proj_ln",
                               next_mod=(l, 3, 4))
        u = _matmul(h, w_ff1, l, tm=1024, tn=1024, relu2=True, name="ff1")
        ff = _matmul(u, w_ff2_b, l, tm=512, tn=512, relu2=False, name="ff2", out_dtype=F32)
        if l + 1 < L:
            xs, h = _res_ln(ff, xs, mod, ln2_g, ln2_b, l, 5, ALPHA, "ff2_ln", next_mod=(l + 1, 0, 1))
        else:
            xs = _res_ln(ff, xs, mod, ln2_g, ln2_b, l, 5, ALPHA, "ff2_ln")

    return xs.reshape(B, S, D)
```
